```python
import math
import jax, jax.numpy as jnp
from jax import lax
import numpy as np

D_MODEL = 1024
BATCH = 8
SEQ = 2048
DEPTH = 1
DEC_BATCH = 128
DEC_SEQ = 1
PAST_LEN = 8192
PAGE_SIZE = 128

GDN_HEADS = 8
GDN_DK = 64
GDN_DV = 64
GDN_CONV = 4
GDN_CHUNK = 64
GDN_QKV = 2 * GDN_HEADS * GDN_DK + GDN_HEADS * GDN_DV
DSA_HEADS = 8
DSA_KV_HEADS = 4
DSA_DH = 64
IDX_HEADS = 4
IDX_DIM = 64
TOPK_MAX = 256
Q_BLOCK = 128
MEM_LEN = 256
MEM_HEADS = 4
MEM_DH = 128
N_BRANCH = 3
BRANCH_W = 512
N_EXPERTS = 32
TOP_K = 4
D_FF = 1024
SWIGLU_LIMIT = 7.0
SWIGLU_ALPHA = 1.702
MOE_BLOCK = 128
EPS = 1e-6

SPLITS = (GDN_QKV, GDN_HEADS * GDN_DV, GDN_HEADS, GDN_HEADS,
          DSA_HEADS * DSA_DH, DSA_KV_HEADS * DSA_DH, DSA_KV_HEADS * DSA_DH,
          IDX_HEADS * IDX_DIM, IDX_DIM, IDX_HEADS,
          MEM_HEADS * MEM_DH, N_BRANCH * D_MODEL)
D_IN = sum(SPLITS)

kernel_name = "hybrid_gdn_dsa_mem_moe_step"

F32 = jnp.float32


def rms_norm(x, g):
    xf = x.astype(F32)
    y = xf * lax.rsqrt(jnp.mean(xf * xf, axis=-1, keepdims=True) + EPS)
    return (y * g.astype(F32)).astype(x.dtype)


def l2_norm(x):
    xf = x.astype(F32)
    return xf * lax.rsqrt(jnp.sum(xf * xf, axis=-1, keepdims=True) + EPS)


def heads(a, n, d):
    return a.reshape(*a.shape[:-1], n, d)


def in_proj(x, norm_g, w_in):
    h = rms_norm(x, norm_g)
    cuts = [int(c) for c in np.cumsum(SPLITS)[:-1]]
    return jnp.split(h @ w_in, cuts, axis=-1)


def causal_conv(x, conv_state, w):
    T = x.shape[1]
    xc = jnp.concatenate([conv_state.astype(x.dtype), x], axis=1)
    y = sum(w[i] * xc[:, i:i + T] for i in range(GDN_CONV))
    return jax.nn.silu(y), xc[:, T:]


def gdn_chunked(q, k, v, beta, g):
    B, T, H, DK = q.shape
    DV = v.shape[-1]
    C = GDN_CHUNK
    N = T // C

    def chunks(a):
        a = a.reshape(B, N, C, H, *a.shape[3:])
        return jnp.moveaxis(a, (1, 3), (0, 2))

    q, k, v, beta, g = chunks(q), chunks(k), chunks(v), chunks(beta), chunks(g)
    G = jnp.cumsum(g, axis=-1)
    incl = jnp.tril(jnp.ones((C, C), bool))
    strict = jnp.tril(jnp.ones((C, C), bool), -1)
    gamma = jnp.exp(jnp.where(incl, G[..., :, None] - G[..., None, :], -jnp.inf))
    kb = k * beta[..., None]
    m = jnp.where(strict, jnp.einsum('nbhcd,nbhed->nbhce', kb, k) * gamma, 0.0)
    a_mat = m + jnp.eye(C, dtype=m.dtype)
    rhs = jnp.concatenate([v * beta[..., None], kb * jnp.exp(G)[..., None]], axis=-1)
    sol = lax.linalg.triangular_solve(a_mat, rhs, left_side=True, lower=True, unit_diagonal=True)
    u, w = sol[..., :DV], sol[..., DV:]
    qk = jnp.einsum('nbhcd,nbhed->nbhce', q, k) * gamma
    qg = q * jnp.exp(G)[..., None]
    kd = k * jnp.exp(G[..., -1:] - G)[..., None]
    decay_last = jnp.exp(G[..., -1])

    def step(S, xs):
        qg_c, kd_c, u_c, w_c, qk_c, d_c = xs
        vn = u_c - jnp.einsum('bhck,bhkv->bhcv', w_c, S)
        o = jnp.einsum('bhck,bhkv->bhcv', qg_c, S) + jnp.einsum('bhce,bhev->bhcv', qk_c, vn)
        S = S * d_c[..., None, None] + jnp.einsum('bhck,bhcv->bhkv', kd_c, vn)
        return S, o

    S0 = jnp.zeros((B, H, DK, DV), F32)
    S, o = lax.scan(step, S0, (qg, kd, u, w, qk, decay_last))
    o = jnp.moveaxis(o, (0, 2), (1, 3)).reshape(B, T, H, DV)
    return o, S


def gdn_recurrent(q, k, v, beta, g, S0):
    def step(S, xs):
        qt, kt, vt, bt, gt = xs
        S = S * jnp.exp(gt)[..., None, None]
        vn = bt[..., None] * (vt - jnp.einsum('bhk,bhkv->bhv', kt, S))
        S = S + jnp.einsum('bhk,bhv->bhkv', kt, vn)
        return S, jnp.einsum('bhk,bhkv->bhv', qt, S)

    xs = tuple(jnp.moveaxis(a, 1, 0) for a in (q, k, v, beta, g))
    S, o = lax.scan(step, S0.astype(F32), xs)
    return jnp.moveaxis(o, 0, 1), S


def gdn_branch(qkv, z, b, a, conv_state, ssm_state, conv_w, a_log, dt_bias, norm_w):
    B, T, _ = qkv.shape
    y, new_conv = causal_conv(qkv, conv_state, conv_w)
    q, k, v = jnp.split(y, [GDN_HEADS * GDN_DK, 2 * GDN_HEADS * GDN_DK], axis=-1)
    q = l2_norm(heads(q, GDN_HEADS, GDN_DK)) * GDN_DK ** -0.5
    k = l2_norm(heads(k, GDN_HEADS, GDN_DK))
    v = heads(v, GDN_HEADS, GDN_DV).astype(F32)
    beta = jax.nn.sigmoid(b.astype(F32))
    g = -jnp.exp(a_log.astype(F32)) * jax.nn.softplus(a.astype(F32) + dt_bias.astype(F32))
    if ssm_state is None:
        o, S = gdn_chunked(q, k, v, beta, g)
    else:
        o, S = gdn_recurrent(q, k, v, beta, g, ssm_state)
    o = rms_norm(o, norm_w) * jax.nn.silu(heads(z, GDN_HEADS, GDN_DV).astype(F32))
    return o.reshape(B, T, -1).astype(qkv.dtype), new_conv, S


def index_scores(qi, ki, wi):
    s = jnp.einsum('bthd,bld->bthl', qi.astype(F32), ki.astype(F32))
    wt = wi.astype(F32) * (IDX_HEADS ** -0.5 * IDX_DIM ** -0.5)
    return jnp.einsum('bthl,bth->btl', jax.nn.relu(s), wt)


def sparse_attend(q, ksel, vsel, valid):
    B, T = q.shape[:2]
    qg = q.reshape(B, T, DSA_KV_HEADS, DSA_HEADS // DSA_KV_HEADS, DSA_DH)
    s = jnp.einsum('btkgd,btnkd->btkgn', qg, ksel).astype(F32) * DSA_DH ** -0.5
    s = jnp.where(valid[:, :, None, None, :], s, -jnp.inf)
    p = jax.nn.softmax(s, axis=-1).astype(vsel.dtype)
    o = jnp.einsum('btkgn,btnkd->btkgd', p, vsel)
    return o.reshape(B, T, DSA_HEADS * DSA_DH)


def _rows(a, i):
    return a[i]


gather_rows = jax.vmap(_rows)


def dsa_prompt(q, k, v, qi, ki, wi):
    B, S = q.shape[:2]
    n_sel = min(TOPK_MAX, S // 4)
    spos = jnp.arange(S)

    def block(t0):
        qb = lax.dynamic_slice_in_dim(q, t0, Q_BLOCK, axis=1)
        qib = lax.dynamic_slice_in_dim(qi, t0, Q_BLOCK, axis=1)
        wib = lax.dynamic_slice_in_dim(wi, t0, Q_BLOCK, axis=1)
        tpos = t0 + jnp.arange(Q_BLOCK)
        sc = index_scores(qib, ki, wib)
        sc = jnp.where(spos[None, None, :] <= tpos[None, :, None], sc, -jnp.inf)
        _, idx = lax.top_k(sc, n_sel)
        valid = idx <= tpos[None, :, None]
        return sparse_attend(qb, gather_rows(k, idx), gather_rows(v, idx), valid)

    out = lax.map(block, jnp.arange(S // Q_BLOCK) * Q_BLOCK)
    return jnp.moveaxis(out, 0, 1).reshape(B, S, -1)


def dsa_sample(q, k, v, qi, ki, wi, cache_k, cache_v, cache_ki, page_table, layer):
    Bd, T = q.shape[:2]
    n_pages = page_table.shape[1]
    past = n_pages * PAGE_SIZE
    L = past + T
    n_sel = min(TOPK_MAX, L // 4)
    ki_past = cache_ki[layer, page_table].reshape(Bd, past, IDX_DIM)
    ki_all = jnp.concatenate([ki_past.astype(ki.dtype), ki], axis=1)
    tpos = past + jnp.arange(T)
    sc = index_scores(qi, ki_all, wi)
    sc = jnp.where(jnp.arange(L)[None, None, :] <= tpos[None, :, None], sc, -jnp.inf)
    _, idx = lax.top_k(sc, n_sel)
    valid = idx <= tpos[None, :, None]
    in_past = (idx < past)[..., None, None]
    pidx = jnp.minimum(idx, past - 1)
    phys = page_table[jnp.arange(Bd)[:, None, None], pidx // PAGE_SIZE]
    off = pidx % PAGE_SIZE
    nidx = jnp.clip(idx - past, 0, T - 1)
    ksel = jnp.where(in_past, cache_k[layer, phys, off].astype(k.dtype), gather_rows(k, nidx))
    vsel = jnp.where(in_past, cache_v[layer, phys, off].astype(v.dtype), gather_rows(v, nidx))
    return sparse_attend(q, ksel, vsel, valid)


def mem_kv(mem, norm_g, w_kv):
    B, M = mem.shape[:2]
    kv = rms_norm(mem, norm_g) @ w_kv
    mk, mv = jnp.split(kv, 2, axis=-1)
    return heads(mk, MEM_HEADS, MEM_DH), heads(mv, MEM_HEADS, MEM_DH)


def mem_attend(q, mk, mv):
    B, T = q.shape[:2]
    qh = heads(q, MEM_HEADS, MEM_DH)
    s = jnp.einsum('bthd,bmhd->bthm', qh, mk.astype(q.dtype)).astype(F32) * MEM_DH ** -0.5
    p = jax.nn.softmax(s, axis=-1).astype(q.dtype)
    return jnp.einsum('bthm,bmhd->bthd', p, mv.astype(q.dtype)).reshape(B, T, -1)


def merge_out(x, o_gdn, o_dsa, o_mem, gates, w_branch, w_out):
    B, T, _ = x.shape
    br = jnp.einsum('nbtc,ncd->nbtd', jnp.stack([o_gdn, o_dsa, o_mem]), w_branch)
    gt = jax.nn.sigmoid(gates.reshape(B, T, N_BRANCH, D_MODEL).astype(F32)).astype(x.dtype)
    merged = jnp.einsum('btnd,nbtd->btd', gt, br)
    return x + merged @ w_out


def moe(h, w_router, b_router, w_gu, b_gu, w_dn, b_dn):
    shp = h.shape
    x = h.reshape(-1, D_MODEL)
    N = x.shape[0]
    NK = N * TOP_K
    logits = (x @ w_router + b_router).astype(F32)
    top_val, top_e = lax.top_k(logits, TOP_K)
    probs = jax.nn.softmax(top_val, axis=-1)
    flat_e = top_e.reshape(-1)
    order = jnp.argsort(flat_e)
    e_sorted = flat_e[order]
    tok_sorted = order // TOP_K
    p_sorted = probs.reshape(-1)[order]
    counts = jnp.bincount(flat_e, length=N_EXPERTS)
    padded = (counts + MOE_BLOCK - 1) // MOE_BLOCK * MOE_BLOCK
    pad_end = jnp.cumsum(padded)
    pad_start = pad_end - padded
    start = jnp.cumsum(counts) - counts
    dest = pad_start[e_sorted] + jnp.arange(NK) - start[e_sorted]
    n_blocks = -(-NK // MOE_BLOCK) + N_EXPERTS
    slot_tok = jnp.zeros(n_blocks * MOE_BLOCK, jnp.int32).at[dest].set(tok_sorted.astype(jnp.int32))
    block_e = jnp.minimum(jnp.searchsorted(pad_end, jnp.arange(n_blocks) * MOE_BLOCK, side='right'), N_EXPERTS - 1)
    xb = x[slot_tok].reshape(n_blocks, MOE_BLOCK, D_MODEL)

    def expert(args):
        xe, e = args
        gu = xe @ w_gu[e] + b_gu[e]
        gate, up = jnp.split(gu, 2, axis=-1)
        gate = jnp.minimum(gate, SWIGLU_LIMIT)
        up = jnp.clip(up, -SWIGLU_LIMIT, SWIGLU_LIMIT)
        glu = gate * jax.nn.sigmoid(SWIGLU_ALPHA * gate)
        return ((up + 1.0) * glu) @ w_dn[e] + b_dn[e]

    yb = lax.map(expert, (xb, block_e)).reshape(n_blocks * MOE_BLOCK, D_MODEL)
    y = jnp.zeros_like(x).at[tok_sorted].add(yb[dest] * p_sorted[:, None].astype(x.dtype))
    return y.reshape(shp)


def setup_inputs(seed: int = 0) -> dict:
    key = jax.random.key(seed)
    ks = iter(jax.random.split(key, 32))

    def nrm(shape, scale=1.0):
        return jax.random.normal(next(ks), shape, F32) * scale

    n_pages = PAST_LEN // PAGE_SIZE
    n_used = DEC_BATCH * n_pages
    n_pool = n_used + (n_used + 3) // 4
    x_prompt = nrm((BATCH, SEQ, D_MODEL))
    x_sample = nrm((DEC_BATCH, DEC_SEQ, D_MODEL))
    cache_k = nrm((DEPTH, n_pool, PAGE_SIZE, DSA_KV_HEADS, DSA_DH))
    cache_v = nrm((DEPTH, n_pool, PAGE_SIZE, DSA_KV_HEADS, DSA_DH))
    cache_k_idx = nrm((DEPTH, n_pool, PAGE_SIZE, IDX_DIM))
    cache_mem_k = nrm((DEPTH, DEC_BATCH, MEM_LEN, MEM_HEADS, MEM_DH))
    cache_mem_v = nrm((DEPTH, DEC_BATCH, MEM_LEN, MEM_HEADS, MEM_DH))
    state_gdn = nrm((DEPTH, DEC_BATCH, GDN_HEADS, GDN_DK, GDN_DV), 0.3)
    state_conv = nrm((DEPTH, DEC_BATCH, GDN_CONV - 1, GDN_QKV))
    page_table = jax.random.permutation(next(ks), n_pool)[:n_used].reshape(DEC_BATCH, n_pages).astype(jnp.int32)
    mem_prompt = nrm((BATCH, MEM_LEN, D_MODEL))
    norm_attn = 1.0 + nrm((DEPTH, D_MODEL), 0.02)
    w_in = nrm((DEPTH, D_MODEL, D_IN), D_MODEL ** -0.5)
    conv_w = nrm((DEPTH, GDN_CONV, GDN_QKV), 0.5)
    gdn_a_log = jnp.log(jax.random.uniform(next(ks), (DEPTH, GDN_HEADS), F32, 1.0, 16.0))
    dt = jnp.exp(jax.random.uniform(next(ks), (DEPTH, GDN_HEADS), F32, math.log(1e-3), math.log(1e-1)))
    gdn_dt_bias = dt + jnp.log(-jnp.expm1(-dt))
    gdn_norm = 1.0 + nrm((DEPTH, GDN_DV), 0.02)
    norm_mem = 1.0 + nrm((DEPTH, D_MODEL), 0.02)
    w_mem_kv = nrm((DEPTH, D_MODEL, 2 * MEM_HEADS * MEM_DH), D_MODEL ** -0.5)
    w_branch = nrm((DEPTH, N_BRANCH, BRANCH_W, D_MODEL), BRANCH_W ** -0.5)
    w_out = nrm((DEPTH, D_MODEL, D_MODEL), D_MODEL ** -0.5)
    norm_ffn = 1.0 + nrm((DEPTH, D_MODEL), 0.02)
    w_router = nrm((DEPTH, D_MODEL, N_EXPERTS), D_MODEL ** -0.5)
    b_router = nrm((DEPTH, N_EXPERTS), 0.01)
    w_gate_up = nrm((DEPTH, N_EXPERTS, D_MODEL, 2 * D_FF), D_MODEL ** -0.5)
    b_gate_up = nrm((DEPTH, N_EXPERTS, 2 * D_FF), 0.01)
    w_down = nrm((DEPTH, N_EXPERTS, D_FF, D_MODEL), D_FF ** -0.5)
    b_down = nrm((DEPTH, N_EXPERTS, D_MODEL), 0.01)
    norm_final = 1.0 + nrm((D_MODEL,), 0.02)
    return {"x_prompt": x_prompt, "x_sample": x_sample, "cache_k": cache_k, "cache_v": cache_v,
            "cache_k_idx": cache_k_idx, "cache_mem_k": cache_mem_k, "cache_mem_v": cache_mem_v,
            "state_gdn": state_gdn, "state_conv": state_conv, "page_table": page_table,
            "mem_prompt": mem_prompt, "norm_attn": norm_attn, "w_in": w_in, "conv_w": conv_w,
            "gdn_a_log": gdn_a_log, "gdn_dt_bias": gdn_dt_bias, "gdn_norm": gdn_norm,
            "norm_mem": norm_mem, "w_mem_kv": w_mem_kv, "w_branch": w_branch, "w_out": w_out,
            "norm_ffn": norm_ffn, "w_router": w_router, "b_router": b_router,
            "w_gate_up": w_gate_up, "b_gate_up": b_gate_up, "w_down": w_down, "b_down": b_down,
            "norm_final": norm_final}


def reference(x_prompt, x_sample, cache_k, cache_v, cache_k_idx, cache_mem_k, cache_mem_v,
              state_gdn, state_conv, page_table, mem_prompt, norm_attn, w_in, conv_w,
              gdn_a_log, gdn_dt_bias, gdn_norm, norm_mem, w_mem_kv, w_branch, w_out,
              norm_ffn, w_router, b_router, w_gate_up, b_gate_up, w_down, b_down, norm_final):
    xp, xs = x_prompt, x_sample
    B = xp.shape[0]
    kp, vp, ip, mkp, mvp, sp, cp = [], [], [], [], [], [], []
    ksl, vsl, isl, ssl, csl = [], [], [], [], []
    for l in range(DEPTH):
        (g_qkv, g_z, g_b, g_a, d_q, d_k, d_v, i_q, i_k, i_w, m_q, gates) = in_proj(xp, norm_attn[l], w_in[l])
        conv0 = jnp.zeros((B, GDN_CONV - 1, GDN_QKV), xp.dtype)
        o_g, conv_new, ssm_new = gdn_branch(g_qkv, g_z, g_b, g_a, conv0, None, conv_w[l],
                                            gdn_a_log[l], gdn_dt_bias[l], gdn_norm[l])
        k_h = heads(d_k, DSA_KV_HEADS, DSA_DH)
        v_h = heads(d_v, DSA_KV_HEADS, DSA_DH)
        o_d = dsa_prompt(heads(d_q, DSA_HEADS, DSA_DH), k_h, v_h, heads(i_q, IDX_HEADS, IDX_DIM), i_k, i_w)
        mk, mv = mem_kv(mem_prompt, norm_mem[l], w_mem_kv[l])
        o_m = mem_attend(m_q, mk, mv)
        xp = merge_out(xp, o_g, o_d, o_m, gates, w_branch[l], w_out[l])
        xp = xp + moe(rms_norm(xp, norm_ffn[l]), w_router[l], b_router[l], w_gate_up[l], b_gate_up[l], w_down[l], b_down[l])
        kp.append(k_h); vp.append(v_h); ip.append(i_k); mkp.append(mk); mvp.append(mv)
        sp.append(ssm_new); cp.append(conv_new)
        (g_qkv, g_z, g_b, g_a, d_q, d_k, d_v, i_q, i_k, i_w, m_q, gates) = in_proj(xs, norm_attn[l], w_in[l])
        o_g, conv_new, ssm_new = gdn_branch(g_qkv, g_z, g_b, g_a, state_conv[l], state_gdn[l], conv_w[l],
                                            gdn_a_log[l], gdn_dt_bias[l], gdn_norm[l])
        k_h = heads(d_k, DSA_KV_HEADS, DSA_DH)
        v_h = heads(d_v, DSA_KV_HEADS, DSA_DH)
        o_d = dsa_sample(heads(d_q, DSA_HEADS, DSA_DH), k_h, v_h, heads(i_q, IDX_HEADS, IDX_DIM), i_k, i_w,
                         cache_k, cache_v, cache_k_idx, page_table, l)
        o_m = mem_attend(m_q, cache_mem_k[l], cache_mem_v[l])
        xs = merge_out(xs, o_g, o_d, o_m, gates, w_branch[l], w_out[l])
        xs = xs + moe(rms_norm(xs, norm_ffn[l]), w_router[l], b_router[l], w_gate_up[l], b_gate_up[l], w_down[l], b_down[l])
        ksl.append(k_h); vsl.append(v_h); isl.append(i_k); ssl.append(ssm_new); csl.append(conv_new)
    y_prompt = rms_norm(xp, norm_final)
    y_sample = rms_norm(xs, norm_final)
    k_prompt, v_prompt, kidx_prompt = jnp.stack(kp), jnp.stack(vp), jnp.stack(ip)
    memk_prompt, memv_prompt = jnp.stack(mkp), jnp.stack(mvp)
    ssm_prompt, conv_prompt = jnp.stack(sp), jnp.stack(cp)
    k_sample, v_sample, kidx_sample = jnp.stack(ksl), jnp.stack(vsl), jnp.stack(isl)
    ssm_sample, conv_sample = jnp.stack(ssl), jnp.stack(csl)
    return (y_prompt, y_sample, k_prompt, v_prompt, kidx_prompt, memk_prompt, memv_prompt,
            ssm_prompt, conv_prompt, k_sample, v_sample, kidx_sample, ssm_sample, conv_sample)
```

```python
import functools

import numpy as np
import jax
import jax.numpy as jnp
from jax import lax
from jax.experimental import pallas as pl
from jax.experimental.pallas import tpu as pltpu

F32 = jnp.float32
BF16 = jnp.bfloat16
I32 = jnp.int32

EPS = 1e-6
GDN_HEADS = 8
GDN_DK = 64
GDN_DV = 64
GDN_CONV = 4
GDN_CHUNK = 64
GDN_QKV = 2 * GDN_HEADS * GDN_DK + GDN_HEADS * GDN_DV
DSA_HEADS = 8
DSA_KV_HEADS = 4
DSA_DH = 64
IDX_HEADS = 4
IDX_DIM = 64
TOPK_MAX = 256
Q_BLOCK = 128
MEM_HEADS = 4
MEM_DH = 128
N_BRANCH = 3
BRANCH_W = 512
N_EXPERTS = 32
TOP_K = 4
SWIGLU_LIMIT = 7.0
SWIGLU_ALPHA = 1.702
PAGE_SIZE = 128
LANES = 128

_SPLITS = (GDN_QKV, GDN_HEADS * GDN_DV, GDN_HEADS, GDN_HEADS,
           DSA_HEADS * DSA_DH, DSA_KV_HEADS * DSA_DH, DSA_KV_HEADS * DSA_DH,
           IDX_HEADS * IDX_DIM, IDX_DIM, IDX_HEADS,
           MEM_HEADS * MEM_DH)
_PROJ_SPLITS = (GDN_QKV, 512, 512, 256, 256, 256, 512, LANES)
_SM_IK = 0
_SM_IW = IDX_DIM
_SM_GB = _SM_IW + IDX_HEADS
_SM_GA = _SM_GB + GDN_HEADS
_SM_END = _SM_GA + GDN_HEADS

VMEM_LIMIT_BYTES = 56 * 1024 * 1024


def _cparams(*sem):
    return pltpu.CompilerParams(dimension_semantics=sem, vmem_limit_bytes=VMEM_LIMIT_BYTES)


def _bf(x):
    return x.astype(BF16)


def _dot(a, b):
    return jnp.dot(_bf(a), _bf(b), preferred_element_type=F32)


_NT = (((1,), (1,)), ((), ()))


def _dot_nt(a, b):
    return lax.dot_general(_bf(a), _bf(b), _NT, preferred_element_type=F32)


def _split2(x):
    hi = x.astype(BF16)
    lo = (x - hi.astype(F32)).astype(BF16)
    return hi, lo


def _split3(x):
    hi = x.astype(BF16)
    r = x - hi.astype(F32)
    mid = r.astype(BF16)
    lo = (r - mid.astype(F32)).astype(BF16)
    return hi, mid, lo


def _dot_nt3(a, b):
    ah, al = _split2(a)
    bh, bl = _split2(b)
    d = lambda x, y: lax.dot_general(x, y, _NT, preferred_element_type=F32)
    return d(ah, bh) + (d(ah, bl) + d(al, bh))


def _dot3(a, b):
    ah, al = _split2(a)
    bh, bl = _split2(b)
    d = lambda x, y: jnp.dot(x, y, preferred_element_type=F32)
    return d(ah, bh) + (d(ah, bl) + d(al, bh))


def _dot_exact01(a, b01):
    hi, mid, lo = _split3(a)
    d = lambda x: jnp.dot(x, b01, preferred_element_type=F32)
    return d(hi) + (d(mid) + d(lo))


def _rms(x, g):
    return x * lax.rsqrt(jnp.mean(x * x, axis=-1, keepdims=True) + EPS) * g


def _sigmoid(x):
    return 1.0 / (1.0 + jnp.exp(-x))


def _silu(x):
    return x * _sigmoid(x)


def _softplus(x):
    return jnp.maximum(x, 0.0) + jnp.log(1.0 + jnp.exp(-jnp.abs(x)))


def _iota(shape, axis):
    return lax.broadcasted_iota(I32, shape, axis)


def _norm_matmul_kernel(x_ref, g_ref, w_ref, *o_refs, splits):
    hb = _rms(x_ref[...], g_ref[...]).astype(BF16)
    off = 0
    for o_ref, n in zip(o_refs, splits):
        o_ref[...] = jnp.dot(hb, w_ref[:, off:off + n], preferred_element_type=F32)
        off += n


def _norm_matmul(x, g, w, splits, tm):
    n, d = x.shape
    return pl.pallas_call(
        functools.partial(_norm_matmul_kernel, splits=splits),
        grid=(n // tm,),
        in_specs=[pl.BlockSpec((tm, d), lambda i: (i, 0)),
                  pl.BlockSpec((1, d), lambda i: (0, 0)),
                  pl.BlockSpec((d, sum(splits)), lambda i: (0, 0))],
        out_specs=[pl.BlockSpec((tm, s), lambda i: (i, 0)) for s in splits],
        out_shape=[jax.ShapeDtypeStruct((n, s), F32) for s in splits],
        compiler_params=_cparams("parallel"),
        name="norm_matmul",
    )(x, g.reshape(1, d), w)


def _gdn_prompt_kernel(q_ref, k_ref, v_ref, qh_ref, kh_ref, vh_ref, z_ref, a_ref, b_ref,
                       cw_ref, alog_ref, dtb_ref, nw_ref, o_ref, s_ref,
                       xq, xk, xv, s_scr, *, tt):
    ti = pl.program_id(1)
    H, C = GDN_HEADS, GDN_CHUNK
    HALO = 8

    @pl.when(ti == 0)
    def _():
        s_scr[...] = jnp.zeros_like(s_scr)

    keep = (ti > 0).astype(F32)
    for idx, (src, halo, dst) in enumerate(((q_ref, qh_ref, xq), (k_ref, kh_ref, xk), (v_ref, vh_ref, xv))):
        dst[:, 0:HALO, :] = halo[0] * keep
        dst[:, HALO:HALO + tt, :] = src[0]
        w = cw_ref[idx]
        y = w[:, 0:1, :] * dst[:, pl.ds(HALO - 3, tt), :]
        for i in range(1, GDN_CONV):
            y = y + w[:, i:i + 1, :] * dst[:, pl.ds(HALO - 3 + i, tt), :]
        y = _silu(y)
        if idx == 0:
            y = y * lax.rsqrt(jnp.sum(y * y, axis=-1, keepdims=True) + EPS) * (GDN_DK ** -0.5)
        elif idx == 1:
            y = y * lax.rsqrt(jnp.sum(y * y, axis=-1, keepdims=True) + EPS)
        dst[:, HALO:HALO + tt, :] = y

    ri = _iota((C, C), 0)
    ci = _iota((C, C), 1)
    incl = ci <= ri
    strict = ci < ri
    eye = ci == ri
    r2 = _iota((C, 2 * C), 0)
    c2 = _iota((C, 2 * C), 1)
    rhs01 = jnp.where((c2 >= C) | (r2 > c2), 1.0, 0.0).astype(BF16)
    alog = alog_ref[...]
    dtb = dtb_ref[...]
    nw = nw_ref[...]

    def chunk(c, carry):
        g_all = -jnp.exp(alog) * _softplus(a_ref[0, c] + dtb)
        beta_all = _sigmoid(b_ref[0, c])
        r0 = pl.multiple_of(HALO + c * C, 8)
        o0 = pl.multiple_of(c * C, 8)
        for h in range(H):
            qc = xq[h, pl.ds(r0, C), :]
            kc = xk[h, pl.ds(r0, C), :]
            vc = xv[h, pl.ds(r0, C), :]
            g_row = g_all[h:h + 1, :]
            b_row = beta_all[h:h + 1, :]
            lhs = jnp.concatenate([jnp.where(incl, g_row, 0.0), jnp.where(eye, b_row, 0.0)], axis=0)
            res = _dot_exact01(lhs, rhs01)
            dmat = res[:C, :C]
            gcol = res[:C, C:]
            bcol = res[C:, C:]
            gamma = jnp.where(incl, jnp.exp(dmat), 0.0)
            eg = jnp.exp(gcol)
            glast = gcol[C - 1:C, :]
            kb = kc * bcol
            vb = vc * bcol
            kk = _dot_nt(jnp.concatenate([kb, qc], axis=0), kc)
            m = jnp.where(strict, kk[:C] * gamma, 0.0)
            qk = kk[C:] * gamma
            y = jnp.concatenate([vb, kb * eg], axis=1)
            y = y - _dot3(m, y)
            p = m
            for _ in range(5):
                p = _dot3(p, p)
                y = y + _dot3(p, y)
            u = y[:, :GDN_DV]
            w = y[:, GDN_DV:]
            s = s_scr[h]
            ws = _dot(jnp.concatenate([w, qc * eg], axis=0), s)
            vn = u - ws[:C]
            o = ws[C:] + _dot(qk, vn)
            kd = kc * jnp.exp(glast - gcol)
            s_scr[h] = s * jnp.exp(glast) + _dot(kd.T, vn)
            zc = z_ref[0, h, pl.ds(o0, C), :]
            o_ref[0, h, pl.ds(o0, C), :] = _rms(o, nw) * _silu(zc)
        return carry

    lax.fori_loop(0, tt // C, chunk, 0)
    s_ref[0] = s_scr[...]


def _gdn_prompt(q, k, v, z, a, b, cw, alog, dtb, nw, tt):
    B, H, T, DK = q.shape
    nct = tt // GDN_CHUNK
    main = pl.BlockSpec((1, H, tt, DK), lambda bi, ti: (bi, 0, ti, 0))
    halo = pl.BlockSpec((1, H, 8, DK), lambda bi, ti: (bi, 0, jnp.maximum(ti * (tt // 8) - 1, 0), 0))
    chunked = pl.BlockSpec((1, nct, H, GDN_CHUNK), lambda bi, ti: (bi, ti, 0, 0))
    full = lambda shape: pl.BlockSpec(shape, lambda bi, ti: (0,) * len(shape))
    return pl.pallas_call(
        functools.partial(_gdn_prompt_kernel, tt=tt),
        grid=(B, T // tt),
        in_specs=[main, main, main, halo, halo, halo, main, chunked, chunked,
                  full((3, H, GDN_CONV, DK)), full((H, 1)), full((H, 1)), full((1, GDN_DV))],
        out_specs=[main, pl.BlockSpec((1, H, GDN_DK, GDN_DV), lambda bi, ti: (bi, 0, 0, 0))],
        out_shape=[jax.ShapeDtypeStruct((B, H, T, GDN_DV), F32),
                   jax.ShapeDtypeStruct((B, H, GDN_DK, GDN_DV), F32)],
        scratch_shapes=[pltpu.VMEM((H, 8 + tt, DK), F32)] * 3 + [pltpu.VMEM((H, GDN_DK, GDN_DV), F32)],
        compiler_params=_cparams("parallel", "arbitrary"),
        name="gdn_prompt",
    )(q, k, v, q, k, v, z, a, b, cw, alog, dtb, nw)


def _gdn_sample_kernel(q_ref, k_ref, v_ref, cq_ref, ck_ref, cv_ref, wq_ref, wk_ref, wv_ref,
                       z_ref, a_ref, b_ref, alog_ref, dtb_ref, nw_ref, s_ref, o_ref, so_ref):
    def conv(x_ref, c_ref, w_ref):
        y = w_ref[GDN_CONV - 1] * x_ref[...]
        for i in range(GDN_CONV - 1):
            y = y + w_ref[i] * c_ref[i]
        return _silu(y)

    q = conv(q_ref, cq_ref, wq_ref)
    q = q * lax.rsqrt(jnp.sum(q * q, axis=0, keepdims=True) + EPS) * (GDN_DK ** -0.5)
    k = conv(k_ref, ck_ref, wk_ref)
    k = k * lax.rsqrt(jnp.sum(k * k, axis=0, keepdims=True) + EPS)
    v = conv(v_ref, cv_ref, wv_ref)
    beta = _sigmoid(b_ref[0])
    g = -jnp.exp(alog_ref[0]) * _softplus(a_ref[0] + dtb_ref[0])
    dec = jnp.exp(g)
    ks = jnp.zeros_like(v)
    for i in range(GDN_DK):
        ks = ks + k[i:i + 1, :] * (s_ref[0, i] * dec)
    vn = beta * (v - ks)
    o = jnp.zeros_like(v)
    for i in range(GDN_DK):
        sn = s_ref[0, i] * dec + k[i:i + 1, :] * vn
        so_ref[0, i] = sn
        o = o + q[i:i + 1, :] * sn
    on = o * lax.rsqrt(jnp.mean(o * o, axis=0, keepdims=True) + EPS) * nw_ref[...]
    o_ref[...] = on * _silu(z_ref[...])


def _gdn_sample(qkv_t, conv_t, cw_t, z_t, a_t, b_t, alog_t, dtb_t, nw_t, s_t):
    nb = qkv_t.shape[1]
    H, DK, DV = GDN_HEADS, GDN_DK, GDN_DV
    row = lambda off: pl.BlockSpec((DK, nb), lambda h: (off + h, 0))
    crow = lambda off: pl.BlockSpec((GDN_CONV - 1, DK, nb), lambda h: (0, off + h, 0))
    wrow = lambda off: pl.BlockSpec((GDN_CONV, DK, nb), lambda h: (0, off + h, 0))
    per_head = pl.BlockSpec((1, 1, nb), lambda h: (h, 0, 0))
    return pl.pallas_call(
        _gdn_sample_kernel,
        grid=(H,),
        in_specs=[row(0), row(H), row(2 * H), crow(0), crow(H), crow(2 * H), wrow(0), wrow(H), wrow(2 * H),
                  row(0), per_head, per_head, per_head, per_head,
                  pl.BlockSpec((DV, nb), lambda h: (0, 0)),
                  pl.BlockSpec((1, DK, DV, nb), lambda h: (h, 0, 0, 0))],
        out_specs=[row(0), pl.BlockSpec((1, DK, DV, nb), lambda h: (h, 0, 0, 0))],
        out_shape=[jax.ShapeDtypeStruct((H * DV, nb), F32), jax.ShapeDtypeStruct((H, DK, DV, nb), F32)],
        compiler_params=_cparams("parallel"),
        name="gdn_sample",
    )(qkv_t, qkv_t, qkv_t, conv_t, conv_t, conv_t, cw_t, cw_t, cw_t, z_t, a_t, b_t, alog_t, dtb_t, nw_t, s_t)


def _count_ge(x, thr):
    return jnp.sum(jnp.where(x >= thr, 1.0, 0.0), axis=1, keepdims=True)


def _topk_mask(score, valid, k):
    rows, width = score.shape
    kf = float(k)
    x = jnp.where(valid, score, -jnp.inf)
    validf = jnp.where(valid, 1.0, 0.0)
    nvalid = jnp.sum(validf, axis=1, keepdims=True)
    few = nvalid <= kf
    mx = jnp.max(x, axis=1, keepdims=True)
    mn = jnp.min(jnp.where(valid, score, jnp.inf), axis=1, keepdims=True)
    lo0 = mn
    hi0 = mx + (jnp.abs(mx) * 1e-6 + 1e-30)
    chi0 = jnp.zeros_like(mx)

    def step(_, carry):
        lo, hi, chi = carry
        mid = 0.5 * lo + 0.5 * hi
        c = _count_ge(x, mid)
        ge = c >= kf
        return jnp.where(ge, mid, lo), jnp.where(ge, hi, mid), jnp.where(ge, chi, c)

    def finish(hi, chi):
        tau = jnp.max(jnp.where(x < hi, x, -jnp.inf), axis=1, keepdims=True)
        ceq = jnp.sum(jnp.where(x == tau, 1.0, 0.0), axis=1, keepdims=True)
        return tau, ceq

    def not_done(hi, chi):
        tau, ceq = finish(hi, chi)
        bad = jnp.where(few | (chi + ceq >= kf), 0.0, 1.0)
        return jnp.max(bad) > 0.0

    lo, hi, chi = lax.fori_loop(0, 24, step, (lo0, hi0, chi0))

    def w_cond(carry):
        it, _, hi, chi = carry
        return jnp.logical_and(it < 48, not_done(hi, chi))

    def w_body(carry):
        it, lo, hi, chi = carry
        lo, hi, chi = lax.fori_loop(0, 8, step, (lo, hi, chi))
        return it + 1, lo, hi, chi

    _, lo, hi, chi = lax.while_loop(w_cond, w_body, (jnp.int32(0), lo, hi, chi))
    tau, ceq = finish(hi, chi)
    need = kf - chi
    gtf = jnp.where(x > tau, 1.0, 0.0)
    eqf = jnp.where(x == tau, 1.0, 0.0)
    has_tie = jnp.max(jnp.where(few | (ceq <= need), 0.0, 1.0)) > 0.0

    def tie_path(_):
        nchunk = width // LANES
        su = jnp.where(_iota((LANES, LANES), 0) < _iota((LANES, LANES), 1), 1.0, 0.0).astype(BF16)
        run = jnp.zeros_like(need)
        pieces = []
        for c in range(nchunk):
            e = eqf[:, c * LANES:(c + 1) * LANES]
            before = jnp.dot(e.astype(BF16), su, preferred_element_type=F32) + run
            pieces.append(jnp.where(before < need, e, 0.0))
            run = run + jnp.sum(e, axis=1, keepdims=True)
        return jnp.concatenate(pieces, axis=1)

    sel_eq = lax.cond(has_tie, tie_path, lambda _: eqf, 0)
    return jnp.where(few, validf, gtf + sel_eq)


def _dsa_prompt_kernel(qi_ref, smq_ref, sma_ref, qh_ref, kt_ref, v_ref, o_ref, *, n_sel):
    j = pl.program_id(1)
    S = kt_ref.shape[-1]
    qi = qi_ref[0]
    wi = smq_ref[0][:, _SM_IW:_SM_IW + IDX_HEADS] * (IDX_HEADS ** -0.5 * IDX_DIM ** -0.5)
    ki = sma_ref[0][:, _SM_IK:_SM_IK + IDX_DIM]
    score = jnp.zeros((Q_BLOCK, S), F32)
    for h in range(IDX_HEADS):
        s = _dot_nt3(qi[:, h * IDX_DIM:(h + 1) * IDX_DIM], ki)
        score = score + jnp.maximum(s, 0.0) * wi[:, h:h + 1]
    tpos = j * Q_BLOCK + _iota((Q_BLOCK, 1), 0)
    valid = _iota((1, S), 1) <= tpos
    mask = _topk_mask(score, valid, n_sel)
    bias = jnp.where(mask > 0.0, 0.0, -jnp.inf)
    bias2 = jnp.concatenate([bias, bias], axis=0)
    G = DSA_HEADS // DSA_KV_HEADS
    for kv in range(DSA_KV_HEADS):
        q2 = qh_ref[0, kv].reshape(G * Q_BLOCK, DSA_DH)
        s = _dot(q2, kt_ref[0, kv]) * (DSA_DH ** -0.5) + bias2
        m = jnp.max(s, axis=1, keepdims=True)
        p = jnp.exp(s - m)
        p = p / jnp.sum(p, axis=1, keepdims=True)
        o_ref[0, kv] = _dot(p, v_ref[0, kv]).reshape(G, Q_BLOCK, DSA_DH)


def _dsa_prompt(qi, small, qh, kt, vh, n_sel):
    B, T, _ = qi.shape
    G = DSA_HEADS // DSA_KV_HEADS
    return pl.pallas_call(
        functools.partial(_dsa_prompt_kernel, n_sel=n_sel),
        grid=(B, T // Q_BLOCK),
        in_specs=[pl.BlockSpec((1, Q_BLOCK, IDX_HEADS * IDX_DIM), lambda b, j: (b, j, 0)),
                  pl.BlockSpec((1, Q_BLOCK, LANES), lambda b, j: (b, j, 0)),
                  pl.BlockSpec((1, T, LANES), lambda b, j: (b, 0, 0)),
                  pl.BlockSpec((1, DSA_KV_HEADS, G, Q_BLOCK, DSA_DH), lambda b, j: (b, 0, 0, j, 0)),
                  pl.BlockSpec((1, DSA_KV_HEADS, DSA_DH, T), lambda b, j: (b, 0, 0, 0)),
                  pl.BlockSpec((1, DSA_KV_HEADS, T, DSA_DH), lambda b, j: (b, 0, 0, 0))],
        out_specs=pl.BlockSpec((1, DSA_KV_HEADS, G, Q_BLOCK, DSA_DH), lambda b, j: (b, 0, 0, j, 0)),
        out_shape=jax.ShapeDtypeStruct((B, DSA_KV_HEADS, G, T, DSA_DH), F32),
        compiler_params=_cparams("parallel", "arbitrary"),
        name="dsa_prompt",
    )(qi, small, small, qh, kt, vh)


def _dsa_sample_score_kernel(pt_ref, qi_ref, wi_ref, *refs, n_pg):
    del pt_ref
    o_ref = refs[n_pg]
    qi = qi_ref[0]
    wi = wi_ref[0] * (IDX_HEADS ** -0.5 * IDX_DIM ** -0.5)
    for p in range(n_pg):
        s = _dot_nt3(qi, refs[p][0])
        o_ref[0, :, p * PAGE_SIZE:(p + 1) * PAGE_SIZE] = jnp.sum(jnp.maximum(s, 0.0) * wi, axis=0, keepdims=True)


def _dsa_sample_scores(page_table, qi8, wi8, cache_ki, n_pg):
    Bd, n_pages = page_table.shape

    def page_spec(p):
        return pl.BlockSpec((1, PAGE_SIZE, IDX_DIM), lambda b, jj, pt: (pt[b, jj * n_pg + p], 0, 0))

    grid_spec = pltpu.PrefetchScalarGridSpec(
        num_scalar_prefetch=1,
        grid=(Bd, n_pages // n_pg),
        in_specs=[pl.BlockSpec((1, 8, IDX_DIM), lambda b, jj, pt: (b, 0, 0)),
                  pl.BlockSpec((1, 8, 1), lambda b, jj, pt: (b, 0, 0))] + [page_spec(p) for p in range(n_pg)],
        out_specs=pl.BlockSpec((1, 1, n_pg * PAGE_SIZE), lambda b, jj, pt: (b, 0, jj)),
    )
    return pl.pallas_call(
        functools.partial(_dsa_sample_score_kernel, n_pg=n_pg),
        grid_spec=grid_spec,
        out_shape=jax.ShapeDtypeStruct((Bd, 1, n_pages * PAGE_SIZE), F32),
        compiler_params=_cparams("parallel", "arbitrary"),
        name="dsa_sample_scores",
    )(page_table, qi8, wi8, *([cache_ki] * n_pg))


def _dsa_sample_select_kernel(sc_ref, qi_ref, sm_ref, idx_ref, pre_scr, *, n_sel):
    Bd, past = sc_ref.shape
    width = past + LANES
    sm = sm_ref[...]
    qi = qi_ref[...]
    ki = sm[:, _SM_IK:_SM_IK + IDX_DIM]
    wi = sm[:, _SM_IW:_SM_IW + IDX_HEADS] * (IDX_HEADS ** -0.5 * IDX_DIM ** -0.5)
    snew = jnp.zeros((Bd, 1), F32)
    for h in range(IDX_HEADS):
        s = jnp.sum(qi[:, h * IDX_DIM:(h + 1) * IDX_DIM] * ki, axis=1, keepdims=True)
        snew = snew + jnp.maximum(s, 0.0) * wi[:, h:h + 1]
    tail = jnp.where(_iota((Bd, LANES), 1) == 0, snew, -jnp.inf)
    x = jnp.concatenate([sc_ref[...], tail], axis=1)
    valid = jnp.broadcast_to(_iota((1, width), 1) <= past, (Bd, width))
    mask = _topk_mask(x, valid, n_sel)
    sui = jnp.where(_iota((LANES, LANES), 0) <= _iota((LANES, LANES), 1), 1.0, 0.0).astype(BF16)
    run = jnp.zeros((Bd, 1), F32)
    for c in range(width // LANES):
        e = mask[:, c * LANES:(c + 1) * LANES]
        pre_scr[:, c * LANES:(c + 1) * LANES] = jnp.dot(e.astype(BF16), sui, preferred_element_type=F32) + run
        run = run + jnp.sum(e, axis=1, keepdims=True)
    rank = _iota((n_sel, 1), 0).astype(F32)
    ones = jnp.ones((8, width), BF16)

    def body(b, carry):
        pre = pre_scr[pl.ds(b, 1), :]
        before = jnp.where(pre <= rank, 1.0, 0.0).astype(BF16)
        cnt = lax.dot_general(ones, before, _NT, preferred_element_type=F32)
        idx_ref[pl.ds(b, 1), :] = cnt[0:1, :].astype(I32)
        return carry

    lax.fori_loop(0, Bd, body, 0)


def _dsa_sample_select(scores, qi, small, n_sel):
    Bd, past = scores.shape
    return pl.pallas_call(
        functools.partial(_dsa_sample_select_kernel, n_sel=n_sel),
        out_shape=jax.ShapeDtypeStruct((Bd, n_sel), I32),
        scratch_shapes=[pltpu.VMEM((Bd, past + LANES), F32)],
        compiler_params=pltpu.CompilerParams(vmem_limit_bytes=VMEM_LIMIT_BYTES),
        name="dsa_sample_select",
    )(scores, qi, small)


def _attend_rows(q, kk, vv, scale):
    s = jnp.sum(kk * q[None], axis=-1, keepdims=True) * scale
    m = jnp.max(s, axis=0, keepdims=True)
    p = jnp.exp(s - m)
    l = jnp.sum(p, axis=0)
    return jnp.sum(p * vv, axis=0) / l


def _dsa_sample_attn_kernel(idx_ref, pt_ref, q_ref, kn_ref, vn_ref, ck_hbm, cv_hbm, o_ref, kbuf, vbuf, sem,
                            *, n_sel, past):
    def k_copy(phys, off, r):
        return pltpu.make_async_copy(ck_hbm.at[phys, off], kbuf.at[r], sem.at[0])

    def v_copy(phys, off, r):
        return pltpu.make_async_copy(cv_hbm.at[phys, off], vbuf.at[r], sem.at[1])

    def issue(r, carry):
        pidx = jnp.minimum(idx_ref[0, 0, r], past - 1)
        phys = pt_ref[0, 0, pidx // PAGE_SIZE]
        off = pidx % PAGE_SIZE
        k_copy(phys, off, r).start()
        v_copy(phys, off, r).start()
        return carry

    lax.fori_loop(0, n_sel, issue, 0)

    def wait(r, carry):
        k_copy(0, 0, r).wait()
        v_copy(0, 0, r).wait()
        return carry

    lax.fori_loop(0, n_sel, wait, 0)

    @pl.when(idx_ref[0, 0, n_sel - 1] >= past)
    def _():
        kbuf[n_sel - 1] = kn_ref[0]
        vbuf[n_sel - 1] = vn_ref[0]

    kk = kbuf[...]
    vv = vbuf[...]
    for g in range(DSA_HEADS // DSA_KV_HEADS):
        o_ref[0, g] = _attend_rows(q_ref[0, g], kk, vv, DSA_DH ** -0.5)


def _dsa_sample_attn(idx, page_table, q, k_new, v_new, cache_k, cache_v):
    Bd, n_sel = idx.shape
    n_pages = page_table.shape[1]
    G = DSA_HEADS // DSA_KV_HEADS
    smem = lambda n: pl.BlockSpec((1, 1, n), lambda b: (b, 0, 0), memory_space=pltpu.SMEM)
    row = pl.BlockSpec((1, DSA_KV_HEADS, DSA_DH), lambda b: (b, 0, 0))
    return pl.pallas_call(
        functools.partial(_dsa_sample_attn_kernel, n_sel=n_sel, past=n_pages * PAGE_SIZE),
        grid=(Bd,),
        in_specs=[smem(n_sel), smem(n_pages),
                  pl.BlockSpec((1, G, DSA_KV_HEADS, DSA_DH), lambda b: (b, 0, 0, 0)), row, row,
                  pl.BlockSpec(memory_space=pl.ANY), pl.BlockSpec(memory_space=pl.ANY)],
        out_specs=pl.BlockSpec((1, G, DSA_KV_HEADS, DSA_DH), lambda b: (b, 0, 0, 0)),
        out_shape=jax.ShapeDtypeStruct((Bd, G, DSA_KV_HEADS, DSA_DH), F32),
        scratch_shapes=[pltpu.VMEM((n_sel, DSA_KV_HEADS, DSA_DH), F32)] * 2 + [pltpu.SemaphoreType.DMA((2,))],
        compiler_params=_cparams("arbitrary"),
        name="dsa_sample_attn",
    )(idx.reshape(Bd, 1, n_sel), page_table.reshape(Bd, 1, n_pages), q, k_new, v_new, cache_k, cache_v)


def _mem_prompt_kernel(q_ref, mk_ref, mv_ref, o_ref):
    q = q_ref[0]
    mk = mk_ref[0]
    mv = mv_ref[0]
    outs = []
    for h in range(MEM_HEADS):
        sl = slice(h * MEM_DH, (h + 1) * MEM_DH)
        s = _dot_nt(q[:, sl], mk[:, sl]) * (MEM_DH ** -0.5)
        m = jnp.max(s, axis=1, keepdims=True)
        p = jnp.exp(s - m)
        p = p / jnp.sum(p, axis=1, keepdims=True)
        outs.append(_dot(p, mv[:, sl]))
    o_ref[0] = jnp.concatenate(outs, axis=1)


def _mem_prompt(q, mk, mv, tq):
    B, T, W = q.shape
    M = mk.shape[1]
    kv = pl.BlockSpec((1, M, W), lambda b, i: (b, 0, 0))
    return pl.pallas_call(
        _mem_prompt_kernel,
        grid=(B, T // tq),
        in_specs=[pl.BlockSpec((1, tq, W), lambda b, i: (b, i, 0)), kv, kv],
        out_specs=pl.BlockSpec((1, tq, W), lambda b, i: (b, i, 0)),
        out_shape=jax.ShapeDtypeStruct((B, T, W), F32),
        compiler_params=_cparams("parallel", "parallel"),
        name="mem_prompt",
    )(q, mk, mv)


def _mem_sample_kernel(q_ref, k_ref, v_ref, o_ref, *, ns):
    for i in range(ns):
        o_ref[i] = _attend_rows(q_ref[i], k_ref[i], v_ref[i], MEM_DH ** -0.5)


def _mem_sample(q, ck, cv, ns):
    Bd, M = ck.shape[:2]
    kv = pl.BlockSpec((ns, M, MEM_HEADS, MEM_DH), lambda b: (b, 0, 0, 0))
    qs = pl.BlockSpec((ns, MEM_HEADS, MEM_DH), lambda b: (b, 0, 0))
    return pl.pallas_call(
        functools.partial(_mem_sample_kernel, ns=ns),
        grid=(Bd // ns,),
        in_specs=[qs, kv, kv],
        out_specs=qs,
        out_shape=jax.ShapeDtypeStruct((Bd, MEM_HEADS, MEM_DH), F32),
        compiler_params=_cparams("parallel"),
        name="mem_sample",
    )(q, ck, cv)


def _merge_kernel(x_ref, og_ref, od_ref, om_ref, gn_ref, wg_ref, wb_ref, wo_ref, fn_ref, xo_ref, h2_ref):
    x = x_ref[...]
    d = x.shape[1]
    hb = _rms(x, gn_ref[...]).astype(BF16)
    acc = jnp.zeros_like(x)
    for n, o_ref in enumerate((og_ref, od_ref, om_ref)):
        gate = _sigmoid(jnp.dot(hb, wg_ref[:, n * d:(n + 1) * d], preferred_element_type=F32))
        acc = acc + gate * jnp.dot(_bf(o_ref[...]), wb_ref[n], preferred_element_type=F32)
    xo = x + jnp.dot(_bf(acc), wo_ref[...], preferred_element_type=F32)
    xo_ref[...] = xo
    h2_ref[...] = _rms(xo, fn_ref[...])


def _merge(x, o_g, o_d, o_m, gn, wg, wb, wo, fn, tm):
    n, d = x.shape
    tok = lambda w: pl.BlockSpec((tm, w), lambda i: (i, 0))
    full = lambda shape: pl.BlockSpec(shape, lambda i: (0,) * len(shape))
    return pl.pallas_call(
        _merge_kernel,
        grid=(n // tm,),
        in_specs=[tok(d), tok(BRANCH_W), tok(BRANCH_W), tok(BRANCH_W), full((1, d)), full((d, N_BRANCH * d)),
                  full((N_BRANCH, BRANCH_W, d)), full((d, d)), full((1, d))],
        out_specs=[tok(d), tok(d)],
        out_shape=[jax.ShapeDtypeStruct((n, d), F32)] * 2,
        compiler_params=_cparams("parallel"),
        name="merge",
    )(x, o_g, o_d, o_m, gn.reshape(1, d), wg, wb, wo, fn.reshape(1, d))


def _route_kernel(h_ref, wr_ref, br_ref, e_ref, p_ref, r_ref, cnt_ref, run_scr, *, tr):
    i = pl.program_id(0)

    @pl.when(i == 0)
    def _():
        run_scr[...] = jnp.zeros_like(run_scr)

    logits = _dot_nt3(wr_ref[...], h_ref[...]) + br_ref[...]
    eidx = _iota((N_EXPERTS, LANES), 0)
    su = jnp.where(_iota((LANES, LANES), 0) < _iota((LANES, LANES), 1), 1.0, 0.0).astype(BF16)
    run = run_scr[...]
    for c in range(tr // LANES):
        sl = slice(c * LANES, (c + 1) * LANES)
        l = logits[:, sl]
        vals, idxs = [], []
        for _ in range(TOP_K):
            m = jnp.max(l, axis=0, keepdims=True)
            idx = jnp.min(jnp.where(l == m, eidx, N_EXPERTS), axis=0, keepdims=True)
            vals.append(m)
            idxs.append(idx)
            l = jnp.where(eidx == idx, -jnp.inf, l)
        ex = [jnp.exp(v - vals[0]) for v in vals]
        den = ex[0] + ex[1] + ex[2] + ex[3]
        oh = jnp.zeros((N_EXPERTS, LANES), F32)
        for k in range(TOP_K):
            p_ref[k:k + 1, sl] = ex[k] / den
            e_ref[k:k + 1, sl] = idxs[k]
            oh = oh + jnp.where(eidx == idxs[k], 1.0, 0.0)
        before = jnp.dot(oh.astype(BF16), su, preferred_element_type=F32) + run
        for k in range(TOP_K):
            rk = jnp.sum(jnp.where(eidx == idxs[k], before, 0.0), axis=0, keepdims=True)
            r_ref[k:k + 1, sl] = rk.astype(I32)
        run = run + jnp.sum(oh, axis=1, keepdims=True)
    run_scr[...] = run
    cnt_ref[...] = jnp.broadcast_to(run, cnt_ref.shape)


def _route(h2, wr_t, br, tr):
    n, d = h2.shape
    tokrow = pl.BlockSpec((TOP_K, tr), lambda i: (0, i))
    return pl.pallas_call(
        functools.partial(_route_kernel, tr=tr),
        grid=(n // tr,),
        in_specs=[pl.BlockSpec((tr, d), lambda i: (i, 0)),
                  pl.BlockSpec((N_EXPERTS, d), lambda i: (0, 0)),
                  pl.BlockSpec((N_EXPERTS, 1), lambda i: (0, 0))],
        out_specs=[tokrow, tokrow, tokrow, pl.BlockSpec((N_EXPERTS, LANES), lambda i: (0, 0))],
        out_shape=[jax.ShapeDtypeStruct((TOP_K, n), I32), jax.ShapeDtypeStruct((TOP_K, n), F32),
                   jax.ShapeDtypeStruct((TOP_K, n), I32), jax.ShapeDtypeStruct((N_EXPERTS, LANES), F32)],
        scratch_shapes=[pltpu.VMEM((N_EXPERTS, 1), F32)],
        compiler_params=_cparams("arbitrary"),
        name="moe_route",
    )(h2, wr_t, br)


def _dispatch_kernel(dest_ref, h_hbm, z_hbm, xg_hbm, sem, *, td):
    del z_hbm
    i = pl.program_id(0)

    def row_copy(src, dst):
        return pltpu.make_async_copy(h_hbm.at[pl.ds(src, 1)], xg_hbm.at[pl.ds(dst, 1)], sem.at[0])

    def issue(t, carry):
        for k in range(TOP_K):
            row_copy(i * td + t, dest_ref[k, t]).start()
        return carry

    lax.fori_loop(0, td, issue, 0)

    def wait(t, carry):
        for k in range(TOP_K):
            row_copy(0, 0).wait()
        return carry

    lax.fori_loop(0, td, wait, 0)


def _dispatch(h2, dest, n_slots, td):
    n, d = h2.shape
    zeros = jnp.zeros((n_slots, d), F32)
    return pl.pallas_call(
        functools.partial(_dispatch_kernel, td=td),
        grid=(n // td,),
        in_specs=[pl.BlockSpec((TOP_K, td), lambda i: (0, i), memory_space=pltpu.SMEM),
                  pl.BlockSpec(memory_space=pl.ANY), pl.BlockSpec(memory_space=pl.ANY)],
        out_specs=pl.BlockSpec(memory_space=pl.ANY),
        out_shape=jax.ShapeDtypeStruct((n_slots, d), F32),
        scratch_shapes=[pltpu.SemaphoreType.DMA((1,))],
        input_output_aliases={2: 0},
        compiler_params=_cparams("arbitrary"),
        name="moe_dispatch",
    )(dest, h2, zeros)


def _expert_kernel(be_ref, nb_ref, x_ref, wgu_ref, bgu_ref, wdn_ref, bdn_ref, o_ref):
    del be_ref
    f = wdn_ref.shape[1]

    used = pl.program_id(0) < nb_ref[0]

    @pl.when(jnp.logical_not(used))
    def _():
        o_ref[...] = jnp.zeros_like(o_ref)

    @pl.when(used)
    def _():
        gu = jnp.dot(_bf(x_ref[...]), wgu_ref[0], preferred_element_type=F32) + bgu_ref[0]
        gate = jnp.minimum(gu[:, :f], SWIGLU_LIMIT)
        up = jnp.clip(gu[:, f:], -SWIGLU_LIMIT, SWIGLU_LIMIT)
        glu = gate * _sigmoid(SWIGLU_ALPHA * gate)
        o_ref[...] = jnp.dot(_bf((up + 1.0) * glu), wdn_ref[0], preferred_element_type=F32) + bdn_ref[0]


def _experts(xg, block_e, nb_used, wgu, bgu, wdn, bdn, blk):
    n_slots, d = xg.shape
    f = wdn.shape[1]
    blk_of = lambda i, nb: jnp.minimum(i, nb[0] - 1)
    tok = pl.BlockSpec((blk, d), lambda i, be, nb: (blk_of(i, nb), 0))
    per_e = lambda shape: pl.BlockSpec((1,) + shape, lambda i, be, nb: (be[blk_of(i, nb)], 0, 0))
    grid_spec = pltpu.PrefetchScalarGridSpec(
        num_scalar_prefetch=2,
        grid=(n_slots // blk,),
        in_specs=[tok, per_e((d, 2 * f)), per_e((1, 2 * f)), per_e((f, d)), per_e((1, d))],
        out_specs=pl.BlockSpec((blk, d), lambda i, be, nb: (i, 0)),
    )
    return pl.pallas_call(
        _expert_kernel,
        grid_spec=grid_spec,
        out_shape=jax.ShapeDtypeStruct((n_slots, d), F32),
        compiler_params=_cparams("arbitrary"),
        name="moe_experts",
    )(block_e, nb_used, xg, wgu, bgu, wdn, bdn)


def _combine_kernel(dest_ref, x_ref, p_ref, g_ref, yb_hbm, o_ref, buf, sem, *, tc):
    def row_copy(src, k, t):
        return pltpu.make_async_copy(yb_hbm.at[pl.ds(src, 1)], buf.at[k, pl.ds(t, 1)], sem.at[0])

    def issue(t, carry):
        for k in range(TOP_K):
            row_copy(dest_ref[k, t], k, t).start()
        return carry

    lax.fori_loop(0, tc, issue, 0)

    def wait(t, carry):
        for k in range(TOP_K):
            row_copy(0, k, t).wait()
        return carry

    lax.fori_loop(0, tc, wait, 0)
    p = p_ref[...]
    acc = p[:, 0:1] * buf[0]
    for k in range(1, TOP_K):
        acc = acc + p[:, k:k + 1] * buf[k]
    o_ref[...] = _rms(x_ref[...] + acc, g_ref[...])


def _combine(x, yb, dest, p_t, g, tc):
    n, d = x.shape
    return pl.pallas_call(
        functools.partial(_combine_kernel, tc=tc),
        grid=(n // tc,),
        in_specs=[pl.BlockSpec((TOP_K, tc), lambda i: (0, i), memory_space=pltpu.SMEM),
                  pl.BlockSpec((tc, d), lambda i: (i, 0)),
                  pl.BlockSpec((tc, TOP_K), lambda i: (i, 0)),
                  pl.BlockSpec((1, d), lambda i: (0, 0)),
                  pl.BlockSpec(memory_space=pl.ANY)],
        out_specs=pl.BlockSpec((tc, d), lambda i: (i, 0)),
        out_shape=jax.ShapeDtypeStruct((n, d), F32),
        scratch_shapes=[pltpu.VMEM((TOP_K, tc, d), F32), pltpu.SemaphoreType.DMA((1,))],
        compiler_params=_cparams("arbitrary"),
        name="moe_combine",
    )(dest, x, p_t, g.reshape(1, d), yb)


def _moe_final(x, h2, g_final, wr_t, br, wgu, bgu, wdn, bdn, blk, tile):
    n, d = x.shape
    e, p, rank, cnt = _route(h2, wr_t, br, tile)
    counts = cnt[:, 0].astype(I32)
    padded = (counts + blk - 1) // blk * blk
    pad_end = jnp.cumsum(padded)
    pad_start = pad_end - padded
    n_blocks = -(-n * TOP_K // blk) + N_EXPERTS
    dest = pad_start[e] + rank
    block_e = jnp.minimum(jnp.searchsorted(pad_end, jnp.arange(n_blocks, dtype=I32) * blk, side="right"),
                          N_EXPERTS - 1).astype(I32)
    nb_used = (pad_end[-1:] // blk).astype(I32)
    xg = _dispatch(h2, dest, n_blocks * blk, tile)
    yb = _experts(xg, block_e, nb_used, wgu, bgu, wdn, bdn, blk)
    return _combine(x, yb, dest, p.T, g_final, min(tile, 256))


def _prep_w_in(w):
    cuts = np.cumsum((0,) + _SPLITS)
    seg = lambda i: w[:, int(cuts[i]):int(cuts[i + 1])]
    small = jnp.concatenate([seg(8), seg(9), seg(2), seg(3), jnp.zeros((w.shape[0], LANES - _SM_END), w.dtype)], axis=1)
    w1 = jnp.concatenate([seg(0), seg(1), seg(4), seg(5), seg(6), seg(7), seg(10), small], axis=1)
    return w1.astype(BF16), w[:, int(cuts[-1]):].astype(BF16)


def kernel(x_prompt, x_sample, cache_k, cache_v, cache_k_idx, cache_mem_k, cache_mem_v, state_gdn, state_conv,
           page_table, mem_prompt, norm_attn, w_in, conv_w, gdn_a_log, gdn_dt_bias, gdn_norm, norm_mem, w_mem_kv,
           w_branch, w_out, norm_ffn, w_router, b_router, w_gate_up, b_gate_up, w_down, b_down, norm_final):
    B, T, D = x_prompt.shape
    Bd, Ts, _ = x_sample.shape
    assert Ts == 1 and w_in.shape[0] == 1, "one layer, one new token per sample"
    H = GDN_HEADS
    G = DSA_HEADS // DSA_KV_HEADS
    n_pages = page_table.shape[1]
    M = mem_prompt.shape[1]

    w1, wg = _prep_w_in(w_in[0])
    wb = w_branch[0].astype(BF16)
    wo = w_out[0].astype(BF16)
    wr_t = w_router[0].T
    br = b_router[0].reshape(N_EXPERTS, 1)
    wgu = w_gate_up[0].astype(BF16)
    wdn = w_down[0].astype(BF16)
    bgu = b_gate_up[0].reshape(N_EXPERTS, 1, -1)
    bdn = b_down[0].reshape(N_EXPERTS, 1, -1)
    alog = gdn_a_log[0]
    dtb = gdn_dt_bias[0]

    xp = x_prompt.reshape(B * T, D)
    g_qkv, g_z, d_q, d_k, d_v, i_q, m_q, small = _norm_matmul(xp, norm_attn[0], w1, _PROJ_SPLITS, min(512, B * T))

    qkv = g_qkv.reshape(B, T, 3, H, GDN_DK).transpose(2, 0, 3, 1, 4)
    z_h = g_z.reshape(B, T, H, GDN_DV).transpose(0, 2, 1, 3)
    chunked = lambda a: a.reshape(B, T // GDN_CHUNK, GDN_CHUNK, H).transpose(0, 1, 3, 2)
    cw_h = conv_w[0].reshape(GDN_CONV, 3, H, GDN_DK).transpose(1, 2, 0, 3)
    o_g, ssm_p = _gdn_prompt(qkv[0], qkv[1], qkv[2], z_h,
                             chunked(small[:, _SM_GA:_SM_GA + H]), chunked(small[:, _SM_GB:_SM_GB + H]),
                             cw_h, alog.reshape(H, 1), dtb.reshape(H, 1), gdn_norm[0].reshape(1, GDN_DV),
                             min(512, T))
    o_g = o_g.transpose(0, 2, 1, 3).reshape(B * T, H * GDN_DV)

    qh = d_q.reshape(B, T, DSA_KV_HEADS, G, DSA_DH).transpose(0, 2, 3, 1, 4)
    kt = d_k.reshape(B, T, DSA_KV_HEADS, DSA_DH).transpose(0, 2, 3, 1)
    vh = d_v.reshape(B, T, DSA_KV_HEADS, DSA_DH).transpose(0, 2, 1, 3)
    o_d = _dsa_prompt(i_q.reshape(B, T, -1), small.reshape(B, T, LANES), qh, kt, vh, min(TOPK_MAX, T // 4))
    o_d = o_d.transpose(0, 3, 1, 2, 4).reshape(B * T, DSA_HEADS * DSA_DH)

    mk, mv = _norm_matmul(mem_prompt.reshape(B * M, D), norm_mem[0], w_mem_kv[0].astype(BF16),
                          (MEM_HEADS * MEM_DH,) * 2, min(512, B * M))
    o_m = _mem_prompt(m_q.reshape(B, T, -1), mk.reshape(B, M, -1), mv.reshape(B, M, -1), min(512, T))
    o_m = o_m.reshape(B * T, MEM_HEADS * MEM_DH)

    xres, h2 = _merge(xp, o_g, o_d, o_m, norm_attn[0], wg, wb, wo, norm_ffn[0], min(256, B * T))
    y_prompt = _moe_final(xres, h2, norm_final, wr_t, br, wgu, bgu, wdn, bdn, 256, min(512, B * T))

    k_prompt = d_k.reshape(1, B, T, DSA_KV_HEADS, DSA_DH)
    v_prompt = d_v.reshape(1, B, T, DSA_KV_HEADS, DSA_DH)
    kidx_prompt = small[:, _SM_IK:_SM_IK + IDX_DIM].reshape(1, B, T, IDX_DIM)
    memk_prompt = mk.reshape(1, B, M, MEM_HEADS, MEM_DH)
    memv_prompt = mv.reshape(1, B, M, MEM_HEADS, MEM_DH)
    conv_prompt = g_qkv.reshape(B, T, GDN_QKV)[:, T - (GDN_CONV - 1):, :][None]

    xs = x_sample.reshape(Bd, D)
    s_qkv, s_z, sd_q, sd_k, sd_v, si_q, sm_q, ssmall = _norm_matmul(xs, norm_attn[0], w1, _PROJ_SPLITS, Bd)

    lanes_b = lambda a: jnp.broadcast_to(a[..., None], a.shape + (Bd,))
    og_t, s_t = _gdn_sample(
        s_qkv.T, state_conv[0].transpose(1, 2, 0), lanes_b(conv_w[0]), s_z.T,
        ssmall[:, _SM_GA:_SM_GA + H].T.reshape(H, 1, Bd), ssmall[:, _SM_GB:_SM_GB + H].T.reshape(H, 1, Bd),
        lanes_b(alog.reshape(H, 1)), lanes_b(dtb.reshape(H, 1)), lanes_b(gdn_norm[0]),
        state_gdn[0].transpose(1, 2, 3, 0))
    so_g = og_t.T
    ssm_sample = s_t.transpose(3, 0, 1, 2)[None]
    conv_sample = jnp.concatenate([state_conv[0][:, 1:], s_qkv[:, None, :]], axis=1)[None]

    qi8 = jnp.pad(si_q.reshape(Bd, IDX_HEADS, IDX_DIM), ((0, 0), (0, 8 - IDX_HEADS), (0, 0)))
    wi8 = jnp.pad(ssmall[:, _SM_IW:_SM_IW + IDX_HEADS], ((0, 0), (0, 8 - IDX_HEADS)))[..., None]
    n_pg = 16 if n_pages % 16 == 0 else n_pages
    scores = _dsa_sample_scores(page_table, qi8, wi8, cache_k_idx[0], n_pg).reshape(Bd, n_pages * PAGE_SIZE)
    n_sel = min(TOPK_MAX, (n_pages * PAGE_SIZE + 1) // 4)
    sel = _dsa_sample_select(scores, si_q, ssmall, n_sel)
    sq = sd_q.reshape(Bd, DSA_KV_HEADS, G, DSA_DH).transpose(0, 2, 1, 3)
    k_new = sd_k.reshape(Bd, DSA_KV_HEADS, DSA_DH)
    v_new = sd_v.reshape(Bd, DSA_KV_HEADS, DSA_DH)
    so_d = _dsa_sample_attn(sel, page_table, sq, k_new, v_new, cache_k[0], cache_v[0])
    so_d = so_d.transpose(0, 2, 1, 3).reshape(Bd, DSA_HEADS * DSA_DH)

    so_m = _mem_sample(sm_q.reshape(Bd, MEM_HEADS, MEM_DH), cache_mem_k[0], cache_mem_v[0], 4)
    so_m = so_m.reshape(Bd, MEM_HEADS * MEM_DH)

    sres, sh2 = _merge(xs, so_g, so_d, so_m, norm_attn[0], wg, wb, wo, norm_ffn[0], Bd)
    y_sample = _moe_final(sres, sh2, norm_final, wr_t, br, wgu, bgu, wdn, bdn, 256, Bd)

    return (y_prompt.reshape(B, T, D), y_sample.reshape(Bd, 1, D), k_prompt, v_prompt, kidx_prompt,
            memk_prompt, memv_prompt, ssm_p[None], conv_prompt,
            k_new.reshape(1, Bd, 1, DSA_KV_HEADS, DSA_DH), v_new.reshape(1, Bd, 1, DSA_KV_HEADS, DSA_DH),
            ssmall[:, _SM_IK:_SM_IK + IDX_DIM].reshape(1, Bd, 1, IDX_DIM), ssm_sample, conv_sample)
```

```python
import functools

import numpy as np
import jax
import jax.numpy as jnp
from jax import lax
from jax.experimental import pallas as pl
from jax.experimental.pallas import tpu as pltpu

F32 = jnp.float32
BF16 = jnp.bfloat16
I32 = jnp.int32

EPS = 1e-6
GDN_HEADS = 8
GDN_DK = 64
GDN_DV = 64
GDN_CONV = 4
GDN_CHUNK = 64
GDN_QKV = 2 * GDN_HEADS * GDN_DK + GDN_HEADS * GDN_DV
DSA_HEADS = 8
DSA_KV_HEADS = 4
DSA_DH = 64
IDX_HEADS = 4
IDX_DIM = 64
TOPK_MAX = 256
Q_BLOCK = 128
MEM_HEADS = 4
MEM_DH = 128
N_BRANCH = 3
BRANCH_W = 512
N_EXPERTS = 32
TOP_K = 4
SWIGLU_LIMIT = 7.0
SWIGLU_ALPHA = 1.702
PAGE_SIZE = 128
LANES = 128

_SPLITS = (GDN_QKV, GDN_HEADS * GDN_DV, GDN_HEADS, GDN_HEADS,
           DSA_HEADS * DSA_DH, DSA_KV_HEADS * DSA_DH, DSA_KV_HEADS * DSA_DH,
           IDX_HEADS * IDX_DIM, IDX_DIM, IDX_HEADS,
           MEM_HEADS * MEM_DH)
_PROJ_SPLITS = (GDN_QKV, 512, 512, 256, 256, 256, 512, LANES)
_SM_IK = 0
_SM_IW = IDX_DIM
_SM_GB = _SM_IW + IDX_HEADS
_SM_GA = _SM_GB + GDN_HEADS
_SM_END = _SM_GA + GDN_HEADS

VMEM_LIMIT_BYTES = 56 * 1024 * 1024


def _cparams(*sem):
    return pltpu.CompilerParams(dimension_semantics=sem, vmem_limit_bytes=VMEM_LIMIT_BYTES)


def _bf(x):
    return x.astype(BF16)


def _dot(a, b):
    return jnp.dot(_bf(a), _bf(b), preferred_element_type=F32)


_NT = (((1,), (1,)), ((), ()))


def _dot_nt(a, b):
    return lax.dot_general(_bf(a), _bf(b), _NT, preferred_element_type=F32)


def _split2(x):
    hi = x.astype(BF16)
    lo = (x - hi.astype(F32)).astype(BF16)
    return hi, lo


def _split3(x):
    hi = x.astype(BF16)
    r = x - hi.astype(F32)
    mid = r.astype(BF16)
    lo = (r - mid.astype(F32)).astype(BF16)
    return hi, mid, lo


def _dot_nt3(a, b):
    ah, al = _split2(a)
    bh, bl = _split2(b)
    d = lambda x, y: lax.dot_general(x, y, _NT, preferred_element_type=F32)
    return d(ah, bh) + (d(ah, bl) + d(al, bh))


def _dot3(a, b):
    ah, al = _split2(a)
    bh, bl = _split2(b)
    d = lambda x, y: jnp.dot(x, y, preferred_element_type=F32)
    return d(ah, bh) + (d(ah, bl) + d(al, bh))


def _dot_exact01(a, b01):
    hi, mid, lo = _split3(a)
    d = lambda x: jnp.dot(x, b01, preferred_element_type=F32)
    return d(hi) + (d(mid) + d(lo))


def _rms(x, g):
    return x * lax.rsqrt(jnp.mean(x * x, axis=-1, keepdims=True) + EPS) * g


def _sigmoid(x):
    return 1.0 / (1.0 + jnp.exp(-x))


def _silu(x):
    return x * _sigmoid(x)


def _softplus(x):
    return jnp.maximum(x, 0.0) + jnp.log(1.0 + jnp.exp(-jnp.abs(x)))


def _iota(shape, axis):
    return lax.broadcasted_iota(I32, shape, axis)


def _norm_matmul_kernel(x_ref, g_ref, w_ref, *o_refs, splits):
    hb = _rms(x_ref[...], g_ref[...]).astype(BF16)
    off = 0
    for o_ref, n in zip(o_refs, splits):
        o_ref[...] = jnp.dot(hb, w_ref[:, off:off + n], preferred_element_type=F32)
        off += n


def _norm_matmul(x, g, w, splits, tm):
    n, d = x.shape
    return pl.pallas_call(
        functools.partial(_norm_matmul_kernel, splits=splits),
        grid=(n // tm,),
        in_specs=[pl.BlockSpec((tm, d), lambda i: (i, 0)),
                  pl.BlockSpec((1, d), lambda i: (0, 0)),
                  pl.BlockSpec((d, sum(splits)), lambda i: (0, 0))],
        out_specs=[pl.BlockSpec((tm, s), lambda i: (i, 0)) for s in splits],
        out_shape=[jax.ShapeDtypeStruct((n, s), F32) for s in splits],
        compiler_params=_cparams("parallel"),
        name="norm_matmul",
    )(x, g.reshape(1, d), w)


QW = 4 * GDN_DK


def _bd_stack(x):
    lane_head = (_iota(x.shape, 1) >> 6) & 3
    return jnp.concatenate([jnp.where(lane_head == h, x, 0.0) for h in range(4)], axis=0)


def _bd_dot(a, x):
    return jnp.dot(_bf(a), _bf(_bd_stack(x)), preferred_element_type=F32)


def _bd_dot3(a, x):
    ah, al = _split2(a)
    xh = x.astype(BF16).astype(F32)
    bh = _bf(_bd_stack(xh))
    bl = _bf(_bd_stack(x - xh))
    d = lambda p, q: jnp.dot(p, q, preferred_element_type=F32)
    return d(ah, bh) + (d(ah, bl) + d(al, bh))


def _gdn_prompt_kernel(q_ref, k_ref, v_ref, qh_ref, kh_ref, vh_ref, z_ref, a_ref, b_ref,
                       cwq_ref, cwk_ref, cwv_ref, alog_ref, dtb_ref, nw_ref, o_ref, s_ref,
                       xq, xk, xv, s_scr, *, tt):
    ti = pl.program_id(1)
    C = GDN_CHUNK
    HALO = 8
    NG = GDN_HEADS // 4

    @pl.when(ti == 0)
    def _():
        s_scr[...] = jnp.zeros_like(s_scr)

    r = _iota((QW, QW), 0)
    c = _iota((QW, QW), 1)
    same = (r >> 6) == (c >> 6)
    ones_bd = jnp.where(same, 1.0, 0.0).astype(BF16)
    su_bd = jnp.where(same, jnp.where((r & 63) > (c & 63), 1.0, 0.0), 0.0).astype(BF16)
    rhs01 = jnp.concatenate([su_bd, ones_bd], axis=1)
    ident = jnp.where(r == c, 1.0, 0.0).astype(BF16)

    keep = (ti > 0).astype(F32)
    for idx, (src, halo, dst, cw) in enumerate(((q_ref, qh_ref, xq, cwq_ref), (k_ref, kh_ref, xk, cwk_ref),
                                                (v_ref, vh_ref, xv, cwv_ref))):
        dst[0:HALO, :] = halo[0] * keep
        dst[HALO:HALO + tt, :] = src[0]
        w = cw[...]
        y = w[0:1, :] * dst[pl.ds(HALO - 3, tt), :]
        for i in range(1, GDN_CONV):
            y = y + w[i:i + 1, :] * dst[pl.ds(HALO - 3 + i, tt), :]
        y = _silu(y)
        if idx < 2:
            ss = jnp.concatenate([_dot_exact01((y * y)[:, g * QW:(g + 1) * QW], ones_bd) for g in range(NG)], axis=1)
            y = y * lax.rsqrt(ss + EPS)
            if idx == 0:
                y = y * (GDN_DK ** -0.5)
        dst[HALO:HALO + tt, :] = y

    ri = _iota((C, QW), 0)
    li = _iota((C, QW), 1) & 63
    lane_head = _iota((C, QW), 1) >> 6
    incl = li <= ri
    strict = li < ri
    eye = li == ri
    nw = nw_ref[...]

    def chunk(ci, carry):
        r0 = pl.multiple_of(HALO + ci * C, 8)
        o0 = pl.multiple_of(ci * C, 8)
        groups = range(NG)
        cols = [slice(g * QW, (g + 1) * QW) for g in groups]
        qc = [xq[pl.ds(r0, C), cols[g]] for g in groups]
        kc = [xk[pl.ds(r0, C), cols[g]] for g in groups]
        vc = [xv[pl.ds(r0, C), cols[g]] for g in groups]
        g_row = [-jnp.exp(alog_ref[g]) * _softplus(a_ref[0, ci, g] + dtb_ref[g]) for g in groups]
        b_row = [_sigmoid(b_ref[0, ci, g]) for g in groups]
        res = [_dot_exact01(jnp.concatenate([jnp.where(incl, g_row[g], 0.0), jnp.where(eye, b_row[g], 0.0)], axis=0),
                            rhs01) for g in groups]
        gcol = [res[g][:C, QW:] for g in groups]
        bcol = [res[g][C:, QW:] for g in groups]
        gamma = [jnp.where(incl, jnp.exp(res[g][:C, :QW]), 0.0) for g in groups]
        eg = [jnp.exp(gcol[g]) for g in groups]
        glast = [gcol[g][C - 1:C, :] for g in groups]
        kb = [kc[g] * bcol[g] for g in groups]
        vb = [vc[g] * bcol[g] for g in groups]
        kt = [lax.dot_general(ident, jnp.concatenate([_bf(kc[g])] * 4, axis=0), _NT, preferred_element_type=F32)
              for g in groups]
        kt = [_bf(jnp.where(same, kt[g], 0.0)) for g in groups]
        kk = [jnp.dot(_bf(jnp.concatenate([kb[g], qc[g]], axis=0)), kt[g], preferred_element_type=F32)
              for g in groups]
        m = [jnp.where(strict, kk[g][:C] * gamma[g], 0.0) for g in groups]
        qk = [kk[g][C:] * gamma[g] for g in groups]
        y = [jnp.concatenate([vb[g], kb[g] * eg[g]], axis=1) for g in groups]
        y = [y[g] - _bd_dot3(m[g], y[g]) for g in groups]
        p = m
        for _ in range(5):
            p = [_bd_dot3(p[g], p[g]) for g in groups]
            y = [y[g] + _bd_dot3(p[g], y[g]) for g in groups]
        s = [s_scr[g] for g in groups]
        ws = [_bd_dot(jnp.concatenate([y[g][:, QW:], qc[g] * eg[g]], axis=0), s[g]) for g in groups]
        vn = [y[g][:, :QW] - ws[g][:C] for g in groups]
        o = [ws[g][C:] + _bd_dot(qk[g], vn[g]) for g in groups]
        kd = [kc[g] * jnp.exp(glast[g] - gcol[g]) for g in groups]
        kdt = [lax.dot_general(ident, _bf(kd[g]), _NT, preferred_element_type=F32) for g in groups]
        full = [jnp.dot(_bf(kdt[g]), _bf(vn[g]), preferred_element_type=F32) for g in groups]
        ms = [_dot_exact01(o[g] * o[g], ones_bd) * (1.0 / GDN_DV) for g in groups]
        for g in groups:
            sadd = jnp.where(lane_head == 0, full[g][0:C], 0.0)
            for h in range(1, 4):
                sadd = sadd + jnp.where(lane_head == h, full[g][h * C:(h + 1) * C], 0.0)
            s_scr[g] = s[g] * jnp.exp(glast[g]) + sadd
            zc = z_ref[0, pl.ds(o0, C), cols[g]]
            o_ref[0, pl.ds(o0, C), cols[g]] = o[g] * lax.rsqrt(ms[g] + EPS) * nw * _silu(zc)
        return carry

    lax.fori_loop(0, tt // C, chunk, 0)
    s_ref[0] = s_scr[...]


def _gdn_prompt(qkv, z, a, b, cw, alog, dtb, nw, tt):
    B, T, _ = qkv.shape
    nct = tt // GDN_CHUNK
    NG = GDN_HEADS // 4
    W = NG * QW
    col = lambda j: pl.BlockSpec((1, tt, W), lambda bi, ti: (bi, ti, j))
    halo = lambda j: pl.BlockSpec((1, 8, W), lambda bi, ti: (bi, jnp.maximum(ti * (tt // 8) - 1, 0), j))
    chunked = pl.BlockSpec((1, nct, NG, 1, QW), lambda bi, ti: (bi, ti, 0, 0, 0))
    cwcol = lambda j: pl.BlockSpec((GDN_CONV, W), lambda bi, ti: (0, j))
    per_group = pl.BlockSpec((NG, 1, QW), lambda bi, ti: (0, 0, 0))
    return pl.pallas_call(
        functools.partial(_gdn_prompt_kernel, tt=tt),
        grid=(B, T // tt),
        in_specs=[col(0), col(1), col(2), halo(0), halo(1), halo(2), col(0), chunked, chunked,
                  cwcol(0), cwcol(1), cwcol(2), per_group, per_group,
                  pl.BlockSpec((1, QW), lambda bi, ti: (0, 0))],
        out_specs=[col(0), pl.BlockSpec((1, NG, GDN_DK, QW), lambda bi, ti: (bi, 0, 0, 0))],
        out_shape=[jax.ShapeDtypeStruct((B, T, W), F32), jax.ShapeDtypeStruct((B, NG, GDN_DK, QW), F32)],
        scratch_shapes=[pltpu.VMEM((8 + tt, W), F32)] * 3 + [pltpu.VMEM((NG, GDN_DK, QW), F32)],
        compiler_params=_cparams("parallel", "arbitrary"),
        name="gdn_prompt",
    )(qkv, qkv, qkv, qkv, qkv, qkv, z, a, b, cw, cw, cw, alog, dtb, nw)


def _gdn_sample_kernel(q_ref, k_ref, v_ref, cq_ref, ck_ref, cv_ref, wq_ref, wk_ref, wv_ref,
                       z_ref, a_ref, b_ref, alog_ref, dtb_ref, nw_ref, s_ref, o_ref, so_ref):
    def conv(x_ref, c_ref, w_ref):
        y = w_ref[GDN_CONV - 1] * x_ref[...]
        for i in range(GDN_CONV - 1):
            y = y + w_ref[i] * c_ref[i]
        return _silu(y)

    q = conv(q_ref, cq_ref, wq_ref)
    q = q * lax.rsqrt(jnp.sum(q * q, axis=0, keepdims=True) + EPS) * (GDN_DK ** -0.5)
    k = conv(k_ref, ck_ref, wk_ref)
    k = k * lax.rsqrt(jnp.sum(k * k, axis=0, keepdims=True) + EPS)
    v = conv(v_ref, cv_ref, wv_ref)
    beta = _sigmoid(b_ref[0])
    g = -jnp.exp(alog_ref[0]) * _softplus(a_ref[0] + dtb_ref[0])
    dec = jnp.exp(g)
    ks = jnp.zeros_like(v)
    for i in range(GDN_DK):
        ks = ks + k[i:i + 1, :] * (s_ref[0, i] * dec)
    vn = beta * (v - ks)
    o = jnp.zeros_like(v)
    for i in range(GDN_DK):
        sn = s_ref[0, i] * dec + k[i:i + 1, :] * vn
        so_ref[0, i] = sn
        o = o + q[i:i + 1, :] * sn
    on = o * lax.rsqrt(jnp.mean(o * o, axis=0, keepdims=True) + EPS) * nw_ref[...]
    o_ref[...] = on * _silu(z_ref[...])


def _gdn_sample(qkv_t, conv_t, cw_t, z_t, a_t, b_t, alog_t, dtb_t, nw_t, s_t):
    nb = qkv_t.shape[1]
    H, DK, DV = GDN_HEADS, GDN_DK, GDN_DV
    row = lambda off: pl.BlockSpec((DK, nb), lambda h: (off + h, 0))
    crow = lambda off: pl.BlockSpec((GDN_CONV - 1, DK, nb), lambda h: (0, off + h, 0))
    wrow = lambda off: pl.BlockSpec((GDN_CONV, DK, nb), lambda h: (0, off + h, 0))
    per_head = pl.BlockSpec((1, 1, nb), lambda h: (h, 0, 0))
    return pl.pallas_call(
        _gdn_sample_kernel,
        grid=(H,),
        in_specs=[row(0), row(H), row(2 * H), crow(0), crow(H), crow(2 * H), wrow(0), wrow(H), wrow(2 * H),
                  row(0), per_head, per_head, per_head, per_head,
                  pl.BlockSpec((DV, nb), lambda h: (0, 0)),
                  pl.BlockSpec((1, DK, DV, nb), lambda h: (h, 0, 0, 0))],
        out_specs=[row(0), pl.BlockSpec((1, DK, DV, nb), lambda h: (h, 0, 0, 0))],
        out_shape=[jax.ShapeDtypeStruct((H * DV, nb), F32), jax.ShapeDtypeStruct((H, DK, DV, nb), F32)],
        compiler_params=_cparams("parallel"),
        name="gdn_sample",
    )(qkv_t, qkv_t, qkv_t, conv_t, conv_t, conv_t, cw_t, cw_t, cw_t, z_t, a_t, b_t, alog_t, dtb_t, nw_t, s_t)


def _count_ge(x, thr):
    return jnp.sum(jnp.where(x >= thr, 1.0, 0.0), axis=1, keepdims=True)


def _topk_mask(score, valid, k):
    rows, width = score.shape
    kf = float(k)
    x = jnp.where(valid, score, -jnp.inf)
    validf = jnp.where(valid, 1.0, 0.0)
    nvalid = jnp.sum(validf, axis=1, keepdims=True)
    few = nvalid <= kf
    mx = jnp.max(x, axis=1, keepdims=True)
    mn = jnp.min(jnp.where(valid, score, jnp.inf), axis=1, keepdims=True)
    lo0 = mn
    hi0 = mx + (jnp.abs(mx) * 1e-6 + 1e-30)
    chi0 = jnp.zeros_like(mx)

    def step(_, carry):
        lo, hi, chi = carry
        mid = 0.5 * lo + 0.5 * hi
        c = _count_ge(x, mid)
        ge = c >= kf
        return jnp.where(ge, mid, lo), jnp.where(ge, hi, mid), jnp.where(ge, chi, c)

    def finish(hi, chi):
        tau = jnp.max(jnp.where(x < hi, x, -jnp.inf), axis=1, keepdims=True)
        ceq = jnp.sum(jnp.where(x == tau, 1.0, 0.0), axis=1, keepdims=True)
        return tau, ceq

    def not_done(hi, chi):
        tau, ceq = finish(hi, chi)
        bad = jnp.where(few | (chi + ceq >= kf), 0.0, 1.0)
        return jnp.max(bad) > 0.0

    lo, hi, chi = lax.fori_loop(0, 24, step, (lo0, hi0, chi0))

    def w_cond(carry):
        it, _, hi, chi = carry
        return jnp.logical_and(it < 48, not_done(hi, chi))

    def w_body(carry):
        it, lo, hi, chi = carry
        lo, hi, chi = lax.fori_loop(0, 8, step, (lo, hi, chi))
        return it + 1, lo, hi, chi

    _, lo, hi, chi = lax.while_loop(w_cond, w_body, (jnp.int32(0), lo, hi, chi))
    tau, ceq = finish(hi, chi)
    need = kf - chi
    gtf = jnp.where(x > tau, 1.0, 0.0)
    eqf = jnp.where(x == tau, 1.0, 0.0)
    has_tie = jnp.max(jnp.where(few | (ceq <= need), 0.0, 1.0)) > 0.0

    def tie_path(_):
        nchunk = width // LANES
        su = jnp.where(_iota((LANES, LANES), 0) < _iota((LANES, LANES), 1), 1.0, 0.0).astype(BF16)
        run = jnp.zeros_like(need)
        pieces = []
        for c in range(nchunk):
            e = eqf[:, c * LANES:(c + 1) * LANES]
            before = jnp.dot(e.astype(BF16), su, preferred_element_type=F32) + run
            pieces.append(jnp.where(before < need, e, 0.0))
            run = run + jnp.sum(e, axis=1, keepdims=True)
        return jnp.concatenate(pieces, axis=1)

    sel_eq = lax.cond(has_tie, tie_path, lambda _: eqf, 0)
    return jnp.where(few, validf, gtf + sel_eq)


def _dsa_prompt_kernel(qi_ref, smq_ref, sma_ref, qh_ref, kt_ref, v_ref, o_ref, *, n_sel, j0):
    j = j0 + pl.program_id(1)
    S = kt_ref.shape[-1]
    qi = qi_ref[0]
    wi = smq_ref[0][:, _SM_IW:_SM_IW + IDX_HEADS] * (IDX_HEADS ** -0.5 * IDX_DIM ** -0.5)
    ki = sma_ref[0][:, _SM_IK:_SM_IK + IDX_DIM]
    score = jnp.zeros((Q_BLOCK, S), F32)
    for h in range(IDX_HEADS):
        s = _dot_nt3(qi[:, h * IDX_DIM:(h + 1) * IDX_DIM], ki)
        score = score + jnp.maximum(s, 0.0) * wi[:, h:h + 1]
    tpos = j * Q_BLOCK + _iota((Q_BLOCK, 1), 0)
    valid = _iota((1, S), 1) <= tpos
    mask = _topk_mask(score, valid, n_sel)
    bias = jnp.where(mask > 0.0, 0.0, -jnp.inf)
    bias2 = jnp.concatenate([bias, bias], axis=0)
    G = DSA_HEADS // DSA_KV_HEADS
    for kv in range(DSA_KV_HEADS):
        q2 = qh_ref[0, kv].reshape(G * Q_BLOCK, DSA_DH)
        s = _dot(q2, kt_ref[0, kv]) * (DSA_DH ** -0.5) + bias2
        m = jnp.max(s, axis=1, keepdims=True)
        p = jnp.exp(s - m)
        p = p / jnp.sum(p, axis=1, keepdims=True)
        o_ref[0, kv] = _dot(p, v_ref[0, kv]).reshape(G, Q_BLOCK, DSA_DH)


def _dsa_prompt(qi, small, qh, kt, vh, n_sel):
    B, T, _ = qi.shape
    G = DSA_HEADS // DSA_KV_HEADS
    nq = T // Q_BLOCK
    nseg = 4 if nq % 4 == 0 else 1
    qps = nq // nseg
    outs = []
    for seg in range(nseg):
        j0 = seg * qps
        S = (seg + 1) * qps * Q_BLOCK
        qblk = lambda b, j, j0=j0: (b, 0, 0, j0 + j, 0)
        outs.append(pl.pallas_call(
            functools.partial(_dsa_prompt_kernel, n_sel=n_sel, j0=j0),
            grid=(B, qps),
            in_specs=[pl.BlockSpec((1, Q_BLOCK, IDX_HEADS * IDX_DIM), lambda b, j, j0=j0: (b, j0 + j, 0)),
                      pl.BlockSpec((1, Q_BLOCK, LANES), lambda b, j, j0=j0: (b, j0 + j, 0)),
                      pl.BlockSpec((1, S, LANES), lambda b, j: (b, 0, 0)),
                      pl.BlockSpec((1, DSA_KV_HEADS, G, Q_BLOCK, DSA_DH), qblk),
                      pl.BlockSpec((1, DSA_KV_HEADS, DSA_DH, S), lambda b, j: (b, 0, 0, 0)),
                      pl.BlockSpec((1, DSA_KV_HEADS, S, DSA_DH), lambda b, j: (b, 0, 0, 0))],
            out_specs=pl.BlockSpec((1, DSA_KV_HEADS, G, Q_BLOCK, DSA_DH), lambda b, j: (b, 0, 0, j, 0)),
            out_shape=jax.ShapeDtypeStruct((B, DSA_KV_HEADS, G, qps * Q_BLOCK, DSA_DH), F32),
            compiler_params=_cparams("parallel", "arbitrary"),
            name="dsa_prompt",
        )(qi, small, small, qh, kt, vh))
    return outs[0] if nseg == 1 else jnp.concatenate(outs, axis=3)


def _dsa_sample_score_kernel(pt_ref, qi_ref, wi_ref, *refs, n_pg):
    del pt_ref
    o_ref = refs[n_pg]
    qi = qi_ref[0]
    wi = wi_ref[0] * (IDX_HEADS ** -0.5 * IDX_DIM ** -0.5)
    for p in range(n_pg):
        s = _dot3(qi, refs[p][0])
        o_ref[0, :, p * PAGE_SIZE:(p + 1) * PAGE_SIZE] = jnp.sum(jnp.maximum(s, 0.0) * wi, axis=0, keepdims=True)


def _dsa_sample_scores(page_table, qi8, wi8, cache_kit, n_pg):
    Bd, n_pages = page_table.shape

    def page_spec(p):
        return pl.BlockSpec((1, IDX_DIM, PAGE_SIZE), lambda b, jj, pt: (pt[b, jj * n_pg + p], 0, 0))

    grid_spec = pltpu.PrefetchScalarGridSpec(
        num_scalar_prefetch=1,
        grid=(Bd, n_pages // n_pg),
        in_specs=[pl.BlockSpec((1, 8, IDX_DIM), lambda b, jj, pt: (b, 0, 0)),
                  pl.BlockSpec((1, 8, 1), lambda b, jj, pt: (b, 0, 0))] + [page_spec(p) for p in range(n_pg)],
        out_specs=pl.BlockSpec((1, 1, n_pg * PAGE_SIZE), lambda b, jj, pt: (b, 0, jj)),
    )
    return pl.pallas_call(
        functools.partial(_dsa_sample_score_kernel, n_pg=n_pg),
        grid_spec=grid_spec,
        out_shape=jax.ShapeDtypeStruct((Bd, 1, n_pages * PAGE_SIZE), F32),
        compiler_params=_cparams("parallel", "arbitrary"),
        name="dsa_sample_scores",
    )(page_table, qi8, wi8, *([cache_kit] * n_pg))


def _dsa_sample_select_kernel(sc_ref, qi_ref, sm_ref, sel_ref, seln_ref, *, n_sel):
    Bd, past = sc_ref.shape
    width = past + LANES
    sm = sm_ref[...]
    qi = qi_ref[...]
    ki = sm[:, _SM_IK:_SM_IK + IDX_DIM]
    wi = sm[:, _SM_IW:_SM_IW + IDX_HEADS] * (IDX_HEADS ** -0.5 * IDX_DIM ** -0.5)
    snew = jnp.zeros((Bd, 1), F32)
    for h in range(IDX_HEADS):
        s = jnp.sum(qi[:, h * IDX_DIM:(h + 1) * IDX_DIM] * ki, axis=1, keepdims=True)
        snew = snew + jnp.maximum(s, 0.0) * wi[:, h:h + 1]
    tail = jnp.where(_iota((Bd, LANES), 1) == 0, snew, -jnp.inf)
    x = jnp.concatenate([sc_ref[...], tail], axis=1)
    valid = jnp.broadcast_to(_iota((1, width), 1) <= past, (Bd, width))
    mask = _topk_mask(x, valid, n_sel)
    sel_ref[...] = mask[:, :past]
    seln_ref[...] = mask[:, past:]


def _dsa_sample_select(scores, qi, small, n_sel):
    Bd, past = scores.shape
    return pl.pallas_call(
        functools.partial(_dsa_sample_select_kernel, n_sel=n_sel),
        out_shape=[jax.ShapeDtypeStruct((Bd, past), F32), jax.ShapeDtypeStruct((Bd, LANES), F32)],
        compiler_params=pltpu.CompilerParams(vmem_limit_bytes=VMEM_LIMIT_BYTES),
        name="dsa_sample_select",
    )(scores, qi, small)


_MASKED = -1e30


def _dsa_sample_attn_kernel(pt_ref, q_ref, sel_ref, seln_ref, kn_ref, vn_ref, *refs, n_pg):
    del pt_ref
    k_refs = refs[:n_pg]
    v_refs = refs[n_pg:2 * n_pg]
    o_ref, m_scr, l_scr, acc_scr = refs[2 * n_pg:]
    jj = pl.program_id(1)
    G = DSA_HEADS // DSA_KV_HEADS
    row_kv = _iota((DSA_HEADS, DSA_DH), 0) // G

    @pl.when(jj == 0)
    def _():
        m_scr[...] = jnp.full_like(m_scr, _MASKED)
        l_scr[...] = jnp.zeros_like(l_scr)
        acc_scr[...] = jnp.zeros_like(acc_scr)

    def block(kts, vts, msk):
        s = jnp.dot(_bf(q_ref[0, 0]), kts[0], preferred_element_type=F32)
        for kv in range(1, DSA_KV_HEADS):
            s = s + jnp.dot(_bf(q_ref[0, kv]), kts[kv], preferred_element_type=F32)
        s = s * (DSA_DH ** -0.5)
        on = msk > 0.0
        m_old = m_scr[...]
        m_new = jnp.maximum(m_old, jnp.max(jnp.where(on, s, _MASKED), axis=1, keepdims=True))
        alpha = jnp.exp(m_old - m_new)
        p = jnp.where(on, jnp.exp(s - m_new), 0.0)
        l_scr[...] = l_scr[...] * alpha + jnp.sum(p, axis=1, keepdims=True)
        pb = _bf(p)
        acc = acc_scr[...] * alpha
        for kv in range(DSA_KV_HEADS):
            o_kv = lax.dot_general(pb, vts[kv], _NT, preferred_element_type=F32)
            acc = acc + jnp.where(row_kv == kv, o_kv, 0.0)
        acc_scr[...] = acc
        m_scr[...] = m_new

    gather = lambda page_refs, kv: jnp.concatenate([_bf(r[0, kv]) for r in page_refs], axis=1)
    block([gather(k_refs, kv) for kv in range(DSA_KV_HEADS)], [gather(v_refs, kv) for kv in range(DSA_KV_HEADS)],
          sel_ref[0])

    @pl.when(jj == pl.num_programs(1) - 1)
    def _():
        block([_bf(kn_ref[0, kv]) for kv in range(DSA_KV_HEADS)], [_bf(vn_ref[0, kv]) for kv in range(DSA_KV_HEADS)],
              seln_ref[0])
        o_ref[0] = acc_scr[...] / l_scr[...]


def _dsa_sample_attn(page_table, q8, sel, sel_new, k_new_b, v_new_b, cache_kt, cache_vt, n_pg):
    Bd, n_pages = page_table.shape

    def page_spec(p):
        return pl.BlockSpec((1, DSA_KV_HEADS, DSA_DH, PAGE_SIZE), lambda b, jj, pt: (pt[b, jj * n_pg + p], 0, 0, 0))

    new_tok = pl.BlockSpec((1, DSA_KV_HEADS, DSA_DH, PAGE_SIZE), lambda b, jj, pt: (b, 0, 0, 0))
    grid_spec = pltpu.PrefetchScalarGridSpec(
        num_scalar_prefetch=1,
        grid=(Bd, n_pages // n_pg),
        in_specs=[pl.BlockSpec((1, DSA_KV_HEADS, DSA_HEADS, DSA_DH), lambda b, jj, pt: (b, 0, 0, 0)),
                  pl.BlockSpec((1, 1, n_pg * PAGE_SIZE), lambda b, jj, pt: (b, 0, jj)),
                  pl.BlockSpec((1, 1, PAGE_SIZE), lambda b, jj, pt: (b, 0, 0)), new_tok, new_tok]
                 + [page_spec(p) for p in range(n_pg)] * 2,
        out_specs=pl.BlockSpec((1, DSA_HEADS, DSA_DH), lambda b, jj, pt: (b, 0, 0)),
        scratch_shapes=[pltpu.VMEM((DSA_HEADS, 1), F32), pltpu.VMEM((DSA_HEADS, 1), F32),
                        pltpu.VMEM((DSA_HEADS, DSA_DH), F32)],
    )
    return pl.pallas_call(
        functools.partial(_dsa_sample_attn_kernel, n_pg=n_pg),
        grid_spec=grid_spec,
        out_shape=jax.ShapeDtypeStruct((Bd, DSA_HEADS, DSA_DH), F32),
        compiler_params=_cparams("parallel", "arbitrary"),
        name="dsa_sample_attn",
    )(page_table, q8, sel.reshape(Bd, 1, -1), sel_new.reshape(Bd, 1, PAGE_SIZE), k_new_b, v_new_b,
      *([cache_kt] * n_pg), *([cache_vt] * n_pg))


def _attend_rows(q, kk, vv, scale):
    s = jnp.sum(kk * q[None], axis=-1, keepdims=True) * scale
    m = jnp.max(s, axis=0, keepdims=True)
    p = jnp.exp(s - m)
    l = jnp.sum(p, axis=0)
    return jnp.sum(p * vv, axis=0) / l


def _mem_prompt_kernel(q_ref, mk_ref, mv_ref, o_ref):
    q = q_ref[0]
    mk = mk_ref[0]
    mv = mv_ref[0]
    outs = []
    for h in range(MEM_HEADS):
        sl = slice(h * MEM_DH, (h + 1) * MEM_DH)
        s = _dot_nt(q[:, sl], mk[:, sl]) * (MEM_DH ** -0.5)
        m = jnp.max(s, axis=1, keepdims=True)
        p = jnp.exp(s - m)
        p = p / jnp.sum(p, axis=1, keepdims=True)
        outs.append(_dot(p, mv[:, sl]))
    o_ref[0] = jnp.concatenate(outs, axis=1)


def _mem_prompt(q, mk, mv, tq):
    B, T, W = q.shape
    M = mk.shape[1]
    kv = pl.BlockSpec((1, M, W), lambda b, i: (b, 0, 0))
    return pl.pallas_call(
        _mem_prompt_kernel,
        grid=(B, T // tq),
        in_specs=[pl.BlockSpec((1, tq, W), lambda b, i: (b, i, 0)), kv, kv],
        out_specs=pl.BlockSpec((1, tq, W), lambda b, i: (b, i, 0)),
        out_shape=jax.ShapeDtypeStruct((B, T, W), F32),
        compiler_params=_cparams("parallel", "parallel"),
        name="mem_prompt",
    )(q, mk, mv)


def _mem_sample_kernel(q_ref, k_ref, v_ref, o_ref, *, ns):
    for i in range(ns):
        o_ref[i] = _attend_rows(q_ref[i], k_ref[i], v_ref[i], MEM_DH ** -0.5)


def _mem_sample(q, ck, cv, ns):
    Bd, M = ck.shape[:2]
    kv = pl.BlockSpec((ns, M, MEM_HEADS, MEM_DH), lambda b: (b, 0, 0, 0))
    qs = pl.BlockSpec((ns, MEM_HEADS, MEM_DH), lambda b: (b, 0, 0))
    return pl.pallas_call(
        functools.partial(_mem_sample_kernel, ns=ns),
        grid=(Bd // ns,),
        in_specs=[qs, kv, kv],
        out_specs=qs,
        out_shape=jax.ShapeDtypeStruct((Bd, MEM_HEADS, MEM_DH), F32),
        compiler_params=_cparams("parallel"),
        name="mem_sample",
    )(q, ck, cv)


def _merge_kernel(x_ref, og_ref, od_ref, om_ref, gn_ref, wg_ref, wb_ref, wo_ref, fn_ref, xo_ref, h2_ref):
    x = x_ref[...]
    d = x.shape[1]
    hb = _rms(x, gn_ref[...]).astype(BF16)
    acc = jnp.zeros_like(x)
    for n, o_ref in enumerate((og_ref, od_ref, om_ref)):
        gate = _sigmoid(jnp.dot(hb, wg_ref[:, n * d:(n + 1) * d], preferred_element_type=F32))
        acc = acc + gate * jnp.dot(_bf(o_ref[...]), wb_ref[n], preferred_element_type=F32)
    xo = x + jnp.dot(_bf(acc), wo_ref[...], preferred_element_type=F32)
    xo_ref[...] = xo
    h2_ref[...] = _rms(xo, fn_ref[...])


def _merge(x, o_g, o_d, o_m, gn, wg, wb, wo, fn, tm):
    n, d = x.shape
    tok = lambda w: pl.BlockSpec((tm, w), lambda i: (i, 0))
    full = lambda shape: pl.BlockSpec(shape, lambda i: (0,) * len(shape))
    return pl.pallas_call(
        _merge_kernel,
        grid=(n // tm,),
        in_specs=[tok(d), tok(BRANCH_W), tok(BRANCH_W), tok(BRANCH_W), full((1, d)), full((d, N_BRANCH * d)),
                  full((N_BRANCH, BRANCH_W, d)), full((d, d)), full((1, d))],
        out_specs=[tok(d), tok(d)],
        out_shape=[jax.ShapeDtypeStruct((n, d), F32)] * 2,
        compiler_params=_cparams("parallel"),
        name="merge",
    )(x, o_g, o_d, o_m, gn.reshape(1, d), wg, wb, wo, fn.reshape(1, d))


def _route_kernel(h_ref, wr_ref, br_ref, e_ref, p_ref, r_ref, cnt_ref, run_scr, *, tr):
    i = pl.program_id(0)

    @pl.when(i == 0)
    def _():
        run_scr[...] = jnp.zeros_like(run_scr)

    logits = _dot_nt3(wr_ref[...], h_ref[...]) + br_ref[...]
    eidx = _iota((N_EXPERTS, LANES), 0)
    su = jnp.where(_iota((LANES, LANES), 0) < _iota((LANES, LANES), 1), 1.0, 0.0).astype(BF16)
    run = run_scr[...]
    for c in range(tr // LANES):
        sl = slice(c * LANES, (c + 1) * LANES)
        l = logits[:, sl]
        vals, idxs = [], []
        for _ in range(TOP_K):
            m = jnp.max(l, axis=0, keepdims=True)
            idx = jnp.min(jnp.where(l == m, eidx, N_EXPERTS), axis=0, keepdims=True)
            vals.append(m)
            idxs.append(idx)
            l = jnp.where(eidx == idx, -jnp.inf, l)
        ex = [jnp.exp(v - vals[0]) for v in vals]
        den = ex[0] + ex[1] + ex[2] + ex[3]
        oh = jnp.zeros((N_EXPERTS, LANES), F32)
        for k in range(TOP_K):
            p_ref[k:k + 1, sl] = ex[k] / den
            e_ref[k:k + 1, sl] = idxs[k]
            oh = oh + jnp.where(eidx == idxs[k], 1.0, 0.0)
        before = jnp.dot(oh.astype(BF16), su, preferred_element_type=F32) + run
        for k in range(TOP_K):
            rk = jnp.sum(jnp.where(eidx == idxs[k], before, 0.0), axis=0, keepdims=True)
            r_ref[k:k + 1, sl] = rk.astype(I32)
        run = run + jnp.sum(oh, axis=1, keepdims=True)
    run_scr[...] = run
    cnt_ref[...] = jnp.broadcast_to(run, cnt_ref.shape)


def _route(h2, wr_t, br, tr):
    n, d = h2.shape
    tokrow = pl.BlockSpec((TOP_K, tr), lambda i: (0, i))
    return pl.pallas_call(
        functools.partial(_route_kernel, tr=tr),
        grid=(n // tr,),
        in_specs=[pl.BlockSpec((tr, d), lambda i: (i, 0)),
                  pl.BlockSpec((N_EXPERTS, d), lambda i: (0, 0)),
                  pl.BlockSpec((N_EXPERTS, 1), lambda i: (0, 0))],
        out_specs=[tokrow, tokrow, tokrow, pl.BlockSpec((N_EXPERTS, LANES), lambda i: (0, 0))],
        out_shape=[jax.ShapeDtypeStruct((TOP_K, n), I32), jax.ShapeDtypeStruct((TOP_K, n), F32),
                   jax.ShapeDtypeStruct((TOP_K, n), I32), jax.ShapeDtypeStruct((N_EXPERTS, LANES), F32)],
        scratch_shapes=[pltpu.VMEM((N_EXPERTS, 1), F32)],
        compiler_params=_cparams("arbitrary"),
        name="moe_route",
    )(h2, wr_t, br)


def _dispatch_kernel(plo_ref, pn_ref, nb_ref, dest_ref, h_ref, xg_hbm, zblk, sem, *, td):
    blk = zblk.shape[0]
    n_blocks = xg_hbm.shape[0] // blk

    def row_copy(t, dst):
        return pltpu.make_async_copy(h_ref.at[pl.ds(t, 1)], xg_hbm.at[pl.ds(dst, 1)], sem.at[0])

    def zero_copy(dst):
        return pltpu.make_async_copy(zblk.at[pl.ds(0, 1)], xg_hbm.at[pl.ds(dst, 1)], sem.at[1])

    def zero_block(i):
        return pltpu.make_async_copy(zblk, xg_hbm.at[pl.ds(pl.multiple_of(i * blk, blk), blk)], sem.at[2])

    def issue(t, carry):
        for k in range(TOP_K):
            row_copy(t, dest_ref[k, t]).start()
        return carry

    lax.fori_loop(0, td, issue, 0)

    @pl.when(pl.program_id(0) == 0)
    def _():
        zblk[...] = jnp.zeros_like(zblk)

        def tail_start(i, c):
            zero_block(i).start()
            return c

        def tail_wait(i, c):
            zero_block(0).wait()
            return c

        lax.fori_loop(nb_ref[0], n_blocks, tail_start, 0)
        lax.fori_loop(nb_ref[0], n_blocks, tail_wait, 0)

        def per_expert(e, carry):
            lo = plo_ref[e]

            def start(r, c):
                zero_copy(lo + r).start()
                return c

            def wait(r, c):
                zero_copy(0).wait()
                return c

            lax.fori_loop(0, pn_ref[e], start, 0)
            lax.fori_loop(0, pn_ref[e], wait, 0)
            return carry

        lax.fori_loop(0, N_EXPERTS, per_expert, 0)

    def wait(t, carry):
        for k in range(TOP_K):
            row_copy(0, 0).wait()
        return carry

    lax.fori_loop(0, td, wait, 0)


def _dispatch(h2, dest, pad_lo, pad_n, nb_used, n_slots, blk, td):
    n, d = h2.shape
    grid_spec = pltpu.PrefetchScalarGridSpec(
        num_scalar_prefetch=3,
        grid=(n // td,),
        in_specs=[pl.BlockSpec((TOP_K, td), lambda i, plo, pn, nb: (0, i), memory_space=pltpu.SMEM),
                  pl.BlockSpec((td, d), lambda i, plo, pn, nb: (i, 0))],
        out_specs=pl.BlockSpec(memory_space=pl.ANY),
        scratch_shapes=[pltpu.VMEM((blk, d), F32), pltpu.SemaphoreType.DMA((3,))],
    )
    return pl.pallas_call(
        functools.partial(_dispatch_kernel, td=td),
        grid_spec=grid_spec,
        out_shape=jax.ShapeDtypeStruct((n_slots, d), F32),
        compiler_params=_cparams("arbitrary"),
        name="moe_dispatch",
    )(pad_lo, pad_n, nb_used, dest, h2)


def _expert_kernel(be_ref, nb_ref, x_ref, wgu_ref, bgu_ref, wdn_ref, bdn_ref, o_ref):
    del be_ref
    f = wdn_ref.shape[1]

    used = pl.program_id(0) < nb_ref[0]

    @pl.when(jnp.logical_not(used))
    def _():
        o_ref[...] = jnp.zeros_like(o_ref)

    @pl.when(used)
    def _():
        gu = jnp.dot(_bf(x_ref[...]), wgu_ref[0], preferred_element_type=F32) + bgu_ref[0]
        gate = jnp.minimum(gu[:, :f], SWIGLU_LIMIT)
        up = jnp.clip(gu[:, f:], -SWIGLU_LIMIT, SWIGLU_LIMIT)
        glu = gate * _sigmoid(SWIGLU_ALPHA * gate)
        o_ref[...] = jnp.dot(_bf((up + 1.0) * glu), wdn_ref[0], preferred_element_type=F32) + bdn_ref[0]


def _experts(xg, block_e, nb_used, wgu, bgu, wdn, bdn, blk):
    n_slots, d = xg.shape
    f = wdn.shape[1]
    blk_of = lambda i, nb: jnp.minimum(i, nb[0] - 1)
    tok = pl.BlockSpec((blk, d), lambda i, be, nb: (blk_of(i, nb), 0))
    per_e = lambda shape: pl.BlockSpec((1,) + shape, lambda i, be, nb: (be[blk_of(i, nb)], 0, 0))
    grid_spec = pltpu.PrefetchScalarGridSpec(
        num_scalar_prefetch=2,
        grid=(n_slots // blk,),
        in_specs=[tok, per_e((d, 2 * f)), per_e((1, 2 * f)), per_e((f, d)), per_e((1, d))],
        out_specs=pl.BlockSpec((blk, d), lambda i, be, nb: (i, 0)),
    )
    return pl.pallas_call(
        _expert_kernel,
        grid_spec=grid_spec,
        out_shape=jax.ShapeDtypeStruct((n_slots, d), F32),
        compiler_params=_cparams("arbitrary"),
        name="moe_experts",
    )(block_e, nb_used, xg, wgu, bgu, wdn, bdn)


def _combine_kernel(dest_ref, x_ref, p_ref, g_ref, yb_hbm, o_ref, buf, sem, *, tc):
    def row_copy(src, k, t):
        return pltpu.make_async_copy(yb_hbm.at[pl.ds(src, 1)], buf.at[k, pl.ds(t, 1)], sem.at[0])

    def issue(t, carry):
        for k in range(TOP_K):
            row_copy(dest_ref[k, t], k, t).start()
        return carry

    lax.fori_loop(0, tc, issue, 0)

    def wait(t, carry):
        for k in range(TOP_K):
            row_copy(0, k, t).wait()
        return carry

    lax.fori_loop(0, tc, wait, 0)
    p = p_ref[...]
    acc = p[:, 0:1] * buf[0]
    for k in range(1, TOP_K):
        acc = acc + p[:, k:k + 1] * buf[k]
    o_ref[...] = _rms(x_ref[...] + acc, g_ref[...])


def _combine(x, yb, dest, p_t, g, tc):
    n, d = x.shape
    return pl.pallas_call(
        functools.partial(_combine_kernel, tc=tc),
        grid=(n // tc,),
        in_specs=[pl.BlockSpec((TOP_K, tc), lambda i: (0, i), memory_space=pltpu.SMEM),
                  pl.BlockSpec((tc, d), lambda i: (i, 0)),
                  pl.BlockSpec((tc, TOP_K), lambda i: (i, 0)),
                  pl.BlockSpec((1, d), lambda i: (0, 0)),
                  pl.BlockSpec(memory_space=pl.ANY)],
        out_specs=pl.BlockSpec((tc, d), lambda i: (i, 0)),
        out_shape=jax.ShapeDtypeStruct((n, d), F32),
        scratch_shapes=[pltpu.VMEM((TOP_K, tc, d), F32), pltpu.SemaphoreType.DMA((1,))],
        compiler_params=_cparams("arbitrary"),
        name="moe_combine",
    )(dest, x, p_t, g.reshape(1, d), yb)


def _moe_final(x, h2, g_final, wr_t, br, wgu, bgu, wdn, bdn, blk, tile):
    n, d = x.shape
    e, p, rank, cnt = _route(h2, wr_t, br, tile)
    counts = cnt[:, 0].astype(I32)
    padded = (counts + blk - 1) // blk * blk
    pad_end = jnp.cumsum(padded)
    pad_start = pad_end - padded
    n_blocks = -(-n * TOP_K // blk) + N_EXPERTS
    expert_ids = jnp.arange(N_EXPERTS, dtype=I32)[:, None, None]
    dest = rank + jnp.sum(jnp.where(e[None] == expert_ids, pad_start[:, None, None], 0), axis=0)
    block_lo = jnp.arange(n_blocks, dtype=I32)[:, None] * blk
    block_e = jnp.minimum(jnp.sum((pad_end[None, :] <= block_lo).astype(I32), axis=1), N_EXPERTS - 1)
    nb_used = (pad_end[-1:] // blk).astype(I32)
    xg = _dispatch(h2, dest, pad_start + counts, padded - counts, nb_used, n_blocks * blk, blk, tile)
    yb = _experts(xg, block_e, nb_used, wgu, bgu, wdn, bdn, blk)
    return _combine(x, yb, dest, p.T, g_final, min(tile, 256))


def _prep_w_in(w):
    cuts = np.cumsum((0,) + _SPLITS)
    seg = lambda i: w[:, int(cuts[i]):int(cuts[i + 1])]
    small = jnp.concatenate([seg(8), seg(9), seg(2), seg(3), jnp.zeros((w.shape[0], LANES - _SM_END), w.dtype)], axis=1)
    w1 = jnp.concatenate([seg(0), seg(1), seg(4), seg(5), seg(6), seg(7), seg(10), small], axis=1)
    return w1.astype(BF16), w[:, int(cuts[-1]):].astype(BF16)


def kernel(x_prompt, x_sample, cache_k, cache_v, cache_k_idx, cache_mem_k, cache_mem_v, state_gdn, state_conv,
           page_table, mem_prompt, norm_attn, w_in, conv_w, gdn_a_log, gdn_dt_bias, gdn_norm, norm_mem, w_mem_kv,
           w_branch, w_out, norm_ffn, w_router, b_router, w_gate_up, b_gate_up, w_down, b_down, norm_final):
    B, T, D = x_prompt.shape
    Bd, Ts, _ = x_sample.shape
    assert Ts == 1 and w_in.shape[0] == 1, "one layer, one new token per sample"
    H = GDN_HEADS
    G = DSA_HEADS // DSA_KV_HEADS
    n_pages = page_table.shape[1]
    M = mem_prompt.shape[1]

    w1, wg = _prep_w_in(w_in[0])
    wb = w_branch[0].astype(BF16)
    wo = w_out[0].astype(BF16)
    wr_t = w_router[0].T
    br = b_router[0].reshape(N_EXPERTS, 1)
    wgu = w_gate_up[0].astype(BF16)
    wdn = w_down[0].astype(BF16)
    bgu = b_gate_up[0].reshape(N_EXPERTS, 1, -1)
    bdn = b_down[0].reshape(N_EXPERTS, 1, -1)
    alog = gdn_a_log[0]
    dtb = gdn_dt_bias[0]

    xp = x_prompt.reshape(B * T, D)
    g_qkv, g_z, d_q, d_k, d_v, i_q, m_q, small = _norm_matmul(xp, norm_attn[0], w1, _PROJ_SPLITS, min(512, B * T))

    quad_rows = lambda a: (a.reshape(B, T // GDN_CHUNK, GDN_CHUNK, 2, 4).transpose(0, 1, 3, 4, 2)
                           .reshape(B, T // GDN_CHUNK, 2, 1, QW))
    per_group = lambda a: jnp.repeat(a, GDN_DK).reshape(2, 1, QW)
    o_g, ssm_q = _gdn_prompt(g_qkv.reshape(B, T, GDN_QKV), g_z.reshape(B, T, H * GDN_DV),
                             quad_rows(small[:, _SM_GA:_SM_GA + H]), quad_rows(small[:, _SM_GB:_SM_GB + H]),
                             conv_w[0], per_group(alog), per_group(dtb), jnp.tile(gdn_norm[0], 4).reshape(1, QW),
                             min(512, T))
    o_g = o_g.reshape(B * T, H * GDN_DV)
    ssm_p = ssm_q.reshape(B, 2, GDN_DK, 4, GDN_DV).transpose(0, 1, 3, 2, 4).reshape(B, H, GDN_DK, GDN_DV)

    qh = d_q.reshape(B, T, DSA_KV_HEADS, G, DSA_DH).transpose(0, 2, 3, 1, 4)
    kt = d_k.reshape(B, T, DSA_KV_HEADS, DSA_DH).transpose(0, 2, 3, 1)
    vh = d_v.reshape(B, T, DSA_KV_HEADS, DSA_DH).transpose(0, 2, 1, 3)
    o_d = _dsa_prompt(i_q.reshape(B, T, -1), small.reshape(B, T, LANES), qh, kt, vh, min(TOPK_MAX, T // 4))
    o_d = o_d.transpose(0, 3, 1, 2, 4).reshape(B * T, DSA_HEADS * DSA_DH)

    mk, mv = _norm_matmul(mem_prompt.reshape(B * M, D), norm_mem[0], w_mem_kv[0].astype(BF16),
                          (MEM_HEADS * MEM_DH,) * 2, min(512, B * M))
    o_m = _mem_prompt(m_q.reshape(B, T, -1), mk.reshape(B, M, -1), mv.reshape(B, M, -1), min(512, T))
    o_m = o_m.reshape(B * T, MEM_HEADS * MEM_DH)

    xres, h2 = _merge(xp, o_g, o_d, o_m, norm_attn[0], wg, wb, wo, norm_ffn[0], min(256, B * T))
    y_prompt = _moe_final(xres, h2, norm_final, wr_t, br, wgu, bgu, wdn, bdn, 256, min(512, B * T))

    k_prompt = d_k.reshape(1, B, T, DSA_KV_HEADS, DSA_DH)
    v_prompt = d_v.reshape(1, B, T, DSA_KV_HEADS, DSA_DH)
    kidx_prompt = small[:, _SM_IK:_SM_IK + IDX_DIM].reshape(1, B, T, IDX_DIM)
    memk_prompt = mk.reshape(1, B, M, MEM_HEADS, MEM_DH)
    memv_prompt = mv.reshape(1, B, M, MEM_HEADS, MEM_DH)
    conv_prompt = g_qkv.reshape(B, T, GDN_QKV)[:, T - (GDN_CONV - 1):, :][None]

    xs = x_sample.reshape(Bd, D)
    s_qkv, s_z, sd_q, sd_k, sd_v, si_q, sm_q, ssmall = _norm_matmul(xs, norm_attn[0], w1, _PROJ_SPLITS, Bd)

    lanes_b = lambda a: jnp.broadcast_to(a[..., None], a.shape + (Bd,))
    og_t, s_t = _gdn_sample(
        s_qkv.T, state_conv[0].transpose(1, 2, 0), lanes_b(conv_w[0]), s_z.T,
        ssmall[:, _SM_GA:_SM_GA + H].T.reshape(H, 1, Bd), ssmall[:, _SM_GB:_SM_GB + H].T.reshape(H, 1, Bd),
        lanes_b(alog.reshape(H, 1)), lanes_b(dtb.reshape(H, 1)), lanes_b(gdn_norm[0]),
        state_gdn[0].transpose(1, 2, 3, 0))
    so_g = og_t.T
    ssm_sample = s_t.transpose(3, 0, 1, 2)[None]
    conv_sample = jnp.concatenate([state_conv[0][:, 1:], s_qkv[:, None, :]], axis=1)[None]

    qi8 = jnp.pad(si_q.reshape(Bd, IDX_HEADS, IDX_DIM), ((0, 0), (0, 8 - IDX_HEADS), (0, 0)))
    wi8 = jnp.pad(ssmall[:, _SM_IW:_SM_IW + IDX_HEADS], ((0, 0), (0, 8 - IDX_HEADS)))[..., None]
    n_pg = 16 if n_pages % 16 == 0 else n_pages
    cache_kit = cache_k_idx[0].transpose(0, 2, 1)
    cache_kt = cache_k[0].transpose(0, 2, 3, 1)
    cache_vt = cache_v[0].transpose(0, 2, 3, 1)
    scores = _dsa_sample_scores(page_table, qi8, wi8, cache_kit, 32 if n_pages % 32 == 0 else n_pg)
    scores = scores.reshape(Bd, n_pages * PAGE_SIZE)
    n_sel = min(TOPK_MAX, (n_pages * PAGE_SIZE + 1) // 4)
    sel, sel_new = _dsa_sample_select(scores, si_q, ssmall, n_sel)
    over_lanes = lambda a: jnp.broadcast_to(a[..., None], a.shape + (PAGE_SIZE,))
    head_kv = jnp.arange(DSA_HEADS, dtype=I32) // G
    q8 = jnp.where((head_kv[None, :] == jnp.arange(DSA_KV_HEADS, dtype=I32)[:, None])[None, :, :, None],
                   sd_q.reshape(Bd, 1, DSA_HEADS, DSA_DH), 0.0)
    k_new = sd_k.reshape(Bd, DSA_KV_HEADS, DSA_DH)
    v_new = sd_v.reshape(Bd, DSA_KV_HEADS, DSA_DH)
    so_d = _dsa_sample_attn(page_table, q8, sel, sel_new, over_lanes(k_new), over_lanes(v_new),
                            cache_kt, cache_vt, n_pg).reshape(Bd, DSA_HEADS * DSA_DH)

    so_m = _mem_sample(sm_q.reshape(Bd, MEM_HEADS, MEM_DH), cache_mem_k[0], cache_mem_v[0], 4)
    so_m = so_m.reshape(Bd, MEM_HEADS * MEM_DH)

    sres, sh2 = _merge(xs, so_g, so_d, so_m, norm_attn[0], wg, wb, wo, norm_ffn[0], Bd)
    y_sample = _moe_final(sres, sh2, norm_final, wr_t, br, wgu, bgu, wdn, bdn, 256, Bd)

    return (y_prompt.reshape(B, T, D), y_sample.reshape(Bd, 1, D), k_prompt, v_prompt, kidx_prompt,
            memk_prompt, memv_prompt, ssm_p[None], conv_prompt,
            k_new.reshape(1, Bd, 1, DSA_KV_HEADS, DSA_DH), v_new.reshape(1, Bd, 1, DSA_KV_HEADS, DSA_DH),
            ssmall[:, _SM_IK:_SM_IK + IDX_DIM].reshape(1, Bd, 1, IDX_DIM), ssm_sample, conv_sample)
```

```python
import functools

import numpy as np
import jax
import jax.numpy as jnp
from jax import lax
from jax.experimental import pallas as pl
from jax.experimental.pallas import tpu as pltpu

F32 = jnp.float32
BF16 = jnp.bfloat16
I32 = jnp.int32

EPS = 1e-6
GDN_HEADS = 8
GDN_DK = 64
GDN_DV = 64
GDN_CONV = 4
GDN_CHUNK = 64
GDN_QKV = 2 * GDN_HEADS * GDN_DK + GDN_HEADS * GDN_DV
DSA_HEADS = 8
DSA_KV_HEADS = 4
DSA_DH = 64
IDX_HEADS = 4
IDX_DIM = 64
TOPK_MAX = 256
Q_BLOCK = 128
MEM_HEADS = 4
MEM_DH = 128
N_BRANCH = 3
BRANCH_W = 512
N_EXPERTS = 32
TOP_K = 4
SWIGLU_LIMIT = 7.0
SWIGLU_ALPHA = 1.702
PAGE_SIZE = 128
LANES = 128

_SPLITS = (GDN_QKV, GDN_HEADS * GDN_DV, GDN_HEADS, GDN_HEADS,
           DSA_HEADS * DSA_DH, DSA_KV_HEADS * DSA_DH, DSA_KV_HEADS * DSA_DH,
           IDX_HEADS * IDX_DIM, IDX_DIM, IDX_HEADS,
           MEM_HEADS * MEM_DH)
_PROJ_SPLITS = (GDN_QKV, 512, 512, 256, 256, 256, 512, LANES)
_SM_IK = 0
_SM_IW = IDX_DIM
_SM_GB = _SM_IW + IDX_HEADS
_SM_GA = _SM_GB + GDN_HEADS
_SM_END = _SM_GA + GDN_HEADS

VMEM_LIMIT_BYTES = 56 * 1024 * 1024


def _cparams(*sem):
    return pltpu.CompilerParams(dimension_semantics=sem, vmem_limit_bytes=VMEM_LIMIT_BYTES)


def _bf(x):
    return x.astype(BF16)


def _dot(a, b):
    return jnp.dot(_bf(a), _bf(b), preferred_element_type=F32)


_NT = (((1,), (1,)), ((), ()))


def _dot_nt(a, b):
    return lax.dot_general(_bf(a), _bf(b), _NT, preferred_element_type=F32)


def _split2(x):
    hi = x.astype(BF16)
    lo = (x - hi.astype(F32)).astype(BF16)
    return hi, lo


def _split3(x):
    hi = x.astype(BF16)
    r = x - hi.astype(F32)
    mid = r.astype(BF16)
    lo = (r - mid.astype(F32)).astype(BF16)
    return hi, mid, lo


def _dot_nt3(a, b):
    ah, al = _split2(a)
    bh, bl = _split2(b)
    d = lambda x, y: lax.dot_general(x, y, _NT, preferred_element_type=F32)
    return d(ah, bh) + (d(ah, bl) + d(al, bh))


def _dot3(a, b):
    ah, al = _split2(a)
    bh, bl = _split2(b)
    d = lambda x, y: jnp.dot(x, y, preferred_element_type=F32)
    return d(ah, bh) + (d(ah, bl) + d(al, bh))


def _dot_exact01(a, b01):
    hi, mid, lo = _split3(a)
    d = lambda x: jnp.dot(x, b01, preferred_element_type=F32)
    return d(hi) + (d(mid) + d(lo))


def _rms(x, g):
    return x * lax.rsqrt(jnp.mean(x * x, axis=-1, keepdims=True) + EPS) * g


def _sigmoid(x):
    return 1.0 / (1.0 + jnp.exp(-x))


def _silu(x):
    return x * _sigmoid(x)


def _softplus(x):
    return jnp.maximum(x, 0.0) + jnp.log(1.0 + jnp.exp(-jnp.abs(x)))


def _iota(shape, axis):
    return lax.broadcasted_iota(I32, shape, axis)


def _norm_matmul_kernel(x_ref, g_ref, w_ref, *o_refs, splits):
    hb = _rms(x_ref[...], g_ref[...]).astype(BF16)
    off = 0
    for o_ref, n in zip(o_refs, splits):
        o_ref[...] = jnp.dot(hb, w_ref[:, off:off + n], preferred_element_type=F32)
        off += n


def _norm_matmul(x, g, w, splits, tm):
    n, d = x.shape
    return pl.pallas_call(
        functools.partial(_norm_matmul_kernel, splits=splits),
        grid=(n // tm,),
        in_specs=[pl.BlockSpec((tm, d), lambda i: (i, 0)),
                  pl.BlockSpec((1, d), lambda i: (0, 0)),
                  pl.BlockSpec((d, sum(splits)), lambda i: (0, 0))],
        out_specs=[pl.BlockSpec((tm, s), lambda i: (i, 0)) for s in splits],
        out_shape=[jax.ShapeDtypeStruct((n, s), F32) for s in splits],
        compiler_params=_cparams("parallel"),
        name="norm_matmul",
    )(x, g.reshape(1, d), w)


def _proj_prompt_kernel(x_ref, g_ref, w_ref, qkv_ref, z_ref, dk_ref, dv_ref, iq_ref, mq_ref, sm_ref,
                        qh_ref, kt_ref, vh_ref):
    hb = _rms(x_ref[...], g_ref[...]).astype(BF16)
    offs = np.cumsum((0,) + _PROJ_SPLITS)
    part = lambda i: jnp.dot(hb, w_ref[:, int(offs[i]):int(offs[i + 1])], preferred_element_type=F32)
    qkv_ref[...] = part(0)
    z_ref[...] = part(1)
    dq = part(2)
    for h in range(DSA_HEADS):
        qh_ref[0, h] = dq[:, h * DSA_DH:(h + 1) * DSA_DH]
    dk = part(3)
    dk_ref[...] = dk
    kt_ref[0] = dk.T.reshape(DSA_KV_HEADS, DSA_DH, dk.shape[0])
    dv = part(4)
    dv_ref[...] = dv
    for kv in range(DSA_KV_HEADS):
        vh_ref[0, kv] = dv[:, kv * DSA_DH:(kv + 1) * DSA_DH]
    iq_ref[...] = part(5)
    mq_ref[...] = part(6)
    sm_ref[...] = part(7)


def _proj_prompt(x, g, w, B, T, tm):
    n, d = x.shape
    tpb = T // tm
    tok = lambda width: pl.BlockSpec((tm, width), lambda b, t: (b * tpb + t, 0))
    flat = lambda width: jax.ShapeDtypeStruct((n, width), F32)
    sp = _PROJ_SPLITS
    return pl.pallas_call(
        _proj_prompt_kernel,
        grid=(B, tpb),
        in_specs=[tok(d), pl.BlockSpec((1, d), lambda b, t: (0, 0)), pl.BlockSpec((d, sum(sp)), lambda b, t: (0, 0))],
        out_specs=[tok(sp[0]), tok(sp[1]), tok(sp[3]), tok(sp[4]), tok(sp[5]), tok(sp[6]), tok(sp[7]),
                   pl.BlockSpec((1, DSA_HEADS, tm, DSA_DH), lambda b, t: (b, 0, t, 0)),
                   pl.BlockSpec((1, DSA_KV_HEADS, DSA_DH, tm), lambda b, t: (b, 0, 0, t)),
                   pl.BlockSpec((1, DSA_KV_HEADS, tm, DSA_DH), lambda b, t: (b, 0, t, 0))],
        out_shape=[flat(sp[0]), flat(sp[1]), flat(sp[3]), flat(sp[4]), flat(sp[5]), flat(sp[6]), flat(sp[7]),
                   jax.ShapeDtypeStruct((B, DSA_HEADS, T, DSA_DH), F32),
                   jax.ShapeDtypeStruct((B, DSA_KV_HEADS, DSA_DH, T), F32),
                   jax.ShapeDtypeStruct((B, DSA_KV_HEADS, T, DSA_DH), F32)],
        compiler_params=_cparams("parallel", "parallel"),
        name="proj_prompt",
    )(x, g.reshape(1, d), w)


QW = 4 * GDN_DK


def _bd_stack(x):
    lane_head = (_iota(x.shape, 1) >> 6) & 3
    return jnp.concatenate([jnp.where(lane_head == h, x, 0.0) for h in range(4)], axis=0)


def _bd_dot(a, x):
    return jnp.dot(_bf(a), _bf(_bd_stack(x)), preferred_element_type=F32)


def _bd_dot3(a, x):
    ah, al = _split2(a)
    xh = x.astype(BF16).astype(F32)
    bh = _bf(_bd_stack(xh))
    bl = _bf(_bd_stack(x - xh))
    d = lambda p, q: jnp.dot(p, q, preferred_element_type=F32)
    return d(ah, bh) + (d(ah, bl) + d(al, bh))


def _gdn_prompt_kernel(q_ref, k_ref, v_ref, qh_ref, kh_ref, vh_ref, z_ref, a_ref, b_ref,
                       cwq_ref, cwk_ref, cwv_ref, alog_ref, dtb_ref, nw_ref, o_ref, s_ref,
                       xq, xk, xv, s_scr, *, tt):
    ti = pl.program_id(1)
    C = GDN_CHUNK
    HALO = 8
    NG = GDN_HEADS // 4

    @pl.when(ti == 0)
    def _():
        s_scr[...] = jnp.zeros_like(s_scr)

    r = _iota((QW, QW), 0)
    c = _iota((QW, QW), 1)
    same = (r >> 6) == (c >> 6)
    ones_bd = jnp.where(same, 1.0, 0.0).astype(BF16)
    su_bd = jnp.where(same, jnp.where((r & 63) > (c & 63), 1.0, 0.0), 0.0).astype(BF16)
    rhs01 = jnp.concatenate([su_bd, ones_bd], axis=1)
    ident = jnp.where(r == c, 1.0, 0.0).astype(BF16)

    keep = (ti > 0).astype(F32)
    for idx, (src, halo, dst, cw) in enumerate(((q_ref, qh_ref, xq, cwq_ref), (k_ref, kh_ref, xk, cwk_ref),
                                                (v_ref, vh_ref, xv, cwv_ref))):
        dst[0:HALO, :] = halo[0] * keep
        dst[HALO:HALO + tt, :] = src[0]
        w = cw[...]
        y = w[0:1, :] * dst[pl.ds(HALO - 3, tt), :]
        for i in range(1, GDN_CONV):
            y = y + w[i:i + 1, :] * dst[pl.ds(HALO - 3 + i, tt), :]
        y = _silu(y)
        if idx < 2:
            ss = jnp.concatenate([_dot_exact01((y * y)[:, g * QW:(g + 1) * QW], ones_bd) for g in range(NG)], axis=1)
            y = y * lax.rsqrt(ss + EPS)
            if idx == 0:
                y = y * (GDN_DK ** -0.5)
        dst[HALO:HALO + tt, :] = y

    ri = _iota((C, QW), 0)
    li = _iota((C, QW), 1) & 63
    lane_head = _iota((C, QW), 1) >> 6
    incl = li <= ri
    strict = li < ri
    eye = li == ri
    nw = nw_ref[...]

    def solve_stage(cis):
        streams = [(ci, g) for ci in cis for g in range(NG)]
        n = range(len(streams))
        rows = [pl.multiple_of(HALO + ci * C, 8) for ci, _ in streams]
        cols = [slice(g * QW, (g + 1) * QW) for _, g in streams]
        qc = [xq[pl.ds(rows[i], C), cols[i]] for i in n]
        kc = [xk[pl.ds(rows[i], C), cols[i]] for i in n]
        vc = [xv[pl.ds(rows[i], C), cols[i]] for i in n]
        g_row = [-jnp.exp(alog_ref[g]) * _softplus(a_ref[0, ci, g] + dtb_ref[g]) for ci, g in streams]
        b_row = [_sigmoid(b_ref[0, ci, g]) for ci, g in streams]
        res = [_dot_exact01(jnp.concatenate([jnp.where(incl, g_row[i], 0.0), jnp.where(eye, b_row[i], 0.0)], axis=0),
                            rhs01) for i in n]
        gcol = [res[i][:C, QW:] for i in n]
        bcol = [res[i][C:, QW:] for i in n]
        gamma = [jnp.where(incl, jnp.exp(res[i][:C, :QW]), 0.0) for i in n]
        eg = [jnp.exp(gcol[i]) for i in n]
        glast = [gcol[i][C - 1:C, :] for i in n]
        kb = [kc[i] * bcol[i] for i in n]
        kt = [lax.dot_general(ident, jnp.concatenate([_bf(kc[i])] * 4, axis=0), _NT, preferred_element_type=F32)
              for i in n]
        kt = [_bf(jnp.where(same, kt[i], 0.0)) for i in n]
        kk = [jnp.dot(_bf(jnp.concatenate([kb[i], qc[i]], axis=0)), kt[i], preferred_element_type=F32)
              for i in n]
        m = [jnp.where(strict, kk[i][:C] * gamma[i], 0.0) for i in n]
        y = [jnp.concatenate([vc[i] * bcol[i], kb[i] * eg[i]], axis=1) for i in n]
        y = [y[i] - _bd_dot3(m[i], y[i]) for i in n]
        p = m
        for _ in range(5):
            p = [_bd_dot3(p[i], p[i]) for i in n]
            y = [y[i] + _bd_dot3(p[i], y[i]) for i in n]
        return [dict(ci=streams[i][0], g=streams[i][1], y=y[i], qg=qc[i] * eg[i], qk=kk[i][C:] * gamma[i],
                     kd=kc[i] * jnp.exp(glast[i] - gcol[i]), dec=jnp.exp(glast[i])) for i in n]

    def scan_stage(st):
        n = range(len(st))
        s = [s_scr[t["g"]] for t in st]
        ws = [_bd_dot(jnp.concatenate([st[i]["y"][:, QW:], st[i]["qg"]], axis=0), s[i]) for i in n]
        vn = [st[i]["y"][:, :QW] - ws[i][:C] for i in n]
        o = [ws[i][C:] + _bd_dot(st[i]["qk"], vn[i]) for i in n]
        kdt = [lax.dot_general(ident, _bf(st[i]["kd"]), _NT, preferred_element_type=F32) for i in n]
        full = [jnp.dot(_bf(kdt[i]), _bf(vn[i]), preferred_element_type=F32) for i in n]
        ms = [_dot_exact01(o[i] * o[i], ones_bd) * (1.0 / GDN_DV) for i in n]
        for i in n:
            g = st[i]["g"]
            sadd = jnp.where(lane_head == 0, full[i][0:C], 0.0)
            for h in range(1, 4):
                sadd = sadd + jnp.where(lane_head == h, full[i][h * C:(h + 1) * C], 0.0)
            s_scr[g] = s[i] * st[i]["dec"] + sadd
            o0 = pl.multiple_of(st[i]["ci"] * C, 8)
            cols = slice(g * QW, (g + 1) * QW)
            zc = z_ref[0, pl.ds(o0, C), cols]
            o_ref[0, pl.ds(o0, C), cols] = o[i] * lax.rsqrt(ms[i] + EPS) * nw * _silu(zc)

    def chunk_pair(pi, carry):
        st = solve_stage([2 * pi, 2 * pi + 1])
        scan_stage(st[:NG])
        scan_stage(st[NG:])
        return carry

    lax.fori_loop(0, tt // (2 * C), chunk_pair, 0)
    s_ref[0] = s_scr[...]


def _gdn_prompt(qkv, z, a, b, cw, alog, dtb, nw, tt):
    B, T, _ = qkv.shape
    nct = tt // GDN_CHUNK
    NG = GDN_HEADS // 4
    W = NG * QW
    col = lambda j: pl.BlockSpec((1, tt, W), lambda bi, ti: (bi, ti, j))
    halo = lambda j: pl.BlockSpec((1, 8, W), lambda bi, ti: (bi, jnp.maximum(ti * (tt // 8) - 1, 0), j))
    chunked = pl.BlockSpec((1, nct, NG, 1, QW), lambda bi, ti: (bi, ti, 0, 0, 0))
    cwcol = lambda j: pl.BlockSpec((GDN_CONV, W), lambda bi, ti: (0, j))
    per_group = pl.BlockSpec((NG, 1, QW), lambda bi, ti: (0, 0, 0))
    return pl.pallas_call(
        functools.partial(_gdn_prompt_kernel, tt=tt),
        grid=(B, T // tt),
        in_specs=[col(0), col(1), col(2), halo(0), halo(1), halo(2), col(0), chunked, chunked,
                  cwcol(0), cwcol(1), cwcol(2), per_group, per_group,
                  pl.BlockSpec((1, QW), lambda bi, ti: (0, 0))],
        out_specs=[col(0), pl.BlockSpec((1, NG, GDN_DK, QW), lambda bi, ti: (bi, 0, 0, 0))],
        out_shape=[jax.ShapeDtypeStruct((B, T, W), F32), jax.ShapeDtypeStruct((B, NG, GDN_DK, QW), F32)],
        scratch_shapes=[pltpu.VMEM((8 + tt, W), F32)] * 3 + [pltpu.VMEM((NG, GDN_DK, QW), F32)],
        compiler_params=_cparams("parallel", "arbitrary"),
        name="gdn_prompt",
    )(qkv, qkv, qkv, qkv, qkv, qkv, z, a, b, cw, cw, cw, alog, dtb, nw)


def _gdn_sample_kernel(q_ref, k_ref, v_ref, cq_ref, ck_ref, cv_ref, wq_ref, wk_ref, wv_ref,
                       z_ref, a_ref, b_ref, alog_ref, dtb_ref, nw_ref, s_ref, o_ref, so_ref):
    def conv(x_ref, c_ref, w_ref):
        y = w_ref[GDN_CONV - 1] * x_ref[...]
        for i in range(GDN_CONV - 1):
            y = y + w_ref[i] * c_ref[i]
        return _silu(y)

    q = conv(q_ref, cq_ref, wq_ref)
    q = q * lax.rsqrt(jnp.sum(q * q, axis=0, keepdims=True) + EPS) * (GDN_DK ** -0.5)
    k = conv(k_ref, ck_ref, wk_ref)
    k = k * lax.rsqrt(jnp.sum(k * k, axis=0, keepdims=True) + EPS)
    v = conv(v_ref, cv_ref, wv_ref)
    beta = _sigmoid(b_ref[0])
    g = -jnp.exp(alog_ref[0]) * _softplus(a_ref[0] + dtb_ref[0])
    dec = jnp.exp(g)
    ks = jnp.zeros_like(v)
    for i in range(GDN_DK):
        ks = ks + k[i:i + 1, :] * (s_ref[0, i] * dec)
    vn = beta * (v - ks)
    o = jnp.zeros_like(v)
    for i in range(GDN_DK):
        sn = s_ref[0, i] * dec + k[i:i + 1, :] * vn
        so_ref[0, i] = sn
        o = o + q[i:i + 1, :] * sn
    on = o * lax.rsqrt(jnp.mean(o * o, axis=0, keepdims=True) + EPS) * nw_ref[...]
    o_ref[...] = on * _silu(z_ref[...])


def _gdn_sample(qkv_t, conv_t, cw_t, z_t, a_t, b_t, alog_t, dtb_t, nw_t, s_t):
    nb = qkv_t.shape[1]
    H, DK, DV = GDN_HEADS, GDN_DK, GDN_DV
    row = lambda off: pl.BlockSpec((DK, nb), lambda h: (off + h, 0))
    crow = lambda off: pl.BlockSpec((GDN_CONV - 1, DK, nb), lambda h: (0, off + h, 0))
    wrow = lambda off: pl.BlockSpec((GDN_CONV, DK, nb), lambda h: (0, off + h, 0))
    per_head = pl.BlockSpec((1, 1, nb), lambda h: (h, 0, 0))
    return pl.pallas_call(
        _gdn_sample_kernel,
        grid=(H,),
        in_specs=[row(0), row(H), row(2 * H), crow(0), crow(H), crow(2 * H), wrow(0), wrow(H), wrow(2 * H),
                  row(0), per_head, per_head, per_head, per_head,
                  pl.BlockSpec((DV, nb), lambda h: (0, 0)),
                  pl.BlockSpec((1, DK, DV, nb), lambda h: (h, 0, 0, 0))],
        out_specs=[row(0), pl.BlockSpec((1, DK, DV, nb), lambda h: (h, 0, 0, 0))],
        out_shape=[jax.ShapeDtypeStruct((H * DV, nb), F32), jax.ShapeDtypeStruct((H, DK, DV, nb), F32)],
        compiler_params=_cparams("parallel"),
        name="gdn_sample",
    )(qkv_t, qkv_t, qkv_t, conv_t, conv_t, conv_t, cw_t, cw_t, cw_t, z_t, a_t, b_t, alog_t, dtb_t, nw_t, s_t)


def _count_ge(x, thr):
    return jnp.sum(jnp.where(x >= thr, 1.0, 0.0), axis=1, keepdims=True)


def _topk_mask(score, valid, k):
    rows, width = score.shape
    kf = float(k)
    x = jnp.where(valid, score, -jnp.inf)
    validf = jnp.where(valid, 1.0, 0.0)
    nvalid = jnp.sum(validf, axis=1, keepdims=True)
    few = nvalid <= kf
    mx = jnp.max(x, axis=1, keepdims=True)
    mn = jnp.min(jnp.where(valid, score, jnp.inf), axis=1, keepdims=True)
    lo0 = mn
    hi0 = mx + (jnp.abs(mx) * 1e-6 + 1e-30)
    chi0 = jnp.zeros_like(mx)

    def step(_, carry):
        lo, hi, chi = carry
        mid = 0.5 * lo + 0.5 * hi
        c = _count_ge(x, mid)
        ge = c >= kf
        return jnp.where(ge, mid, lo), jnp.where(ge, hi, mid), jnp.where(ge, chi, c)

    def finish(hi, chi):
        tau = jnp.max(jnp.where(x < hi, x, -jnp.inf), axis=1, keepdims=True)
        ceq = jnp.sum(jnp.where(x == tau, 1.0, 0.0), axis=1, keepdims=True)
        return tau, ceq

    def not_done(hi, chi):
        tau, ceq = finish(hi, chi)
        bad = jnp.where(few | (chi + ceq >= kf), 0.0, 1.0)
        return jnp.max(bad) > 0.0

    lo, hi, chi = lax.fori_loop(0, 24, step, (lo0, hi0, chi0))

    def w_cond(carry):
        it, _, hi, chi = carry
        return jnp.logical_and(it < 48, not_done(hi, chi))

    def w_body(carry):
        it, lo, hi, chi = carry
        lo, hi, chi = lax.fori_loop(0, 8, step, (lo, hi, chi))
        return it + 1, lo, hi, chi

    _, lo, hi, chi = lax.while_loop(w_cond, w_body, (jnp.int32(0), lo, hi, chi))
    tau, ceq = finish(hi, chi)
    need = kf - chi
    gtf = jnp.where(x > tau, 1.0, 0.0)
    eqf = jnp.where(x == tau, 1.0, 0.0)
    has_tie = jnp.max(jnp.where(few | (ceq <= need), 0.0, 1.0)) > 0.0

    def tie_path(_):
        nchunk = width // LANES
        su = jnp.where(_iota((LANES, LANES), 0) < _iota((LANES, LANES), 1), 1.0, 0.0).astype(BF16)
        run = jnp.zeros_like(need)
        pieces = []
        for c in range(nchunk):
            e = eqf[:, c * LANES:(c + 1) * LANES]
            before = jnp.dot(e.astype(BF16), su, preferred_element_type=F32) + run
            pieces.append(jnp.where(before < need, e, 0.0))
            run = run + jnp.sum(e, axis=1, keepdims=True)
        return jnp.concatenate(pieces, axis=1)

    sel_eq = lax.cond(has_tie, tie_path, lambda _: eqf, 0)
    return jnp.where(few, validf, gtf + sel_eq)


def _dsa_prompt_kernel(qi_ref, smq_ref, sma_ref, qh_ref, kt_ref, v_ref, o_ref, *, n_sel, j0, qb):
    j = j0 + pl.program_id(1)
    S = kt_ref.shape[-1]
    qi = qi_ref[0]
    wi = smq_ref[0][:, _SM_IW:_SM_IW + IDX_HEADS] * (IDX_HEADS ** -0.5 * IDX_DIM ** -0.5)
    ki = sma_ref[0][:, _SM_IK:_SM_IK + IDX_DIM]
    score = jnp.zeros((qb, S), F32)
    for h in range(IDX_HEADS):
        s = _dot_nt(qi[:, h * IDX_DIM:(h + 1) * IDX_DIM], ki)
        score = score + jnp.maximum(s, 0.0) * wi[:, h:h + 1]
    tpos = j * qb + _iota((qb, 1), 0)
    valid = _iota((1, S), 1) <= tpos
    mask = _topk_mask(score, valid, n_sel)
    bias = jnp.where(mask > 0.0, 0.0, -jnp.inf)
    bias2 = jnp.concatenate([bias, bias], axis=0)
    G = DSA_HEADS // DSA_KV_HEADS
    outs = []
    for kv in range(DSA_KV_HEADS):
        q2 = qh_ref[0, G * kv:G * (kv + 1)].reshape(G * qb, DSA_DH) * (DSA_DH ** -0.5)
        s = _dot(q2, kt_ref[0, kv]) + bias2
        p = jnp.exp(s - jnp.max(s, axis=1, keepdims=True))
        o = _dot(p, v_ref[0, kv]) / jnp.sum(p, axis=1, keepdims=True)
        outs += [o[g * qb:(g + 1) * qb] for g in range(G)]
    o_ref[0] = jnp.concatenate(outs, axis=1)


def _dsa_prompt(qi, small, qh, kt, vh, n_sel, qb):
    B, T, _ = qi.shape
    nq = T // qb
    nseg = 4 if nq % 4 == 0 else 1
    qps = nq // nseg
    outs = []
    for seg in range(nseg):
        j0 = seg * qps
        S = (seg + 1) * qps * qb
        outs.append(pl.pallas_call(
            functools.partial(_dsa_prompt_kernel, n_sel=n_sel, j0=j0, qb=qb),
            grid=(B, qps),
            in_specs=[pl.BlockSpec((1, qb, IDX_HEADS * IDX_DIM), lambda b, j, j0=j0: (b, j0 + j, 0)),
                      pl.BlockSpec((1, qb, LANES), lambda b, j, j0=j0: (b, j0 + j, 0)),
                      pl.BlockSpec((1, S, LANES), lambda b, j: (b, 0, 0)),
                      pl.BlockSpec((1, DSA_HEADS, qb, DSA_DH), lambda b, j, j0=j0: (b, 0, j0 + j, 0)),
                      pl.BlockSpec((1, DSA_KV_HEADS, DSA_DH, S), lambda b, j: (b, 0, 0, 0)),
                      pl.BlockSpec((1, DSA_KV_HEADS, S, DSA_DH), lambda b, j: (b, 0, 0, 0))],
            out_specs=pl.BlockSpec((1, qb, DSA_HEADS * DSA_DH), lambda b, j: (b, j, 0)),
            out_shape=jax.ShapeDtypeStruct((B, qps * qb, DSA_HEADS * DSA_DH), F32),
            compiler_params=_cparams("parallel", "arbitrary"),
            name="dsa_prompt",
        )(qi, small, small, qh, kt, vh))
    return outs[0] if nseg == 1 else jnp.concatenate(outs, axis=1)


def _dsa_sample_score_kernel(pt_ref, qi_ref, wi_ref, *refs, n_pg):
    del pt_ref
    o_ref = refs[n_pg]
    qi = qi_ref[0]
    wi = wi_ref[0] * (IDX_HEADS ** -0.5 * IDX_DIM ** -0.5)
    for p in range(n_pg):
        s = _dot(qi, refs[p][0])
        o_ref[0, :, p * PAGE_SIZE:(p + 1) * PAGE_SIZE] = jnp.sum(jnp.maximum(s, 0.0) * wi, axis=0, keepdims=True)


def _dsa_sample_scores(page_table, qi8, wi8, cache_kit, n_pg):
    Bd, n_pages = page_table.shape

    def page_spec(p):
        return pl.BlockSpec((1, IDX_DIM, PAGE_SIZE), lambda b, jj, pt: (pt[b, jj * n_pg + p], 0, 0))

    grid_spec = pltpu.PrefetchScalarGridSpec(
        num_scalar_prefetch=1,
        grid=(Bd, n_pages // n_pg),
        in_specs=[pl.BlockSpec((1, 8, IDX_DIM), lambda b, jj, pt: (b, 0, 0)),
                  pl.BlockSpec((1, 8, 1), lambda b, jj, pt: (b, 0, 0))] + [page_spec(p) for p in range(n_pg)],
        out_specs=pl.BlockSpec((1, 1, n_pg * PAGE_SIZE), lambda b, jj, pt: (b, 0, jj)),
    )
    return pl.pallas_call(
        functools.partial(_dsa_sample_score_kernel, n_pg=n_pg),
        grid_spec=grid_spec,
        out_shape=jax.ShapeDtypeStruct((Bd, 1, n_pages * PAGE_SIZE), F32),
        compiler_params=_cparams("parallel", "arbitrary"),
        name="dsa_sample_scores",
    )(page_table, qi8, wi8, *([cache_kit] * n_pg))


def _dsa_sample_select_kernel(sc_ref, qi_ref, sm_ref, sel_ref, seln_ref, *, n_sel):
    Bd, past = sc_ref.shape
    width = past + LANES
    sm = sm_ref[...]
    qi = _bf(qi_ref[...]).astype(F32)
    ki = _bf(sm[:, _SM_IK:_SM_IK + IDX_DIM]).astype(F32)
    wi = sm[:, _SM_IW:_SM_IW + IDX_HEADS] * (IDX_HEADS ** -0.5 * IDX_DIM ** -0.5)
    snew = jnp.zeros((Bd, 1), F32)
    for h in range(IDX_HEADS):
        s = jnp.sum(qi[:, h * IDX_DIM:(h + 1) * IDX_DIM] * ki, axis=1, keepdims=True)
        snew = snew + jnp.maximum(s, 0.0) * wi[:, h:h + 1]
    tail = jnp.where(_iota((Bd, LANES), 1) == 0, snew, -jnp.inf)
    x = jnp.concatenate([sc_ref[...], tail], axis=1)
    valid = jnp.broadcast_to(_iota((1, width), 1) <= past, (Bd, width))
    mask = _topk_mask(x, valid, n_sel)
    sel_ref[...] = mask[:, :past]
    seln_ref[...] = mask[:, past:]


def _dsa_sample_select(scores, qi, small, n_sel):
    Bd, past = scores.shape
    return pl.pallas_call(
        functools.partial(_dsa_sample_select_kernel, n_sel=n_sel),
        out_shape=[jax.ShapeDtypeStruct((Bd, past), F32), jax.ShapeDtypeStruct((Bd, LANES), F32)],
        compiler_params=pltpu.CompilerParams(vmem_limit_bytes=VMEM_LIMIT_BYTES),
        name="dsa_sample_select",
    )(scores, qi, small)


_MASKED = -1e30


def _dsa_sample_attn_kernel(pt_ref, q_ref, sel_ref, seln_ref, kn_ref, vn_ref, *refs, n_pg):
    del pt_ref
    k_refs = refs[:n_pg]
    v_refs = refs[n_pg:2 * n_pg]
    o_ref, m_scr, l_scr, acc_scr = refs[2 * n_pg:]
    jj = pl.program_id(1)
    G = DSA_HEADS // DSA_KV_HEADS
    row_kv = _iota((DSA_HEADS, DSA_DH), 0) // G

    @pl.when(jj == 0)
    def _():
        m_scr[...] = jnp.full_like(m_scr, _MASKED)
        l_scr[...] = jnp.zeros_like(l_scr)
        acc_scr[...] = jnp.zeros_like(acc_scr)

    def block(kts, vts, msk):
        s = jnp.dot(_bf(q_ref[0, 0]), kts[0], preferred_element_type=F32)
        for kv in range(1, DSA_KV_HEADS):
            s = s + jnp.dot(_bf(q_ref[0, kv]), kts[kv], preferred_element_type=F32)
        s = s * (DSA_DH ** -0.5)
        on = msk > 0.0
        m_old = m_scr[...]
        m_new = jnp.maximum(m_old, jnp.max(jnp.where(on, s, _MASKED), axis=1, keepdims=True))
        alpha = jnp.exp(m_old - m_new)
        p = jnp.where(on, jnp.exp(s - m_new), 0.0)
        l_scr[...] = l_scr[...] * alpha + jnp.sum(p, axis=1, keepdims=True)
        pb = _bf(p)
        acc = acc_scr[...] * alpha
        for kv in range(DSA_KV_HEADS):
            o_kv = lax.dot_general(pb, vts[kv], _NT, preferred_element_type=F32)
            acc = acc + jnp.where(row_kv == kv, o_kv, 0.0)
        acc_scr[...] = acc
        m_scr[...] = m_new

    gather = lambda page_refs, kv: jnp.concatenate([_bf(r[0, kv]) for r in page_refs], axis=1)
    block([gather(k_refs, kv) for kv in range(DSA_KV_HEADS)], [gather(v_refs, kv) for kv in range(DSA_KV_HEADS)],
          sel_ref[0])

    @pl.when(jj == pl.num_programs(1) - 1)
    def _():
        block([_bf(kn_ref[0, kv]) for kv in range(DSA_KV_HEADS)], [_bf(vn_ref[0, kv]) for kv in range(DSA_KV_HEADS)],
              seln_ref[0])
        o_ref[0] = acc_scr[...] / l_scr[...]


def _dsa_sample_attn(page_table, q8, sel, sel_new, k_new_b, v_new_b, cache_kt, cache_vt, n_pg):
    Bd, n_pages = page_table.shape

    def page_spec(p):
        return pl.BlockSpec((1, DSA_KV_HEADS, DSA_DH, PAGE_SIZE), lambda b, jj, pt: (pt[b, jj * n_pg + p], 0, 0, 0))

    new_tok = pl.BlockSpec((1, DSA_KV_HEADS, DSA_DH, PAGE_SIZE), lambda b, jj, pt: (b, 0, 0, 0))
    grid_spec = pltpu.PrefetchScalarGridSpec(
        num_scalar_prefetch=1,
        grid=(Bd, n_pages // n_pg),
        in_specs=[pl.BlockSpec((1, DSA_KV_HEADS, DSA_HEADS, DSA_DH), lambda b, jj, pt: (b, 0, 0, 0)),
                  pl.BlockSpec((1, 1, n_pg * PAGE_SIZE), lambda b, jj, pt: (b, 0, jj)),
                  pl.BlockSpec((1, 1, PAGE_SIZE), lambda b, jj, pt: (b, 0, 0)), new_tok, new_tok]
                 + [page_spec(p) for p in range(n_pg)] * 2,
        out_specs=pl.BlockSpec((1, DSA_HEADS, DSA_DH), lambda b, jj, pt: (b, 0, 0)),
        scratch_shapes=[pltpu.VMEM((DSA_HEADS, 1), F32), pltpu.VMEM((DSA_HEADS, 1), F32),
                        pltpu.VMEM((DSA_HEADS, DSA_DH), F32)],
    )
    return pl.pallas_call(
        functools.partial(_dsa_sample_attn_kernel, n_pg=n_pg),
        grid_spec=grid_spec,
        out_shape=jax.ShapeDtypeStruct((Bd, DSA_HEADS, DSA_DH), F32),
        compiler_params=_cparams("parallel", "arbitrary"),
        name="dsa_sample_attn",
    )(page_table, q8, sel.reshape(Bd, 1, -1), sel_new.reshape(Bd, 1, PAGE_SIZE), k_new_b, v_new_b,
      *([cache_kt] * n_pg), *([cache_vt] * n_pg))


def _attend_rows(q, kk, vv, scale):
    s = jnp.sum(kk * q[None], axis=-1, keepdims=True) * scale
    m = jnp.max(s, axis=0, keepdims=True)
    p = jnp.exp(s - m)
    l = jnp.sum(p, axis=0)
    return jnp.sum(p * vv, axis=0) / l


def _mem_prompt_kernel(q_ref, mk_ref, mv_ref, o_ref):
    q = q_ref[0]
    mk = mk_ref[0]
    mv = mv_ref[0]
    outs = []
    for h in range(MEM_HEADS):
        sl = slice(h * MEM_DH, (h + 1) * MEM_DH)
        s = _dot_nt(q[:, sl], mk[:, sl]) * (MEM_DH ** -0.5)
        m = jnp.max(s, axis=1, keepdims=True)
        p = jnp.exp(s - m)
        p = p / jnp.sum(p, axis=1, keepdims=True)
        outs.append(_dot(p, mv[:, sl]))
    o_ref[0] = jnp.concatenate(outs, axis=1)


def _mem_prompt(q, mk, mv, tq):
    B, T, W = q.shape
    M = mk.shape[1]
    kv = pl.BlockSpec((1, M, W), lambda b, i: (b, 0, 0))
    return pl.pallas_call(
        _mem_prompt_kernel,
        grid=(B, T // tq),
        in_specs=[pl.BlockSpec((1, tq, W), lambda b, i: (b, i, 0)), kv, kv],
        out_specs=pl.BlockSpec((1, tq, W), lambda b, i: (b, i, 0)),
        out_shape=jax.ShapeDtypeStruct((B, T, W), F32),
        compiler_params=_cparams("parallel", "parallel"),
        name="mem_prompt",
    )(q, mk, mv)


def _mem_sample_kernel(q_ref, k_ref, v_ref, o_ref, *, ns):
    for i in range(ns):
        o_ref[i] = _attend_rows(q_ref[i], k_ref[i], v_ref[i], MEM_DH ** -0.5)


def _mem_sample(q, ck, cv, ns):
    Bd, M = ck.shape[:2]
    kv = pl.BlockSpec((ns, M, MEM_HEADS, MEM_DH), lambda b: (b, 0, 0, 0))
    qs = pl.BlockSpec((ns, MEM_HEADS, MEM_DH), lambda b: (b, 0, 0))
    return pl.pallas_call(
        functools.partial(_mem_sample_kernel, ns=ns),
        grid=(Bd // ns,),
        in_specs=[qs, kv, kv],
        out_specs=qs,
        out_shape=jax.ShapeDtypeStruct((Bd, MEM_HEADS, MEM_DH), F32),
        compiler_params=_cparams("parallel"),
        name="mem_sample",
    )(q, ck, cv)


def _merge_kernel(x_ref, og_ref, od_ref, om_ref, gn_ref, wg_ref, wb_ref, wo_ref, fn_ref, xo_ref, h2_ref):
    x = x_ref[...]
    d = x.shape[1]
    hb = _rms(x, gn_ref[...]).astype(BF16)
    acc = jnp.zeros_like(x)
    for n, o_ref in enumerate((og_ref, od_ref, om_ref)):
        gate = _sigmoid(jnp.dot(hb, wg_ref[:, n * d:(n + 1) * d], preferred_element_type=F32))
        acc = acc + gate * jnp.dot(_bf(o_ref[...]), wb_ref[n], preferred_element_type=F32)
    xo = x + jnp.dot(_bf(acc), wo_ref[...], preferred_element_type=F32)
    xo_ref[...] = xo
    h2_ref[...] = _rms(xo, fn_ref[...])


def _merge(x, o_g, o_d, o_m, gn, wg, wb, wo, fn, tm):
    n, d = x.shape
    tok = lambda w: pl.BlockSpec((tm, w), lambda i: (i, 0))
    full = lambda shape: pl.BlockSpec(shape, lambda i: (0,) * len(shape))
    return pl.pallas_call(
        _merge_kernel,
        grid=(n // tm,),
        in_specs=[tok(d), tok(BRANCH_W), tok(BRANCH_W), tok(BRANCH_W), full((1, d)), full((d, N_BRANCH * d)),
                  full((N_BRANCH, BRANCH_W, d)), full((d, d)), full((1, d))],
        out_specs=[tok(d), tok(d)],
        out_shape=[jax.ShapeDtypeStruct((n, d), F32)] * 2,
        compiler_params=_cparams("parallel"),
        name="merge",
    )(x, o_g, o_d, o_m, gn.reshape(1, d), wg, wb, wo, fn.reshape(1, d))


def _route_kernel(h_ref, wr_ref, br_ref, e_ref, p_ref, r_ref, cnt_ref, run_scr, *, tr):
    i = pl.program_id(0)

    @pl.when(i == 0)
    def _():
        run_scr[...] = jnp.zeros_like(run_scr)

    logits = _dot_nt(wr_ref[...], h_ref[...]) + br_ref[...]
    eidx = _iota((N_EXPERTS, LANES), 0)
    su = jnp.where(_iota((LANES, LANES), 0) < _iota((LANES, LANES), 1), 1.0, 0.0).astype(BF16)
    run = run_scr[...]
    for c in range(tr // LANES):
        sl = slice(c * LANES, (c + 1) * LANES)
        l = logits[:, sl]
        vals, idxs = [], []
        for _ in range(TOP_K):
            m = jnp.max(l, axis=0, keepdims=True)
            idx = jnp.min(jnp.where(l == m, eidx, N_EXPERTS), axis=0, keepdims=True)
            vals.append(m)
            idxs.append(idx)
            l = jnp.where(eidx == idx, -jnp.inf, l)
        ex = [jnp.exp(v - vals[0]) for v in vals]
        den = ex[0] + ex[1] + ex[2] + ex[3]
        oh = jnp.zeros((N_EXPERTS, LANES), F32)
        for k in range(TOP_K):
            p_ref[k:k + 1, sl] = ex[k] / den
            e_ref[k:k + 1, sl] = idxs[k]
            oh = oh + jnp.where(eidx == idxs[k], 1.0, 0.0)
        before = jnp.dot(oh.astype(BF16), su, preferred_element_type=F32) + run
        for k in range(TOP_K):
            rk = jnp.sum(jnp.where(eidx == idxs[k], before, 0.0), axis=0, keepdims=True)
            r_ref[k:k + 1, sl] = rk.astype(I32)
        run = run + jnp.sum(oh, axis=1, keepdims=True)
    run_scr[...] = run
    cnt_ref[...] = jnp.broadcast_to(run, cnt_ref.shape)


def _route(h2, wr_t, br, tr):
    n, d = h2.shape
    tokrow = pl.BlockSpec((TOP_K, tr), lambda i: (0, i))
    return pl.pallas_call(
        functools.partial(_route_kernel, tr=tr),
        grid=(n // tr,),
        in_specs=[pl.BlockSpec((tr, d), lambda i: (i, 0)),
                  pl.BlockSpec((N_EXPERTS, d), lambda i: (0, 0)),
                  pl.BlockSpec((N_EXPERTS, 1), lambda i: (0, 0))],
        out_specs=[tokrow, tokrow, tokrow, pl.BlockSpec((N_EXPERTS, LANES), lambda i: (0, 0))],
        out_shape=[jax.ShapeDtypeStruct((TOP_K, n), I32), jax.ShapeDtypeStruct((TOP_K, n), F32),
                   jax.ShapeDtypeStruct((TOP_K, n), I32), jax.ShapeDtypeStruct((N_EXPERTS, LANES), F32)],
        scratch_shapes=[pltpu.VMEM((N_EXPERTS, 1), F32)],
        compiler_params=_cparams("arbitrary"),
        name="moe_route",
    )(h2, wr_t, br)


def _dispatch_kernel(plo_ref, pn_ref, nb_ref, dest_ref, h_ref, xg_hbm, zblk, sem, *, td):
    blk = zblk.shape[0]
    n_blocks = xg_hbm.shape[0] // blk

    def row_copy(t, dst):
        return pltpu.make_async_copy(h_ref.at[pl.ds(t, 1)], xg_hbm.at[pl.ds(dst, 1)], sem.at[0])

    def zero_block(i):
        return pltpu.make_async_copy(zblk, xg_hbm.at[pl.ds(pl.multiple_of(i * blk, blk), blk)], sem.at[2])

    def issue(t, carry):
        for k in range(TOP_K):
            row_copy(t, dest_ref[k, t]).start()
        return carry

    lax.fori_loop(0, td, issue, 0, unroll=8)

    @pl.when(pl.program_id(0) == 0)
    def _():
        zblk[...] = jnp.zeros_like(zblk)

        def tail_start(i, c):
            zero_block(i).start()
            return c

        def tail_wait(i, c):
            zero_block(0).wait()
            return c

        lax.fori_loop(nb_ref[0], n_blocks, tail_start, 0)
        lax.fori_loop(nb_ref[0], n_blocks, tail_wait, 0)

        big = [1 << i for i in reversed(range(3, blk.bit_length() - 1))]

        def per_expert(e, carry):
            lo = plo_ref[e]
            n = pn_ref[e]
            head = jnp.minimum(n, (-lo) & 7)
            body = n - head
            for wait in (False, True):
                def zero(off, size, on):
                    copy = pltpu.make_async_copy(zblk.at[pl.ds(0, size)], xg_hbm.at[pl.ds(off, size)], sem.at[1])

                    @pl.when(on)
                    def _():
                        copy.wait() if wait else copy.start()

                for r in range(7):
                    zero(lo + r, 1, r < head)
                off = lo + head
                for size in big:
                    zero(pl.multiple_of(off, 8), size, (body & size) != 0)
                    off = off + (body & size)
                for r in range(7):
                    zero(off + r, 1, r < (body & 7))
            return carry

        lax.fori_loop(0, N_EXPERTS, per_expert, 0)

    for k in range(TOP_K):
        pltpu.make_async_copy(h_ref, xg_hbm.at[pl.ds(0, td)], sem.at[0]).wait()


def _dispatch(h2, dest, pad_lo, pad_n, nb_used, n_slots, blk, td):
    n, d = h2.shape
    grid_spec = pltpu.PrefetchScalarGridSpec(
        num_scalar_prefetch=3,
        grid=(n // td,),
        in_specs=[pl.BlockSpec((TOP_K, td), lambda i, plo, pn, nb: (0, i), memory_space=pltpu.SMEM),
                  pl.BlockSpec((td, d), lambda i, plo, pn, nb: (i, 0))],
        out_specs=pl.BlockSpec(memory_space=pl.ANY),
        scratch_shapes=[pltpu.VMEM((blk, d), F32), pltpu.SemaphoreType.DMA((3,))],
    )
    return pl.pallas_call(
        functools.partial(_dispatch_kernel, td=td),
        grid_spec=grid_spec,
        out_shape=jax.ShapeDtypeStruct((n_slots, d), F32),
        compiler_params=_cparams("arbitrary"),
        name="moe_dispatch",
    )(pad_lo, pad_n, nb_used, dest, h2)


def _expert_kernel(be_ref, nb_ref, x_ref, wgu_ref, bgu_ref, wdn_ref, bdn_ref, o_ref):
    del be_ref
    f = wdn_ref.shape[1]

    used = pl.program_id(0) < nb_ref[0]

    @pl.when(jnp.logical_not(used))
    def _():
        o_ref[...] = jnp.zeros_like(o_ref)

    @pl.when(used)
    def _():
        gu = jnp.dot(_bf(x_ref[...]), _bf(wgu_ref[0]), preferred_element_type=F32) + bgu_ref[0]
        gate = jnp.minimum(gu[:, :f], SWIGLU_LIMIT)
        up = jnp.clip(gu[:, f:], -SWIGLU_LIMIT, SWIGLU_LIMIT)
        glu = gate * _sigmoid(SWIGLU_ALPHA * gate)
        o_ref[...] = jnp.dot(_bf((up + 1.0) * glu), _bf(wdn_ref[0]), preferred_element_type=F32) + bdn_ref[0]


def _experts(xg, block_e, nb_used, wgu, bgu, wdn, bdn, blk):
    n_slots, d = xg.shape
    f = wdn.shape[1]
    blk_of = lambda i, nb: jnp.minimum(i, nb[0] - 1)
    tok = pl.BlockSpec((blk, d), lambda i, be, nb: (blk_of(i, nb), 0))
    per_e = lambda shape: pl.BlockSpec((1,) + shape, lambda i, be, nb: (be[blk_of(i, nb)], 0, 0))
    grid_spec = pltpu.PrefetchScalarGridSpec(
        num_scalar_prefetch=2,
        grid=(n_slots // blk,),
        in_specs=[tok, per_e((d, 2 * f)), per_e((1, 2 * f)), per_e((f, d)), per_e((1, d))],
        out_specs=pl.BlockSpec((blk, d), lambda i, be, nb: (i, 0)),
    )
    return pl.pallas_call(
        _expert_kernel,
        grid_spec=grid_spec,
        out_shape=jax.ShapeDtypeStruct((n_slots, d), F32),
        compiler_params=_cparams("arbitrary"),
        name="moe_experts",
    )(block_e, nb_used, xg, wgu, bgu, wdn, bdn)


def _combine_kernel(dest_ref, x_ref, p_ref, g_ref, yb_hbm, o_ref, buf, sem, *, tc):
    def row_copy(src, k, t):
        return pltpu.make_async_copy(yb_hbm.at[pl.ds(src, 1)], buf.at[k, pl.ds(t, 1)], sem.at[0])

    def issue(t, carry):
        for k in range(TOP_K):
            row_copy(dest_ref[k, t], k, t).start()
        return carry

    lax.fori_loop(0, tc, issue, 0, unroll=8)
    for k in range(TOP_K):
        pltpu.make_async_copy(yb_hbm.at[pl.ds(0, tc)], buf.at[k], sem.at[0]).wait()
    p = p_ref[...]
    acc = p[:, 0:1] * buf[0]
    for k in range(1, TOP_K):
        acc = acc + p[:, k:k + 1] * buf[k]
    o_ref[...] = _rms(x_ref[...] + acc, g_ref[...])


def _combine(x, yb, dest, p_t, g, tc):
    n, d = x.shape
    return pl.pallas_call(
        functools.partial(_combine_kernel, tc=tc),
        grid=(n // tc,),
        in_specs=[pl.BlockSpec((TOP_K, tc), lambda i: (0, i), memory_space=pltpu.SMEM),
                  pl.BlockSpec((tc, d), lambda i: (i, 0)),
                  pl.BlockSpec((tc, TOP_K), lambda i: (i, 0)),
                  pl.BlockSpec((1, d), lambda i: (0, 0)),
                  pl.BlockSpec(memory_space=pl.ANY)],
        out_specs=pl.BlockSpec((tc, d), lambda i: (i, 0)),
        out_shape=jax.ShapeDtypeStruct((n, d), F32),
        scratch_shapes=[pltpu.VMEM((TOP_K, tc, d), F32), pltpu.SemaphoreType.DMA((1,))],
        compiler_params=_cparams("arbitrary"),
        name="moe_combine",
    )(dest, x, p_t, g.reshape(1, d), yb)


def _moe_final(x, h2, g_final, wr_t, br, wgu, bgu, wdn, bdn, blk, tile):
    n, d = x.shape
    e, p, rank, cnt = _route(h2, wr_t, br, tile)
    counts = cnt[:, 0].astype(I32)
    padded = (counts + blk - 1) // blk * blk
    pad_end = jnp.cumsum(padded)
    pad_start = pad_end - padded
    n_blocks = -(-n * TOP_K // blk) + N_EXPERTS
    expert_ids = jnp.arange(N_EXPERTS, dtype=I32)[:, None, None]
    dest = rank + jnp.sum(jnp.where(e[None] == expert_ids, pad_start[:, None, None], 0), axis=0)
    block_lo = jnp.arange(n_blocks, dtype=I32)[:, None] * blk
    block_e = jnp.minimum(jnp.sum((pad_end[None, :] <= block_lo).astype(I32), axis=1), N_EXPERTS - 1)
    nb_used = (pad_end[-1:] // blk).astype(I32)
    xg = _dispatch(h2, dest, pad_start + counts, padded - counts, nb_used, n_blocks * blk, blk, tile)
    yb = _experts(xg, block_e, nb_used, wgu, bgu, wdn, bdn, blk)
    return _combine(x, yb, dest, p.T, g_final, min(tile, 256))


def _prep_w_in(w):
    cuts = np.cumsum((0,) + _SPLITS)
    seg = lambda i: w[:, int(cuts[i]):int(cuts[i + 1])]
    small = jnp.concatenate([seg(8), seg(9), seg(2), seg(3), jnp.zeros((w.shape[0], LANES - _SM_END), w.dtype)], axis=1)
    w1 = jnp.concatenate([seg(0), seg(1), seg(4), seg(5), seg(6), seg(7), seg(10), small], axis=1)
    return w1.astype(BF16), w[:, int(cuts[-1]):].astype(BF16)


def kernel(x_prompt, x_sample, cache_k, cache_v, cache_k_idx, cache_mem_k, cache_mem_v, state_gdn, state_conv,
           page_table, mem_prompt, norm_attn, w_in, conv_w, gdn_a_log, gdn_dt_bias, gdn_norm, norm_mem, w_mem_kv,
           w_branch, w_out, norm_ffn, w_router, b_router, w_gate_up, b_gate_up, w_down, b_down, norm_final):
    B, T, D = x_prompt.shape
    Bd, Ts, _ = x_sample.shape
    assert Ts == 1 and w_in.shape[0] == 1, "one layer, one new token per sample"
    H = GDN_HEADS
    G = DSA_HEADS // DSA_KV_HEADS
    n_pages = page_table.shape[1]
    M = mem_prompt.shape[1]

    w1, wg = _prep_w_in(w_in[0])
    wb = w_branch[0].astype(BF16)
    wo = w_out[0].astype(BF16)
    wr_t = w_router[0].T
    br = b_router[0].reshape(N_EXPERTS, 1)
    wgu = w_gate_up[0]
    wdn = w_down[0]
    bgu = b_gate_up[0].reshape(N_EXPERTS, 1, -1)
    bdn = b_down[0].reshape(N_EXPERTS, 1, -1)
    alog = gdn_a_log[0]
    dtb = gdn_dt_bias[0]

    xp = x_prompt.reshape(B * T, D)
    g_qkv, g_z, d_k, d_v, i_q, m_q, small, qh, kt, vh = _proj_prompt(xp, norm_attn[0], w1, B, T, min(512, T))

    quad_rows = lambda a: (a.reshape(B, T // GDN_CHUNK, GDN_CHUNK, 2, 4).transpose(0, 1, 3, 4, 2)
                           .reshape(B, T // GDN_CHUNK, 2, 1, QW))
    per_group = lambda a: jnp.repeat(a, GDN_DK).reshape(2, 1, QW)
    o_g, ssm_q = _gdn_prompt(g_qkv.reshape(B, T, GDN_QKV), g_z.reshape(B, T, H * GDN_DV),
                             quad_rows(small[:, _SM_GA:_SM_GA + H]), quad_rows(small[:, _SM_GB:_SM_GB + H]),
                             conv_w[0], per_group(alog), per_group(dtb), jnp.tile(gdn_norm[0], 4).reshape(1, QW),
                             min(512, T))
    o_g = o_g.reshape(B * T, H * GDN_DV)
    ssm_p = ssm_q.reshape(B, 2, GDN_DK, 4, GDN_DV).transpose(0, 1, 3, 2, 4).reshape(B, H, GDN_DK, GDN_DV)

    o_d = _dsa_prompt(i_q.reshape(B, T, -1), small.reshape(B, T, LANES), qh, kt, vh, min(TOPK_MAX, T // 4),
                      min(256, T))
    o_d = o_d.reshape(B * T, DSA_HEADS * DSA_DH)

    mk, mv = _norm_matmul(mem_prompt.reshape(B * M, D), norm_mem[0], w_mem_kv[0].astype(BF16),
                          (MEM_HEADS * MEM_DH,) * 2, min(512, B * M))
    o_m = _mem_prompt(m_q.reshape(B, T, -1), mk.reshape(B, M, -1), mv.reshape(B, M, -1), min(512, T))
    o_m = o_m.reshape(B * T, MEM_HEADS * MEM_DH)

    xres, h2 = _merge(xp, o_g, o_d, o_m, norm_attn[0], wg, wb, wo, norm_ffn[0], min(256, B * T))
    y_prompt = _moe_final(xres, h2, norm_final, wr_t, br, wgu, bgu, wdn, bdn, 256, min(512, B * T))

    k_prompt = d_k.reshape(1, B, T, DSA_KV_HEADS, DSA_DH)
    v_prompt = d_v.reshape(1, B, T, DSA_KV_HEADS, DSA_DH)
    kidx_prompt = small[:, _SM_IK:_SM_IK + IDX_DIM].reshape(1, B, T, IDX_DIM)
    memk_prompt = mk.reshape(1, B, M, MEM_HEADS, MEM_DH)
    memv_prompt = mv.reshape(1, B, M, MEM_HEADS, MEM_DH)
    conv_prompt = g_qkv.reshape(B, T, GDN_QKV)[:, T - (GDN_CONV - 1):, :][None]

    xs = x_sample.reshape(Bd, D)
    s_qkv, s_z, sd_q, sd_k, sd_v, si_q, sm_q, ssmall = _norm_matmul(xs, norm_attn[0], w1, _PROJ_SPLITS, Bd)

    lanes_b = lambda a: jnp.broadcast_to(a[..., None], a.shape + (Bd,))
    og_t, s_t = _gdn_sample(
        s_qkv.T, state_conv[0].transpose(1, 2, 0), lanes_b(conv_w[0]), s_z.T,
        ssmall[:, _SM_GA:_SM_GA + H].T.reshape(H, 1, Bd), ssmall[:, _SM_GB:_SM_GB + H].T.reshape(H, 1, Bd),
        lanes_b(alog.reshape(H, 1)), lanes_b(dtb.reshape(H, 1)), lanes_b(gdn_norm[0]),
        state_gdn[0].transpose(1, 2, 3, 0))
    so_g = og_t.T
    ssm_sample = s_t.transpose(3, 0, 1, 2)[None]
    conv_sample = jnp.concatenate([state_conv[0][:, 1:], s_qkv[:, None, :]], axis=1)[None]

    qi8 = jnp.pad(si_q.reshape(Bd, IDX_HEADS, IDX_DIM), ((0, 0), (0, 8 - IDX_HEADS), (0, 0)))
    wi8 = jnp.pad(ssmall[:, _SM_IW:_SM_IW + IDX_HEADS], ((0, 0), (0, 8 - IDX_HEADS)))[..., None]
    n_pg = 32 if n_pages % 32 == 0 else n_pages
    cache_kit = cache_k_idx[0].transpose(0, 2, 1)
    cache_kt = cache_k[0].transpose(0, 2, 3, 1)
    cache_vt = cache_v[0].transpose(0, 2, 3, 1)
    scores = _dsa_sample_scores(page_table, qi8, wi8, cache_kit, 64 if n_pages % 64 == 0 else n_pg)
    scores = scores.reshape(Bd, n_pages * PAGE_SIZE)
    n_sel = min(TOPK_MAX, (n_pages * PAGE_SIZE + 1) // 4)
    sel, sel_new = _dsa_sample_select(scores, si_q, ssmall, n_sel)
    over_lanes = lambda a: jnp.broadcast_to(a[..., None], a.shape + (PAGE_SIZE,))
    head_kv = jnp.arange(DSA_HEADS, dtype=I32) // G
    q8 = jnp.where((head_kv[None, :] == jnp.arange(DSA_KV_HEADS, dtype=I32)[:, None])[None, :, :, None],
                   sd_q.reshape(Bd, 1, DSA_HEADS, DSA_DH), 0.0)
    k_new = sd_k.reshape(Bd, DSA_KV_HEADS, DSA_DH)
    v_new = sd_v.reshape(Bd, DSA_KV_HEADS, DSA_DH)
    so_d = _dsa_sample_attn(page_table, q8, sel, sel_new, over_lanes(k_new), over_lanes(v_new),
                            cache_kt, cache_vt, n_pg).reshape(Bd, DSA_HEADS * DSA_DH)

    so_m = _mem_sample(sm_q.reshape(Bd, MEM_HEADS, MEM_DH), cache_mem_k[0], cache_mem_v[0], 4)
    so_m = so_m.reshape(Bd, MEM_HEADS * MEM_DH)

    sres, sh2 = _merge(xs, so_g, so_d, so_m, norm_attn[0], wg, wb, wo, norm_ffn[0], Bd)
    y_sample = _moe_final(sres, sh2, norm_final, wr_t, br, wgu, bgu, wdn, bdn, 256, Bd)

    return (y_prompt.reshape(B, T, D), y_sample.reshape(Bd, 1, D), k_prompt, v_prompt, kidx_prompt,
            memk_prompt, memv_prompt, ssm_p[None], conv_prompt,
            k_new.reshape(1, Bd, 1, DSA_KV_HEADS, DSA_DH), v_new.reshape(1, Bd, 1, DSA_KV_HEADS, DSA_DH),
            ssmall[:, _SM_IK:_SM_IK + IDX_DIM].reshape(1, Bd, 1, IDX_DIM), ssm_sample, conv_sample)
```

```python
import functools

import numpy as np
import jax
import jax.numpy as jnp
from jax import lax
from jax.experimental import pallas as pl
from jax.experimental.pallas import tpu as pltpu

F32 = jnp.float32
BF16 = jnp.bfloat16
I32 = jnp.int32

EPS = 1e-6
GDN_HEADS = 8
GDN_DK = 64
GDN_DV = 64
GDN_CONV = 4
GDN_CHUNK = 64
GDN_QKV = 2 * GDN_HEADS * GDN_DK + GDN_HEADS * GDN_DV
DSA_HEADS = 8
DSA_KV_HEADS = 4
DSA_DH = 64
IDX_HEADS = 4
IDX_DIM = 64
TOPK_MAX = 256
Q_BLOCK = 128
MEM_HEADS = 4
MEM_DH = 128
N_BRANCH = 3
BRANCH_W = 512
N_EXPERTS = 32
TOP_K = 4
SWIGLU_LIMIT = 7.0
SWIGLU_ALPHA = 1.702
PAGE_SIZE = 128
LANES = 128

_SPLITS = (GDN_QKV, GDN_HEADS * GDN_DV, GDN_HEADS, GDN_HEADS,
           DSA_HEADS * DSA_DH, DSA_KV_HEADS * DSA_DH, DSA_KV_HEADS * DSA_DH,
           IDX_HEADS * IDX_DIM, IDX_DIM, IDX_HEADS,
           MEM_HEADS * MEM_DH)
_PROJ_SPLITS = (GDN_QKV, 512, 512, 256, 256, 256, 512, LANES)
_SM_IK = 0
_SM_IW = IDX_DIM
_SM_GB = _SM_IW + IDX_HEADS
_SM_GA = _SM_GB + GDN_HEADS
_SM_END = _SM_GA + GDN_HEADS

VMEM_LIMIT_BYTES = 56 * 1024 * 1024


def _cparams(*sem):
    return pltpu.CompilerParams(dimension_semantics=sem, vmem_limit_bytes=VMEM_LIMIT_BYTES)


def _bf(x):
    return x.astype(BF16)


def _dot(a, b):
    return jnp.dot(_bf(a), _bf(b), preferred_element_type=F32)


_NT = (((1,), (1,)), ((), ()))


def _dot_nt(a, b):
    return lax.dot_general(_bf(a), _bf(b), _NT, preferred_element_type=F32)


def _split2(x):
    hi = x.astype(BF16)
    lo = (x - hi.astype(F32)).astype(BF16)
    return hi, lo


def _split3(x):
    hi = x.astype(BF16)
    r = x - hi.astype(F32)
    mid = r.astype(BF16)
    lo = (r - mid.astype(F32)).astype(BF16)
    return hi, mid, lo


def _dot_nt3(a, b):
    ah, al = _split2(a)
    bh, bl = _split2(b)
    d = lambda x, y: lax.dot_general(x, y, _NT, preferred_element_type=F32)
    return d(ah, bh) + (d(ah, bl) + d(al, bh))


def _dot3(a, b):
    ah, al = _split2(a)
    bh, bl = _split2(b)
    d = lambda x, y: jnp.dot(x, y, preferred_element_type=F32)
    return d(ah, bh) + (d(ah, bl) + d(al, bh))


def _dot_exact01(a, b01):
    hi, mid, lo = _split3(a)
    d = lambda x: jnp.dot(x, b01, preferred_element_type=F32)
    return d(hi) + (d(mid) + d(lo))


def _rms(x, g):
    return x * lax.rsqrt(jnp.mean(x * x, axis=-1, keepdims=True) + EPS) * g


def _sigmoid(x):
    return 1.0 / (1.0 + jnp.exp(-x))


def _silu(x):
    return x * _sigmoid(x)


def _softplus(x):
    return jnp.maximum(x, 0.0) + jnp.log(1.0 + jnp.exp(-jnp.abs(x)))


def _iota(shape, axis):
    return lax.broadcasted_iota(I32, shape, axis)


def _norm_matmul_kernel(x_ref, g_ref, w_ref, *o_refs, splits):
    hb = _rms(x_ref[...], g_ref[...]).astype(BF16)
    off = 0
    for o_ref, n in zip(o_refs, splits):
        o_ref[...] = jnp.dot(hb, w_ref[:, off:off + n], preferred_element_type=F32)
        off += n


def _norm_matmul(x, g, w, splits, tm):
    n, d = x.shape
    return pl.pallas_call(
        functools.partial(_norm_matmul_kernel, splits=splits),
        grid=(n // tm,),
        in_specs=[pl.BlockSpec((tm, d), lambda i: (i, 0)),
                  pl.BlockSpec((1, d), lambda i: (0, 0)),
                  pl.BlockSpec((d, sum(splits)), lambda i: (0, 0))],
        out_specs=[pl.BlockSpec((tm, s), lambda i: (i, 0)) for s in splits],
        out_shape=[jax.ShapeDtypeStruct((n, s), F32) for s in splits],
        compiler_params=_cparams("parallel"),
        name="norm_matmul",
    )(x, g.reshape(1, d), w)


def _proj_prompt_kernel(x_ref, g_ref, w_ref, qkv_ref, z_ref, dk_ref, dv_ref, iq_ref, mq_ref, sm_ref,
                        qh_ref, kt_ref, vh_ref):
    hb = _rms(x_ref[...], g_ref[...]).astype(BF16)
    offs = np.cumsum((0,) + _PROJ_SPLITS)
    part = lambda i: jnp.dot(hb, w_ref[:, int(offs[i]):int(offs[i + 1])], preferred_element_type=F32)
    qkv_ref[...] = part(0)
    z_ref[...] = part(1)
    dq = part(2)
    for h in range(DSA_HEADS):
        qh_ref[0, h] = dq[:, h * DSA_DH:(h + 1) * DSA_DH]
    dk = part(3)
    dk_ref[...] = dk
    kt_ref[0] = dk.T.reshape(DSA_KV_HEADS, DSA_DH, dk.shape[0])
    dv = part(4)
    dv_ref[...] = dv
    for kv in range(DSA_KV_HEADS):
        vh_ref[0, kv] = dv[:, kv * DSA_DH:(kv + 1) * DSA_DH]
    iq_ref[...] = part(5)
    mq_ref[...] = part(6)
    sm_ref[...] = part(7)


def _proj_prompt(x, g, w, B, T, tm):
    n, d = x.shape
    tpb = T // tm
    tok = lambda width: pl.BlockSpec((tm, width), lambda b, t: (b * tpb + t, 0))
    flat = lambda width: jax.ShapeDtypeStruct((n, width), F32)
    sp = _PROJ_SPLITS
    return pl.pallas_call(
        _proj_prompt_kernel,
        grid=(B, tpb),
        in_specs=[tok(d), pl.BlockSpec((1, d), lambda b, t: (0, 0)), pl.BlockSpec((d, sum(sp)), lambda b, t: (0, 0))],
        out_specs=[tok(sp[0]), tok(sp[1]), tok(sp[3]), tok(sp[4]), tok(sp[5]), tok(sp[6]), tok(sp[7]),
                   pl.BlockSpec((1, DSA_HEADS, tm, DSA_DH), lambda b, t: (b, 0, t, 0)),
                   pl.BlockSpec((1, DSA_KV_HEADS, DSA_DH, tm), lambda b, t: (b, 0, 0, t)),
                   pl.BlockSpec((1, DSA_KV_HEADS, tm, DSA_DH), lambda b, t: (b, 0, t, 0))],
        out_shape=[flat(sp[0]), flat(sp[1]), flat(sp[3]), flat(sp[4]), flat(sp[5]), flat(sp[6]), flat(sp[7]),
                   jax.ShapeDtypeStruct((B, DSA_HEADS, T, DSA_DH), F32),
                   jax.ShapeDtypeStruct((B, DSA_KV_HEADS, DSA_DH, T), F32),
                   jax.ShapeDtypeStruct((B, DSA_KV_HEADS, T, DSA_DH), F32)],
        compiler_params=_cparams("parallel", "parallel"),
        name="proj_prompt",
    )(x, g.reshape(1, d), w)


QW = 4 * GDN_DK


def _bd_stack(x):
    lane_head = (_iota(x.shape, 1) >> 6) & 3
    return jnp.concatenate([jnp.where(lane_head == h, x, 0.0) for h in range(4)], axis=0)


def _bd_dot(a, x):
    return jnp.dot(_bf(a), _bf(_bd_stack(x)), preferred_element_type=F32)


def _bd_split(x):
    xh = x.astype(BF16).astype(F32)
    return _bf(_bd_stack(xh)), _bf(_bd_stack(x - xh))


def _bd_dot3(a, xb):
    ah, al = _split2(a)
    bh, bl = xb
    d = lambda p, q: jnp.dot(p, q, preferred_element_type=F32)
    return d(ah, bh) + (d(ah, bl) + d(al, bh))


def _gdn_prompt_kernel(q_ref, k_ref, v_ref, qh_ref, kh_ref, vh_ref, z_ref, a_ref, b_ref,
                       cwq_ref, cwk_ref, cwv_ref, alog_ref, dtb_ref, nw_ref, o_ref, s_ref,
                       xq, xk, xv, s_scr, *, tt):
    ti = pl.program_id(1)
    C = GDN_CHUNK
    HALO = 8
    NG = GDN_HEADS // 4

    @pl.when(ti == 0)
    def _():
        s_scr[...] = jnp.zeros_like(s_scr)

    r = _iota((QW, QW), 0)
    c = _iota((QW, QW), 1)
    same = (r >> 6) == (c >> 6)
    ones_bd = jnp.where(same, 1.0, 0.0).astype(BF16)
    su_bd = jnp.where(same, jnp.where((r & 63) > (c & 63), 1.0, 0.0), 0.0).astype(BF16)
    rhs01 = jnp.concatenate([su_bd, ones_bd], axis=1)
    ident = jnp.where(r == c, 1.0, 0.0).astype(BF16)

    keep = (ti > 0).astype(F32)
    for idx, (src, halo, dst, cw) in enumerate(((q_ref, qh_ref, xq, cwq_ref), (k_ref, kh_ref, xk, cwk_ref),
                                                (v_ref, vh_ref, xv, cwv_ref))):
        dst[0:HALO, :] = halo[0] * keep
        dst[HALO:HALO + tt, :] = src[0]
        w = cw[...]
        y = w[0:1, :] * dst[pl.ds(HALO - 3, tt), :]
        for i in range(1, GDN_CONV):
            y = y + w[i:i + 1, :] * dst[pl.ds(HALO - 3 + i, tt), :]
        y = _silu(y)
        if idx < 2:
            ss = jnp.concatenate([_dot_exact01((y * y)[:, g * QW:(g + 1) * QW], ones_bd) for g in range(NG)], axis=1)
            y = y * lax.rsqrt(ss + EPS)
            if idx == 0:
                y = y * (GDN_DK ** -0.5)
        dst[HALO:HALO + tt, :] = y

    ri = _iota((C, QW), 0)
    li = _iota((C, QW), 1) & 63
    lane_head = _iota((C, QW), 1) >> 6
    incl = li <= ri
    strict = li < ri
    eye = li == ri
    nw = nw_ref[...]

    def solve_stage(cis):
        streams = [(ci, g) for ci in cis for g in range(NG)]
        n = range(len(streams))
        rows = [pl.multiple_of(HALO + ci * C, 8) for ci, _ in streams]
        cols = [slice(g * QW, (g + 1) * QW) for _, g in streams]
        qc = [xq[pl.ds(rows[i], C), cols[i]] for i in n]
        kc = [xk[pl.ds(rows[i], C), cols[i]] for i in n]
        vc = [xv[pl.ds(rows[i], C), cols[i]] for i in n]
        g_row = [-jnp.exp(alog_ref[g]) * _softplus(a_ref[0, ci, g] + dtb_ref[g]) for ci, g in streams]
        b_row = [_sigmoid(b_ref[0, ci, g]) for ci, g in streams]
        res = [_dot_exact01(jnp.concatenate([jnp.where(incl, g_row[i], 0.0), jnp.where(eye, b_row[i], 0.0)], axis=0),
                            rhs01) for i in n]
        gcol = [res[i][:C, QW:] for i in n]
        bcol = [res[i][C:, QW:] for i in n]
        gamma = [jnp.where(incl, jnp.exp(res[i][:C, :QW]), 0.0) for i in n]
        eg = [jnp.exp(gcol[i]) for i in n]
        glast = [gcol[i][C - 1:C, :] for i in n]
        kb = [kc[i] * bcol[i] for i in n]
        kt = [lax.dot_general(ident, jnp.concatenate([_bf(kc[i])] * 4, axis=0), _NT, preferred_element_type=F32)
              for i in n]
        kt = [_bf(jnp.where(same, kt[i], 0.0)) for i in n]
        kk = [jnp.dot(_bf(jnp.concatenate([kb[i], qc[i]], axis=0)), kt[i], preferred_element_type=F32)
              for i in n]
        r = [jnp.where(strict, -(kk[i][:C] * gamma[i]), 0.0) for i in n]
        t = [jnp.where(eye, 1.0, r[i]) for i in n]
        for k in range(6):
            rb = [_bd_split(r[i]) for i in n]
            if k > 0:
                t = [t[i] + _bd_dot3(t[i], rb[i]) for i in n]
            if k < 5:
                r = [_bd_dot3(r[i], rb[i]) for i in n]
        rhs = [_bd_split(jnp.concatenate([vc[i] * bcol[i], kb[i] * eg[i]], axis=1)) for i in n]
        y = [_bd_dot3(t[i], rhs[i]) for i in n]
        return [dict(ci=streams[i][0], g=streams[i][1], y=y[i], qg=qc[i] * eg[i], qk=kk[i][C:] * gamma[i],
                     kd=kc[i] * jnp.exp(glast[i] - gcol[i]), dec=jnp.exp(glast[i])) for i in n]

    def scan_stage(st):
        n = range(len(st))
        s = [s_scr[t["g"]] for t in st]
        ws = [_bd_dot(jnp.concatenate([st[i]["y"][:, QW:], st[i]["qg"]], axis=0), s[i]) for i in n]
        vn = [st[i]["y"][:, :QW] - ws[i][:C] for i in n]
        o = [ws[i][C:] + _bd_dot(st[i]["qk"], vn[i]) for i in n]
        kdt = [lax.dot_general(ident, _bf(st[i]["kd"]), _NT, preferred_element_type=F32) for i in n]
        full = [jnp.dot(_bf(kdt[i]), _bf(vn[i]), preferred_element_type=F32) for i in n]
        ms = [_dot_exact01(o[i] * o[i], ones_bd) * (1.0 / GDN_DV) for i in n]
        for i in n:
            g = st[i]["g"]
            sadd = jnp.where(lane_head == 0, full[i][0:C], 0.0)
            for h in range(1, 4):
                sadd = sadd + jnp.where(lane_head == h, full[i][h * C:(h + 1) * C], 0.0)
            s_scr[g] = s[i] * st[i]["dec"] + sadd
            o0 = pl.multiple_of(st[i]["ci"] * C, 8)
            cols = slice(g * QW, (g + 1) * QW)
            zc = z_ref[0, pl.ds(o0, C), cols]
            o_ref[0, pl.ds(o0, C), cols] = o[i] * lax.rsqrt(ms[i] + EPS) * nw * _silu(zc)

    def chunk_pair(pi, carry):
        st = solve_stage([2 * pi, 2 * pi + 1])
        scan_stage(st[:NG])
        scan_stage(st[NG:])
        return carry

    lax.fori_loop(0, tt // (2 * C), chunk_pair, 0)
    s_ref[0] = s_scr[...]


def _gdn_prompt(qkv, z, a, b, cw, alog, dtb, nw, tt):
    B, T, _ = qkv.shape
    nct = tt // GDN_CHUNK
    NG = GDN_HEADS // 4
    W = NG * QW
    col = lambda j: pl.BlockSpec((1, tt, W), lambda bi, ti: (bi, ti, j))
    halo = lambda j: pl.BlockSpec((1, 8, W), lambda bi, ti: (bi, jnp.maximum(ti * (tt // 8) - 1, 0), j))
    chunked = pl.BlockSpec((1, nct, NG, 1, QW), lambda bi, ti: (bi, ti, 0, 0, 0))
    cwcol = lambda j: pl.BlockSpec((GDN_CONV, W), lambda bi, ti: (0, j))
    per_group = pl.BlockSpec((NG, 1, QW), lambda bi, ti: (0, 0, 0))
    return pl.pallas_call(
        functools.partial(_gdn_prompt_kernel, tt=tt),
        grid=(B, T // tt),
        in_specs=[col(0), col(1), col(2), halo(0), halo(1), halo(2), col(0), chunked, chunked,
                  cwcol(0), cwcol(1), cwcol(2), per_group, per_group,
                  pl.BlockSpec((1, QW), lambda bi, ti: (0, 0))],
        out_specs=[col(0), pl.BlockSpec((1, NG, GDN_DK, QW), lambda bi, ti: (bi, 0, 0, 0))],
        out_shape=[jax.ShapeDtypeStruct((B, T, W), F32), jax.ShapeDtypeStruct((B, NG, GDN_DK, QW), F32)],
        scratch_shapes=[pltpu.VMEM((8 + tt, W), F32)] * 3 + [pltpu.VMEM((NG, GDN_DK, QW), F32)],
        compiler_params=_cparams("parallel", "arbitrary"),
        name="gdn_prompt",
    )(qkv, qkv, qkv, qkv, qkv, qkv, z, a, b, cw, cw, cw, alog, dtb, nw)


def _gdn_sample_kernel(q_ref, k_ref, v_ref, cq_ref, ck_ref, cv_ref, wq_ref, wk_ref, wv_ref,
                       z_ref, a_ref, b_ref, alog_ref, dtb_ref, nw_ref, s_ref, o_ref, so_ref):
    def conv(x_ref, c_ref, w_ref):
        y = w_ref[GDN_CONV - 1] * x_ref[...]
        for i in range(GDN_CONV - 1):
            y = y + w_ref[i] * c_ref[i]
        return _silu(y)

    q = conv(q_ref, cq_ref, wq_ref)
    q = q * lax.rsqrt(jnp.sum(q * q, axis=0, keepdims=True) + EPS) * (GDN_DK ** -0.5)
    k = conv(k_ref, ck_ref, wk_ref)
    k = k * lax.rsqrt(jnp.sum(k * k, axis=0, keepdims=True) + EPS)
    v = conv(v_ref, cv_ref, wv_ref)
    beta = _sigmoid(b_ref[0])
    g = -jnp.exp(alog_ref[0]) * _softplus(a_ref[0] + dtb_ref[0])
    dec = jnp.exp(g)
    ks = jnp.zeros_like(v)
    for i in range(GDN_DK):
        ks = ks + k[i:i + 1, :] * (s_ref[0, i] * dec)
    vn = beta * (v - ks)
    o = jnp.zeros_like(v)
    for i in range(GDN_DK):
        sn = s_ref[0, i] * dec + k[i:i + 1, :] * vn
        so_ref[0, i] = sn
        o = o + q[i:i + 1, :] * sn
    on = o * lax.rsqrt(jnp.mean(o * o, axis=0, keepdims=True) + EPS) * nw_ref[...]
    o_ref[...] = on * _silu(z_ref[...])


def _gdn_sample(qkv_t, conv_t, cw_t, z_t, a_t, b_t, alog_t, dtb_t, nw_t, s_t):
    nb = qkv_t.shape[1]
    H, DK, DV = GDN_HEADS, GDN_DK, GDN_DV
    row = lambda off: pl.BlockSpec((DK, nb), lambda h: (off + h, 0))
    crow = lambda off: pl.BlockSpec((GDN_CONV - 1, DK, nb), lambda h: (0, off + h, 0))
    wrow = lambda off: pl.BlockSpec((GDN_CONV, DK, nb), lambda h: (0, off + h, 0))
    per_head = pl.BlockSpec((1, 1, nb), lambda h: (h, 0, 0))
    return pl.pallas_call(
        _gdn_sample_kernel,
        grid=(H,),
        in_specs=[row(0), row(H), row(2 * H), crow(0), crow(H), crow(2 * H), wrow(0), wrow(H), wrow(2 * H),
                  row(0), per_head, per_head, per_head, per_head,
                  pl.BlockSpec((DV, nb), lambda h: (0, 0)),
                  pl.BlockSpec((1, DK, DV, nb), lambda h: (h, 0, 0, 0))],
        out_specs=[row(0), pl.BlockSpec((1, DK, DV, nb), lambda h: (h, 0, 0, 0))],
        out_shape=[jax.ShapeDtypeStruct((H * DV, nb), F32), jax.ShapeDtypeStruct((H, DK, DV, nb), F32)],
        compiler_params=_cparams("parallel"),
        name="gdn_sample",
    )(qkv_t, qkv_t, qkv_t, conv_t, conv_t, conv_t, cw_t, cw_t, cw_t, z_t, a_t, b_t, alog_t, dtb_t, nw_t, s_t)


def _count_ge(x, thr):
    return jnp.sum(jnp.where(x >= thr, 1.0, 0.0), axis=1, keepdims=True)


def _topk_mask(score, valid, k):
    rows, width = score.shape
    kf = float(k)
    x = jnp.where(valid, score, -jnp.inf)
    validf = jnp.where(valid, 1.0, 0.0)
    nvalid = jnp.sum(validf, axis=1, keepdims=True)
    few = nvalid <= kf
    mx = jnp.max(x, axis=1, keepdims=True)
    mn = jnp.min(jnp.where(valid, score, jnp.inf), axis=1, keepdims=True)
    lo0 = mn
    hi0 = mx + (jnp.abs(mx) * 1e-6 + 1e-30)
    chi0 = jnp.zeros_like(mx)
    clo0 = nvalid

    def step(_, carry):
        lo, hi, chi, clo = carry
        mid = 0.5 * lo + 0.5 * hi
        c = _count_ge(x, mid)
        ge = c >= kf
        return jnp.where(ge, mid, lo), jnp.where(ge, hi, mid), jnp.where(ge, chi, c), jnp.where(ge, c, clo)

    def finish(hi, chi):
        tau = jnp.max(jnp.where(x < hi, x, -jnp.inf), axis=1, keepdims=True)
        ceq = jnp.sum(jnp.where(x == tau, 1.0, 0.0), axis=1, keepdims=True)
        return tau, ceq

    def not_done(hi, chi, clo):
        tau, ceq = finish(hi, chi)
        bad = jnp.where(few | (clo == kf) | (chi + ceq >= kf), 0.0, 1.0)
        return jnp.max(bad) > 0.0

    def exact_cond(carry):
        it, _, _, _, clo = carry
        open_rows = jnp.max(jnp.where(few | (clo == kf), 0.0, 1.0)) > 0.0
        return jnp.logical_and(it < 6, open_rows)

    def rounds(nsteps):
        def body(carry):
            it, lo, hi, chi, clo = carry
            lo, hi, chi, clo = lax.fori_loop(0, nsteps, step, (lo, hi, chi, clo))
            return it + 1, lo, hi, chi, clo
        return body

    _, lo, hi, chi, clo = lax.while_loop(exact_cond, rounds(4), (jnp.int32(0), lo0, hi0, chi0, clo0))

    def w_cond(carry):
        it, _, hi, chi, clo = carry
        open_rows = jnp.max(jnp.where(few | (clo == kf), 0.0, 1.0)) > 0.0
        return jnp.logical_and(open_rows, jnp.logical_and(it < 48, not_done(hi, chi, clo)))

    _, lo, hi, chi, clo = lax.while_loop(w_cond, rounds(8), (jnp.int32(0), lo, hi, chi, clo))
    exact = clo == kf
    tau, ceq = finish(hi, chi)
    tau = jnp.where(exact, lo, tau)
    need = jnp.where(exact, 0.0, kf - chi)
    gtf = jnp.where(x >= jnp.where(exact, lo, hi), 1.0, 0.0)
    eqf = jnp.where(jnp.logical_and(x == tau, jnp.logical_not(exact)), 1.0, 0.0)
    has_tie = jnp.max(jnp.where(few | exact | (ceq <= need), 0.0, 1.0)) > 0.0

    def tie_path(_):
        nchunk = width // LANES
        su = jnp.where(_iota((LANES, LANES), 0) < _iota((LANES, LANES), 1), 1.0, 0.0).astype(BF16)
        run = jnp.zeros_like(need)
        pieces = []
        for c in range(nchunk):
            e = eqf[:, c * LANES:(c + 1) * LANES]
            before = jnp.dot(e.astype(BF16), su, preferred_element_type=F32) + run
            pieces.append(jnp.where(before < need, e, 0.0))
            run = run + jnp.sum(e, axis=1, keepdims=True)
        return jnp.concatenate(pieces, axis=1)

    sel_eq = lax.cond(has_tie, tie_path, lambda _: eqf, 0)
    return jnp.where(few, validf, gtf + sel_eq)


def _dsa_prompt_kernel(qi_ref, smq_ref, sma_ref, qh_ref, kt_ref, v_ref, o_ref, *, n_sel, j0, qb):
    j = j0 + pl.program_id(1)
    S = kt_ref.shape[-1]
    qi = qi_ref[0]
    wi = smq_ref[0][:, _SM_IW:_SM_IW + IDX_HEADS] * (IDX_HEADS ** -0.5 * IDX_DIM ** -0.5)
    ki = sma_ref[0][:, _SM_IK:_SM_IK + IDX_DIM]
    score = jnp.zeros((qb, S), F32)
    for h in range(IDX_HEADS):
        s = _dot_nt(qi[:, h * IDX_DIM:(h + 1) * IDX_DIM], ki)
        score = score + jnp.maximum(s, 0.0) * wi[:, h:h + 1]
    tpos = j * qb + _iota((qb, 1), 0)
    valid = _iota((1, S), 1) <= tpos
    mask = _topk_mask(score, valid, n_sel)
    bias = jnp.where(mask > 0.0, 0.0, -jnp.inf)
    bias2 = jnp.concatenate([bias, bias], axis=0)
    G = DSA_HEADS // DSA_KV_HEADS
    outs = []
    for kv in range(DSA_KV_HEADS):
        q2 = qh_ref[0, G * kv:G * (kv + 1)].reshape(G * qb, DSA_DH) * (DSA_DH ** -0.5)
        s = _dot(q2, kt_ref[0, kv]) + bias2
        p = jnp.exp(s - jnp.max(s, axis=1, keepdims=True))
        o = _dot(p, v_ref[0, kv]) / jnp.sum(p, axis=1, keepdims=True)
        outs += [o[g * qb:(g + 1) * qb] for g in range(G)]
    o_ref[0] = jnp.concatenate(outs, axis=1)


def _dsa_prompt(qi, small, qh, kt, vh, n_sel, qb):
    B, T, _ = qi.shape
    nq = T // qb
    nseg = 8 if nq % 8 == 0 else (4 if nq % 4 == 0 else 1)
    qps = nq // nseg
    outs = []
    for seg in range(nseg):
        j0 = seg * qps
        S = (seg + 1) * qps * qb
        outs.append(pl.pallas_call(
            functools.partial(_dsa_prompt_kernel, n_sel=n_sel, j0=j0, qb=qb),
            grid=(B, qps),
            in_specs=[pl.BlockSpec((1, qb, IDX_HEADS * IDX_DIM), lambda b, j, j0=j0: (b, j0 + j, 0)),
                      pl.BlockSpec((1, qb, LANES), lambda b, j, j0=j0: (b, j0 + j, 0)),
                      pl.BlockSpec((1, S, LANES), lambda b, j: (b, 0, 0)),
                      pl.BlockSpec((1, DSA_HEADS, qb, DSA_DH), lambda b, j, j0=j0: (b, 0, j0 + j, 0)),
                      pl.BlockSpec((1, DSA_KV_HEADS, DSA_DH, S), lambda b, j: (b, 0, 0, 0)),
                      pl.BlockSpec((1, DSA_KV_HEADS, S, DSA_DH), lambda b, j: (b, 0, 0, 0))],
            out_specs=pl.BlockSpec((1, qb, DSA_HEADS * DSA_DH), lambda b, j: (b, j, 0)),
            out_shape=jax.ShapeDtypeStruct((B, qps * qb, DSA_HEADS * DSA_DH), F32),
            compiler_params=_cparams("parallel", "arbitrary"),
            name="dsa_prompt",
        )(qi, small, small, qh, kt, vh))
    return outs[0] if nseg == 1 else jnp.concatenate(outs, axis=1)


def _dsa_sample_score_kernel(pt_ref, qi_ref, wi_ref, *refs, n_pg):
    del pt_ref
    o_ref = refs[n_pg]
    qi = qi_ref[0]
    wi = wi_ref[0] * (IDX_HEADS ** -0.5 * IDX_DIM ** -0.5)
    for p in range(n_pg):
        s = _dot(qi, refs[p][0])
        o_ref[0, :, p * PAGE_SIZE:(p + 1) * PAGE_SIZE] = jnp.sum(jnp.maximum(s, 0.0) * wi, axis=0, keepdims=True)


def _dsa_sample_scores(page_table, qi8, wi8, cache_kit, n_pg):
    Bd, n_pages = page_table.shape

    def page_spec(p):
        return pl.BlockSpec((1, IDX_DIM, PAGE_SIZE), lambda b, jj, pt: (pt[b, jj * n_pg + p], 0, 0))

    grid_spec = pltpu.PrefetchScalarGridSpec(
        num_scalar_prefetch=1,
        grid=(Bd, n_pages // n_pg),
        in_specs=[pl.BlockSpec((1, 8, IDX_DIM), lambda b, jj, pt: (b, 0, 0)),
                  pl.BlockSpec((1, 8, 1), lambda b, jj, pt: (b, 0, 0))] + [page_spec(p) for p in range(n_pg)],
        out_specs=pl.BlockSpec((1, 1, n_pg * PAGE_SIZE), lambda b, jj, pt: (b, 0, jj)),
    )
    return pl.pallas_call(
        functools.partial(_dsa_sample_score_kernel, n_pg=n_pg),
        grid_spec=grid_spec,
        out_shape=jax.ShapeDtypeStruct((Bd, 1, n_pages * PAGE_SIZE), F32),
        compiler_params=_cparams("parallel", "arbitrary"),
        name="dsa_sample_scores",
    )(page_table, qi8, wi8, *([cache_kit] * n_pg))


def _dsa_sample_select_kernel(sc_ref, qi_ref, sm_ref, sel_ref, seln_ref, *, n_sel):
    Bd, past = sc_ref.shape
    width = past + LANES
    sm = sm_ref[...]
    qi = _bf(qi_ref[...]).astype(F32)
    ki = _bf(sm[:, _SM_IK:_SM_IK + IDX_DIM]).astype(F32)
    wi = sm[:, _SM_IW:_SM_IW + IDX_HEADS] * (IDX_HEADS ** -0.5 * IDX_DIM ** -0.5)
    snew = jnp.zeros((Bd, 1), F32)
    for h in range(IDX_HEADS):
        s = jnp.sum(qi[:, h * IDX_DIM:(h + 1) * IDX_DIM] * ki, axis=1, keepdims=True)
        snew = snew + jnp.maximum(s, 0.0) * wi[:, h:h + 1]
    tail = jnp.where(_iota((Bd, LANES), 1) == 0, snew, -jnp.inf)
    x = jnp.concatenate([sc_ref[...], tail], axis=1)
    valid = jnp.broadcast_to(_iota((1, width), 1) <= past, (Bd, width))
    mask = _topk_mask(x, valid, n_sel)
    sel_ref[...] = mask[:, :past]
    seln_ref[...] = mask[:, past:]


def _dsa_sample_select(scores, qi, small, n_sel):
    Bd, past = scores.shape
    return pl.pallas_call(
        functools.partial(_dsa_sample_select_kernel, n_sel=n_sel),
        out_shape=[jax.ShapeDtypeStruct((Bd, past), F32), jax.ShapeDtypeStruct((Bd, LANES), F32)],
        compiler_params=pltpu.CompilerParams(vmem_limit_bytes=VMEM_LIMIT_BYTES),
        name="dsa_sample_select",
    )(scores, qi, small)


_MASKED = -1e30


def _dsa_sample_attn_kernel(pt_ref, q_ref, sel_ref, seln_ref, kn_ref, vn_ref, *refs, n_pg):
    del pt_ref
    k_refs = refs[:n_pg]
    v_refs = refs[n_pg:2 * n_pg]
    o_ref, m_scr, l_scr, acc_scr = refs[2 * n_pg:]
    jj = pl.program_id(1)
    G = DSA_HEADS // DSA_KV_HEADS
    row_kv = _iota((DSA_HEADS, DSA_DH), 0) // G

    @pl.when(jj == 0)
    def _():
        m_scr[...] = jnp.full_like(m_scr, _MASKED)
        l_scr[...] = jnp.zeros_like(l_scr)
        acc_scr[...] = jnp.zeros_like(acc_scr)

    def block(kts, vts, msk):
        s = jnp.dot(_bf(q_ref[0, 0]), kts[0], preferred_element_type=F32)
        for kv in range(1, DSA_KV_HEADS):
            s = s + jnp.dot(_bf(q_ref[0, kv]), kts[kv], preferred_element_type=F32)
        s = s * (DSA_DH ** -0.5)
        on = msk > 0.0
        m_old = m_scr[...]
        m_new = jnp.maximum(m_old, jnp.max(jnp.where(on, s, _MASKED), axis=1, keepdims=True))
        alpha = jnp.exp(m_old - m_new)
        p = jnp.where(on, jnp.exp(s - m_new), 0.0)
        l_scr[...] = l_scr[...] * alpha + jnp.sum(p, axis=1, keepdims=True)
        pb = _bf(p)
        acc = acc_scr[...] * alpha
        for kv in range(DSA_KV_HEADS):
            o_kv = lax.dot_general(pb, vts[kv], _NT, preferred_element_type=F32)
            acc = acc + jnp.where(row_kv == kv, o_kv, 0.0)
        acc_scr[...] = acc
        m_scr[...] = m_new

    gather = lambda page_refs, kv: jnp.concatenate([_bf(r[0, kv]) for r in page_refs], axis=1)
    block([gather(k_refs, kv) for kv in range(DSA_KV_HEADS)], [gather(v_refs, kv) for kv in range(DSA_KV_HEADS)],
          sel_ref[0])

    @pl.when(jj == pl.num_programs(1) - 1)
    def _():
        block([_bf(kn_ref[0, kv]) for kv in range(DSA_KV_HEADS)], [_bf(vn_ref[0, kv]) for kv in range(DSA_KV_HEADS)],
              seln_ref[0])
        o_ref[0] = acc_scr[...] / l_scr[...]


def _dsa_sample_attn(page_table, q8, sel, sel_new, k_new_b, v_new_b, cache_kt, cache_vt, n_pg):
    Bd, n_pages = page_table.shape

    def page_spec(p):
        return pl.BlockSpec((1, DSA_KV_HEADS, DSA_DH, PAGE_SIZE), lambda b, jj, pt: (pt[b, jj * n_pg + p], 0, 0, 0))

    new_tok = pl.BlockSpec((1, DSA_KV_HEADS, DSA_DH, PAGE_SIZE), lambda b, jj, pt: (b, 0, 0, 0))
    grid_spec = pltpu.PrefetchScalarGridSpec(
        num_scalar_prefetch=1,
        grid=(Bd, n_pages // n_pg),
        in_specs=[pl.BlockSpec((1, DSA_KV_HEADS, DSA_HEADS, DSA_DH), lambda b, jj, pt: (b, 0, 0, 0)),
                  pl.BlockSpec((1, 1, n_pg * PAGE_SIZE), lambda b, jj, pt: (b, 0, jj)),
                  pl.BlockSpec((1, 1, PAGE_SIZE), lambda b, jj, pt: (b, 0, 0)), new_tok, new_tok]
                 + [page_spec(p) for p in range(n_pg)] * 2,
        out_specs=pl.BlockSpec((1, DSA_HEADS, DSA_DH), lambda b, jj, pt: (b, 0, 0)),
        scratch_shapes=[pltpu.VMEM((DSA_HEADS, 1), F32), pltpu.VMEM((DSA_HEADS, 1), F32),
                        pltpu.VMEM((DSA_HEADS, DSA_DH), F32)],
    )
    return pl.pallas_call(
        functools.partial(_dsa_sample_attn_kernel, n_pg=n_pg),
        grid_spec=grid_spec,
        out_shape=jax.ShapeDtypeStruct((Bd, DSA_HEADS, DSA_DH), F32),
        compiler_params=_cparams("parallel", "arbitrary"),
        name="dsa_sample_attn",
    )(page_table, q8, sel.reshape(Bd, 1, -1), sel_new.reshape(Bd, 1, PAGE_SIZE), k_new_b, v_new_b,
      *([cache_kt] * n_pg), *([cache_vt] * n_pg))


def _attend_rows(q, kk, vv, scale):
    s = jnp.sum(kk * q[None], axis=-1, keepdims=True) * scale
    m = jnp.max(s, axis=0, keepdims=True)
    p = jnp.exp(s - m)
    l = jnp.sum(p, axis=0)
    return jnp.sum(p * vv, axis=0) / l


def _mem_prompt_kernel(q_ref, mk_ref, mv_ref, o_ref):
    q = q_ref[0]
    mk = mk_ref[0]
    mv = mv_ref[0]
    outs = []
    for h in range(MEM_HEADS):
        sl = slice(h * MEM_DH, (h + 1) * MEM_DH)
        s = _dot_nt(q[:, sl], mk[:, sl]) * (MEM_DH ** -0.5)
        m = jnp.max(s, axis=1, keepdims=True)
        p = jnp.exp(s - m)
        p = p / jnp.sum(p, axis=1, keepdims=True)
        outs.append(_dot(p, mv[:, sl]))
    o_ref[0] = jnp.concatenate(outs, axis=1)


def _mem_prompt(q, mk, mv, tq):
    B, T, W = q.shape
    M = mk.shape[1]
    kv = pl.BlockSpec((1, M, W), lambda b, i: (b, 0, 0))
    return pl.pallas_call(
        _mem_prompt_kernel,
        grid=(B, T // tq),
        in_specs=[pl.BlockSpec((1, tq, W), lambda b, i: (b, i, 0)), kv, kv],
        out_specs=pl.BlockSpec((1, tq, W), lambda b, i: (b, i, 0)),
        out_shape=jax.ShapeDtypeStruct((B, T, W), F32),
        compiler_params=_cparams("parallel", "parallel"),
        name="mem_prompt",
    )(q, mk, mv)


def _mem_sample_kernel(q_ref, k_ref, v_ref, o_ref, *, ns):
    for i in range(ns):
        o_ref[i] = _attend_rows(q_ref[i], k_ref[i], v_ref[i], MEM_DH ** -0.5)


def _mem_sample(q, ck, cv, ns):
    Bd, M = ck.shape[:2]
    kv = pl.BlockSpec((ns, M, MEM_HEADS, MEM_DH), lambda b: (b, 0, 0, 0))
    qs = pl.BlockSpec((ns, MEM_HEADS, MEM_DH), lambda b: (b, 0, 0))
    return pl.pallas_call(
        functools.partial(_mem_sample_kernel, ns=ns),
        grid=(Bd // ns,),
        in_specs=[qs, kv, kv],
        out_specs=qs,
        out_shape=jax.ShapeDtypeStruct((Bd, MEM_HEADS, MEM_DH), F32),
        compiler_params=_cparams("parallel"),
        name="mem_sample",
    )(q, ck, cv)


def _merge_kernel(x_ref, og_ref, od_ref, om_ref, gn_ref, wg_ref, wb_ref, wo_ref, fn_ref, xo_ref, h2_ref):
    x = x_ref[...]
    d = x.shape[1]
    hb = _rms(x, gn_ref[...]).astype(BF16)
    acc = jnp.zeros_like(x)
    for n, o_ref in enumerate((og_ref, od_ref, om_ref)):
        gate = _sigmoid(jnp.dot(hb, wg_ref[:, n * d:(n + 1) * d], preferred_element_type=F32))
        acc = acc + gate * jnp.dot(_bf(o_ref[...]), wb_ref[n], preferred_element_type=F32)
    xo = x + jnp.dot(_bf(acc), wo_ref[...], preferred_element_type=F32)
    xo_ref[...] = xo
    h2_ref[...] = _rms(xo, fn_ref[...])


def _merge(x, o_g, o_d, o_m, gn, wg, wb, wo, fn, tm):
    n, d = x.shape
    tok = lambda w: pl.BlockSpec((tm, w), lambda i: (i, 0))
    full = lambda shape: pl.BlockSpec(shape, lambda i: (0,) * len(shape))
    return pl.pallas_call(
        _merge_kernel,
        grid=(n // tm,),
        in_specs=[tok(d), tok(BRANCH_W), tok(BRANCH_W), tok(BRANCH_W), full((1, d)), full((d, N_BRANCH * d)),
                  full((N_BRANCH, BRANCH_W, d)), full((d, d)), full((1, d))],
        out_specs=[tok(d), tok(d)],
        out_shape=[jax.ShapeDtypeStruct((n, d), F32)] * 2,
        compiler_params=_cparams("parallel"),
        name="merge",
    )(x, o_g, o_d, o_m, gn.reshape(1, d), wg, wb, wo, fn.reshape(1, d))


def _route_kernel(h_ref, wr_ref, br_ref, e_ref, p_ref, r_ref, cnt_ref, run_scr, *, tr):
    i = pl.program_id(0)

    @pl.when(i == 0)
    def _():
        run_scr[...] = jnp.zeros_like(run_scr)

    logits = _dot_nt(wr_ref[...], h_ref[...]) + br_ref[...]
    eidx = _iota((N_EXPERTS, LANES), 0)
    su = jnp.where(_iota((LANES, LANES), 0) < _iota((LANES, LANES), 1), 1.0, 0.0).astype(BF16)
    run = run_scr[...]
    for c in range(tr // LANES):
        sl = slice(c * LANES, (c + 1) * LANES)
        l = logits[:, sl]
        vals, idxs = [], []
        for _ in range(TOP_K):
            m = jnp.max(l, axis=0, keepdims=True)
            idx = jnp.min(jnp.where(l == m, eidx, N_EXPERTS), axis=0, keepdims=True)
            vals.append(m)
            idxs.append(idx)
            l = jnp.where(eidx == idx, -jnp.inf, l)
        ex = [jnp.exp(v - vals[0]) for v in vals]
        den = ex[0] + ex[1] + ex[2] + ex[3]
        oh = jnp.zeros((N_EXPERTS, LANES), F32)
        for k in range(TOP_K):
            p_ref[k:k + 1, sl] = ex[k] / den
            e_ref[k:k + 1, sl] = idxs[k]
            oh = oh + jnp.where(eidx == idxs[k], 1.0, 0.0)
        before = jnp.dot(oh.astype(BF16), su, preferred_element_type=F32) + run
        for k in range(TOP_K):
            rk = jnp.sum(jnp.where(eidx == idxs[k], before, 0.0), axis=0, keepdims=True)
            r_ref[k:k + 1, sl] = rk.astype(I32)
        run = run + jnp.sum(oh, axis=1, keepdims=True)
    run_scr[...] = run
    cnt_ref[...] = jnp.broadcast_to(run, cnt_ref.shape)


def _route(h2, wr_t, br, tr):
    n, d = h2.shape
    tokrow = pl.BlockSpec((TOP_K, tr), lambda i: (0, i))
    return pl.pallas_call(
        functools.partial(_route_kernel, tr=tr),
        grid=(n // tr,),
        in_specs=[pl.BlockSpec((tr, d), lambda i: (i, 0)),
                  pl.BlockSpec((N_EXPERTS, d), lambda i: (0, 0)),
                  pl.BlockSpec((N_EXPERTS, 1), lambda i: (0, 0))],
        out_specs=[tokrow, tokrow, tokrow, pl.BlockSpec((N_EXPERTS, LANES), lambda i: (0, 0))],
        out_shape=[jax.ShapeDtypeStruct((TOP_K, n), I32), jax.ShapeDtypeStruct((TOP_K, n), F32),
                   jax.ShapeDtypeStruct((TOP_K, n), I32), jax.ShapeDtypeStruct((N_EXPERTS, LANES), F32)],
        scratch_shapes=[pltpu.VMEM((N_EXPERTS, 1), F32)],
        compiler_params=_cparams("arbitrary"),
        name="moe_route",
    )(h2, wr_t, br)


def _dispatch_kernel(plo_ref, pn_ref, nb_ref, dest_ref, h_ref, xg_hbm, zblk, sem, *, td):
    blk = zblk.shape[0]
    n_blocks = xg_hbm.shape[0] // blk

    def row_copy(t, dst):
        return pltpu.make_async_copy(h_ref.at[pl.ds(t, 1)], xg_hbm.at[pl.ds(dst, 1)], sem.at[0])

    def zero_block(i):
        return pltpu.make_async_copy(zblk, xg_hbm.at[pl.ds(pl.multiple_of(i * blk, blk), blk)], sem.at[2])

    def issue(t, carry):
        for k in range(TOP_K):
            row_copy(t, dest_ref[k, t]).start()
        return carry

    lax.fori_loop(0, td, issue, 0, unroll=8)

    @pl.when(pl.program_id(0) == 0)
    def _():
        zblk[...] = jnp.zeros_like(zblk)

        def tail_start(i, c):
            zero_block(i).start()
            return c

        def tail_wait(i, c):
            zero_block(0).wait()
            return c

        lax.fori_loop(nb_ref[0], n_blocks, tail_start, 0)
        lax.fori_loop(nb_ref[0], n_blocks, tail_wait, 0)

        big = [1 << i for i in reversed(range(3, blk.bit_length() - 1))]

        def per_expert(e, carry):
            lo = plo_ref[e]
            n = pn_ref[e]
            head = jnp.minimum(n, (-lo) & 7)
            body = n - head
            for wait in (False, True):
                def zero(off, size, on):
                    copy = pltpu.make_async_copy(zblk.at[pl.ds(0, size)], xg_hbm.at[pl.ds(off, size)], sem.at[1])

                    @pl.when(on)
                    def _():
                        copy.wait() if wait else copy.start()

                for r in range(7):
                    zero(lo + r, 1, r < head)
                off = lo + head
                for size in big:
                    zero(pl.multiple_of(off, 8), size, (body & size) != 0)
                    off = off + (body & size)
                for r in range(7):
                    zero(off + r, 1, r < (body & 7))
            return carry

        lax.fori_loop(0, N_EXPERTS, per_expert, 0)

    for k in range(TOP_K):
        pltpu.make_async_copy(h_ref, xg_hbm.at[pl.ds(0, td)], sem.at[0]).wait()


def _dispatch(h2, dest, pad_lo, pad_n, nb_used, n_slots, blk, td):
    n, d = h2.shape
    grid_spec = pltpu.PrefetchScalarGridSpec(
        num_scalar_prefetch=3,
        grid=(n // td,),
        in_specs=[pl.BlockSpec((TOP_K, td), lambda i, plo, pn, nb: (0, i), memory_space=pltpu.SMEM),
                  pl.BlockSpec((td, d), lambda i, plo, pn, nb: (i, 0))],
        out_specs=pl.BlockSpec(memory_space=pl.ANY),
        scratch_shapes=[pltpu.VMEM((blk, d), F32), pltpu.SemaphoreType.DMA((3,))],
    )
    return pl.pallas_call(
        functools.partial(_dispatch_kernel, td=td),
        grid_spec=grid_spec,
        out_shape=jax.ShapeDtypeStruct((n_slots, d), F32),
        compiler_params=_cparams("arbitrary"),
        name="moe_dispatch",
    )(pad_lo, pad_n, nb_used, dest, h2)


def _expert_kernel(be_ref, nb_ref, x_ref, wgu_ref, bgu_ref, wdn_ref, bdn_ref, o_ref):
    del be_ref
    f = wdn_ref.shape[1]

    used = pl.program_id(0) < nb_ref[0]

    @pl.when(jnp.logical_not(used))
    def _():
        o_ref[...] = jnp.zeros_like(o_ref)

    @pl.when(used)
    def _():
        gu = jnp.dot(_bf(x_ref[...]), _bf(wgu_ref[0]), preferred_element_type=F32) + bgu_ref[0]
        gate = jnp.minimum(gu[:, :f], SWIGLU_LIMIT)
        up = jnp.clip(gu[:, f:], -SWIGLU_LIMIT, SWIGLU_LIMIT)
        glu = gate * _sigmoid(SWIGLU_ALPHA * gate)
        o_ref[...] = jnp.dot(_bf((up + 1.0) * glu), _bf(wdn_ref[0]), preferred_element_type=F32) + bdn_ref[0]


def _experts(xg, block_e, nb_used, wgu, bgu, wdn, bdn, blk):
    n_slots, d = xg.shape
    f = wdn.shape[1]
    blk_of = lambda i, nb: jnp.minimum(i, nb[0] - 1)
    tok = pl.BlockSpec((blk, d), lambda i, be, nb: (blk_of(i, nb), 0))
    per_e = lambda shape: pl.BlockSpec((1,) + shape, lambda i, be, nb: (be[blk_of(i, nb)], 0, 0))
    grid_spec = pltpu.PrefetchScalarGridSpec(
        num_scalar_prefetch=2,
        grid=(n_slots // blk,),
        in_specs=[tok, per_e((d, 2 * f)), per_e((1, 2 * f)), per_e((f, d)), per_e((1, d))],
        out_specs=pl.BlockSpec((blk, d), lambda i, be, nb: (i, 0)),
    )
    return pl.pallas_call(
        _expert_kernel,
        grid_spec=grid_spec,
        out_shape=jax.ShapeDtypeStruct((n_slots, d), F32),
        compiler_params=_cparams("arbitrary"),
        name="moe_experts",
    )(block_e, nb_used, xg, wgu, bgu, wdn, bdn)


def _combine_kernel(dest_ref, x_ref, p_ref, g_ref, yb_hbm, o_ref, buf, sem, *, tc):
    def row_copy(src, k, t):
        return pltpu.make_async_copy(yb_hbm.at[pl.ds(src, 1)], buf.at[k, pl.ds(t, 1)], sem.at[0])

    def issue(t, carry):
        for k in range(TOP_K):
            row_copy(dest_ref[k, t], k, t).start()
        return carry

    lax.fori_loop(0, tc, issue, 0, unroll=8)
    for k in range(TOP_K):
        pltpu.make_async_copy(yb_hbm.at[pl.ds(0, tc)], buf.at[k], sem.at[0]).wait()
    p = p_ref[...]
    acc = p[:, 0:1] * buf[0]
    for k in range(1, TOP_K):
        acc = acc + p[:, k:k + 1] * buf[k]
    o_ref[...] = _rms(x_ref[...] + acc, g_ref[...])


def _combine(x, yb, dest, p_t, g, tc):
    n, d = x.shape
    return pl.pallas_call(
        functools.partial(_combine_kernel, tc=tc),
        grid=(n // tc,),
        in_specs=[pl.BlockSpec((TOP_K, tc), lambda i: (0, i), memory_space=pltpu.SMEM),
                  pl.BlockSpec((tc, d), lambda i: (i, 0)),
                  pl.BlockSpec((tc, TOP_K), lambda i: (i, 0)),
                  pl.BlockSpec((1, d), lambda i: (0, 0)),
                  pl.BlockSpec(memory_space=pl.ANY)],
        out_specs=pl.BlockSpec((tc, d), lambda i: (i, 0)),
        out_shape=jax.ShapeDtypeStruct((n, d), F32),
        scratch_shapes=[pltpu.VMEM((TOP_K, tc, d), F32), pltpu.SemaphoreType.DMA((1,))],
        compiler_params=_cparams("arbitrary"),
        name="moe_combine",
    )(dest, x, p_t, g.reshape(1, d), yb)


def _moe_final(x, h2, g_final, wr_t, br, wgu, bgu, wdn, bdn, blk, tile):
    n, d = x.shape
    e, p, rank, cnt = _route(h2, wr_t, br, tile)
    counts = cnt[:, 0].astype(I32)
    padded = (counts + blk - 1) // blk * blk
    pad_end = jnp.cumsum(padded)
    pad_start = pad_end - padded
    n_blocks = -(-n * TOP_K // blk) + N_EXPERTS
    expert_ids = jnp.arange(N_EXPERTS, dtype=I32)[:, None, None]
    dest = rank + jnp.sum(jnp.where(e[None] == expert_ids, pad_start[:, None, None], 0), axis=0)
    block_lo = jnp.arange(n_blocks, dtype=I32)[:, None] * blk
    block_e = jnp.minimum(jnp.sum((pad_end[None, :] <= block_lo).astype(I32), axis=1), N_EXPERTS - 1)
    nb_used = (pad_end[-1:] // blk).astype(I32)
    xg = _dispatch(h2, dest, pad_start + counts, padded - counts, nb_used, n_blocks * blk, blk, tile)
    yb = _experts(xg, block_e, nb_used, wgu, bgu, wdn, bdn, blk)
    return _combine(x, yb, dest, p.T, g_final, min(tile, 256))


def _prep_w_in(w):
    cuts = np.cumsum((0,) + _SPLITS)
    seg = lambda i: w[:, int(cuts[i]):int(cuts[i + 1])]
    small = jnp.concatenate([seg(8), seg(9), seg(2), seg(3), jnp.zeros((w.shape[0], LANES - _SM_END), w.dtype)], axis=1)
    w1 = jnp.concatenate([seg(0), seg(1), seg(4), seg(5), seg(6), seg(7), seg(10), small], axis=1)
    return w1.astype(BF16), w[:, int(cuts[-1]):].astype(BF16)


def kernel(x_prompt, x_sample, cache_k, cache_v, cache_k_idx, cache_mem_k, cache_mem_v, state_gdn, state_conv,
           page_table, mem_prompt, norm_attn, w_in, conv_w, gdn_a_log, gdn_dt_bias, gdn_norm, norm_mem, w_mem_kv,
           w_branch, w_out, norm_ffn, w_router, b_router, w_gate_up, b_gate_up, w_down, b_down, norm_final):
    B, T, D = x_prompt.shape
    Bd, Ts, _ = x_sample.shape
    assert Ts == 1 and w_in.shape[0] == 1, "one layer, one new token per sample"
    H = GDN_HEADS
    G = DSA_HEADS // DSA_KV_HEADS
    n_pages = page_table.shape[1]
    M = mem_prompt.shape[1]

    w1, wg = _prep_w_in(w_in[0])
    wb = w_branch[0].astype(BF16)
    wo = w_out[0].astype(BF16)
    wr_t = w_router[0].T
    br = b_router[0].reshape(N_EXPERTS, 1)
    wgu = w_gate_up[0]
    wdn = w_down[0]
    bgu = b_gate_up[0].reshape(N_EXPERTS, 1, -1)
    bdn = b_down[0].reshape(N_EXPERTS, 1, -1)
    alog = gdn_a_log[0]
    dtb = gdn_dt_bias[0]

    xp = x_prompt.reshape(B * T, D)
    g_qkv, g_z, d_k, d_v, i_q, m_q, small, qh, kt, vh = _proj_prompt(xp, norm_attn[0], w1, B, T, min(512, T))

    quad_rows = lambda a: (a.reshape(B, T // GDN_CHUNK, GDN_CHUNK, 2, 4).transpose(0, 1, 3, 4, 2)
                           .reshape(B, T // GDN_CHUNK, 2, 1, QW))
    per_group = lambda a: jnp.repeat(a, GDN_DK).reshape(2, 1, QW)
    o_g, ssm_q = _gdn_prompt(g_qkv.reshape(B, T, GDN_QKV), g_z.reshape(B, T, H * GDN_DV),
                             quad_rows(small[:, _SM_GA:_SM_GA + H]), quad_rows(small[:, _SM_GB:_SM_GB + H]),
                             conv_w[0], per_group(alog), per_group(dtb), jnp.tile(gdn_norm[0], 4).reshape(1, QW),
                             min(512, T))
    o_g = o_g.reshape(B * T, H * GDN_DV)
    ssm_p = ssm_q.reshape(B, 2, GDN_DK, 4, GDN_DV).transpose(0, 1, 3, 2, 4).reshape(B, H, GDN_DK, GDN_DV)

    o_d = _dsa_prompt(i_q.reshape(B, T, -1), small.reshape(B, T, LANES), qh, kt, vh, min(TOPK_MAX, T // 4),
                      min(256, T))
    o_d = o_d.reshape(B * T, DSA_HEADS * DSA_DH)

    mk, mv = _norm_matmul(mem_prompt.reshape(B * M, D), norm_mem[0], w_mem_kv[0].astype(BF16),
                          (MEM_HEADS * MEM_DH,) * 2, min(512, B * M))
    o_m = _mem_prompt(m_q.reshape(B, T, -1), mk.reshape(B, M, -1), mv.reshape(B, M, -1), min(512, T))
    o_m = o_m.reshape(B * T, MEM_HEADS * MEM_DH)

    xres, h2 = _merge(xp, o_g, o_d, o_m, norm_attn[0], wg, wb, wo, norm_ffn[0], min(256, B * T))
    y_prompt = _moe_final(xres, h2, norm_final, wr_t, br, wgu, bgu, wdn, bdn, 256, min(512, B * T))

    k_prompt = d_k.reshape(1, B, T, DSA_KV_HEADS, DSA_DH)
    v_prompt = d_v.reshape(1, B, T, DSA_KV_HEADS, DSA_DH)
    kidx_prompt = small[:, _SM_IK:_SM_IK + IDX_DIM].reshape(1, B, T, IDX_DIM)
    memk_prompt = mk.reshape(1, B, M, MEM_HEADS, MEM_DH)
    memv_prompt = mv.reshape(1, B, M, MEM_HEADS, MEM_DH)
    conv_prompt = g_qkv.reshape(B, T, GDN_QKV)[:, T - (GDN_CONV - 1):, :][None]

    xs = x_sample.reshape(Bd, D)
    s_qkv, s_z, sd_q, sd_k, sd_v, si_q, sm_q, ssmall = _norm_matmul(xs, norm_attn[0], w1, _PROJ_SPLITS, Bd)

    lanes_b = lambda a: jnp.broadcast_to(a[..., None], a.shape + (Bd,))
    og_t, s_t = _gdn_sample(
        s_qkv.T, state_conv[0].transpose(1, 2, 0), lanes_b(conv_w[0]), s_z.T,
        ssmall[:, _SM_GA:_SM_GA + H].T.reshape(H, 1, Bd), ssmall[:, _SM_GB:_SM_GB + H].T.reshape(H, 1, Bd),
        lanes_b(alog.reshape(H, 1)), lanes_b(dtb.reshape(H, 1)), lanes_b(gdn_norm[0]),
        state_gdn[0].transpose(1, 2, 3, 0))
    so_g = og_t.T
    ssm_sample = s_t.transpose(3, 0, 1, 2)[None]
    conv_sample = jnp.concatenate([state_conv[0][:, 1:], s_qkv[:, None, :]], axis=1)[None]

    qi8 = jnp.pad(si_q.reshape(Bd, IDX_HEADS, IDX_DIM), ((0, 0), (0, 8 - IDX_HEADS), (0, 0)))
    wi8 = jnp.pad(ssmall[:, _SM_IW:_SM_IW + IDX_HEADS], ((0, 0), (0, 8 - IDX_HEADS)))[..., None]
    n_pg = 32 if n_pages % 32 == 0 else n_pages
    cache_kit = cache_k_idx[0].transpose(0, 2, 1)
    cache_kt = cache_k[0].transpose(0, 2, 3, 1)
    cache_vt = cache_v[0].transpose(0, 2, 3, 1)
    scores = _dsa_sample_scores(page_table, qi8, wi8, cache_kit, 64 if n_pages % 64 == 0 else n_pg)
    scores = scores.reshape(Bd, n_pages * PAGE_SIZE)
    n_sel = min(TOPK_MAX, (n_pages * PAGE_SIZE + 1) // 4)
    sel, sel_new = _dsa_sample_select(scores, si_q, ssmall, n_sel)
    over_lanes = lambda a: jnp.broadcast_to(a[..., None], a.shape + (PAGE_SIZE,))
    head_kv = jnp.arange(DSA_HEADS, dtype=I32) // G
    q8 = jnp.where((head_kv[None, :] == jnp.arange(DSA_KV_HEADS, dtype=I32)[:, None])[None, :, :, None],
                   sd_q.reshape(Bd, 1, DSA_HEADS, DSA_DH), 0.0)
    k_new = sd_k.reshape(Bd, DSA_KV_HEADS, DSA_DH)
    v_new = sd_v.reshape(Bd, DSA_KV_HEADS, DSA_DH)
    so_d = _dsa_sample_attn(page_table, q8, sel, sel_new, over_lanes(k_new), over_lanes(v_new),
                            cache_kt, cache_vt, n_pg).reshape(Bd, DSA_HEADS * DSA_DH)

    so_m = _mem_sample(sm_q.reshape(Bd, MEM_HEADS, MEM_DH), cache_mem_k[0], cache_mem_v[0], 4)
    so_m = so_m.reshape(Bd, MEM_HEADS * MEM_DH)

    sres, sh2 = _merge(xs, so_g, so_d, so_m, norm_attn[0], wg, wb, wo, norm_ffn[0], Bd)
    y_sample = _moe_final(sres, sh2, norm_final, wr_t, br, wgu, bgu, wdn, bdn, 256, Bd)

    return (y_prompt.reshape(B, T, D), y_sample.reshape(Bd, 1, D), k_prompt, v_prompt, kidx_prompt,
            memk_prompt, memv_prompt, ssm_p[None], conv_prompt,
            k_new.reshape(1, Bd, 1, DSA_KV_HEADS, DSA_DH), v_new.reshape(1, Bd, 1, DSA_KV_HEADS, DSA_DH),
            ssmall[:, _SM_IK:_SM_IK + IDX_DIM].reshape(1, Bd, 1, IDX_DIM), ssm_sample, conv_sample)
```

```python
import functools

import numpy as np
import jax
import jax.numpy as jnp
from jax import lax
from jax.experimental import pallas as pl
from jax.experimental.pallas import tpu as pltpu

F32 = jnp.float32
BF16 = jnp.bfloat16
I32 = jnp.int32

EPS = 1e-6
GDN_HEADS = 8
GDN_DK = 64
GDN_DV = 64
GDN_CONV = 4
GDN_CHUNK = 64
GDN_QKV = 2 * GDN_HEADS * GDN_DK + GDN_HEADS * GDN_DV
DSA_HEADS = 8
DSA_KV_HEADS = 4
DSA_DH = 64
IDX_HEADS = 4
IDX_DIM = 64
TOPK_MAX = 256
Q_BLOCK = 128
MEM_HEADS = 4
MEM_DH = 128
N_BRANCH = 3
BRANCH_W = 512
N_EXPERTS = 32
TOP_K = 4
SWIGLU_LIMIT = 7.0
SWIGLU_ALPHA = 1.702
PAGE_SIZE = 128
LANES = 128

_SPLITS = (GDN_QKV, GDN_HEADS * GDN_DV, GDN_HEADS, GDN_HEADS,
           DSA_HEADS * DSA_DH, DSA_KV_HEADS * DSA_DH, DSA_KV_HEADS * DSA_DH,
           IDX_HEADS * IDX_DIM, IDX_DIM, IDX_HEADS,
           MEM_HEADS * MEM_DH)
_PROJ_SPLITS = (GDN_QKV, 512, 512, 256, 256, 256, 512, LANES)
_SM_IK = 0
_SM_IW = IDX_DIM
_SM_GB = _SM_IW + IDX_HEADS
_SM_GA = _SM_GB + GDN_HEADS
_SM_END = _SM_GA + GDN_HEADS

VMEM_LIMIT_BYTES = 56 * 1024 * 1024


def _cparams(*sem):
    return pltpu.CompilerParams(dimension_semantics=sem, vmem_limit_bytes=VMEM_LIMIT_BYTES)


def _bf(x):
    return x.astype(BF16)


def _dot(a, b):
    return jnp.dot(_bf(a), _bf(b), preferred_element_type=F32)


_NT = (((1,), (1,)), ((), ()))


def _dot_nt(a, b):
    return lax.dot_general(_bf(a), _bf(b), _NT, preferred_element_type=F32)


def _split2(x):
    hi = x.astype(BF16)
    lo = (x - hi.astype(F32)).astype(BF16)
    return hi, lo


def _split3(x):
    hi = x.astype(BF16)
    r = x - hi.astype(F32)
    mid = r.astype(BF16)
    lo = (r - mid.astype(F32)).astype(BF16)
    return hi, mid, lo


def _dot_nt3(a, b):
    ah, al = _split2(a)
    bh, bl = _split2(b)
    d = lambda x, y: lax.dot_general(x, y, _NT, preferred_element_type=F32)
    return d(ah, bh) + (d(ah, bl) + d(al, bh))


def _dot3(a, b):
    ah, al = _split2(a)
    bh, bl = _split2(b)
    d = lambda x, y: jnp.dot(x, y, preferred_element_type=F32)
    return d(ah, bh) + (d(ah, bl) + d(al, bh))


def _dot_exact01(a, b01):
    hi, mid, lo = _split3(a)
    d = lambda x: jnp.dot(x, b01, preferred_element_type=F32)
    return d(hi) + (d(mid) + d(lo))


def _rms(x, g):
    return x * lax.rsqrt(jnp.mean(x * x, axis=-1, keepdims=True) + EPS) * g


def _sigmoid(x):
    return 1.0 / (1.0 + jnp.exp(-x))


def _silu(x):
    return x * _sigmoid(x)


def _softplus(x):
    return jnp.maximum(x, 0.0) + jnp.log(1.0 + jnp.exp(-jnp.abs(x)))


def _iota(shape, axis):
    return lax.broadcasted_iota(I32, shape, axis)


def _norm_matmul_kernel(x_ref, g_ref, w_ref, *o_refs, splits):
    hb = _rms(x_ref[...], g_ref[...]).astype(BF16)
    off = 0
    for o_ref, n in zip(o_refs, splits):
        o_ref[...] = jnp.dot(hb, w_ref[:, off:off + n], preferred_element_type=F32)
        off += n


def _norm_matmul(x, g, w, splits, tm):
    n, d = x.shape
    return pl.pallas_call(
        functools.partial(_norm_matmul_kernel, splits=splits),
        grid=(n // tm,),
        in_specs=[pl.BlockSpec((tm, d), lambda i: (i, 0)),
                  pl.BlockSpec((1, d), lambda i: (0, 0)),
                  pl.BlockSpec((d, sum(splits)), lambda i: (0, 0))],
        out_specs=[pl.BlockSpec((tm, s), lambda i: (i, 0)) for s in splits],
        out_shape=[jax.ShapeDtypeStruct((n, s), F32) for s in splits],
        compiler_params=_cparams("parallel"),
        name="norm_matmul",
    )(x, g.reshape(1, d), w)


def _proj_prompt_kernel(x_ref, g_ref, w_ref, qkv_ref, z_ref, dk_ref, dv_ref, iq_ref, mq_ref, sm_ref,
                        qh_ref, kt_ref, vh_ref):
    hb = _rms(x_ref[...], g_ref[...]).astype(BF16)
    offs = np.cumsum((0,) + _PROJ_SPLITS)
    part = lambda i: jnp.dot(hb, w_ref[:, int(offs[i]):int(offs[i + 1])], preferred_element_type=F32)
    qkv_ref[...] = part(0)
    z_ref[...] = part(1)
    dq = part(2)
    for h in range(DSA_HEADS):
        qh_ref[0, h] = dq[:, h * DSA_DH:(h + 1) * DSA_DH]
    dk = part(3)
    dk_ref[...] = dk
    kt_ref[0] = dk.T.reshape(DSA_KV_HEADS, DSA_DH, dk.shape[0])
    dv = part(4)
    dv_ref[...] = dv
    for kv in range(DSA_KV_HEADS):
        vh_ref[0, kv] = dv[:, kv * DSA_DH:(kv + 1) * DSA_DH]
    iq_ref[...] = part(5)
    mq_ref[...] = part(6)
    sm_ref[...] = part(7)


def _proj_prompt(x, g, w, B, T, tm):
    n, d = x.shape
    tpb = T // tm
    tok = lambda width: pl.BlockSpec((tm, width), lambda b, t: (b * tpb + t, 0))
    flat = lambda width: jax.ShapeDtypeStruct((n, width), F32)
    sp = _PROJ_SPLITS
    return pl.pallas_call(
        _proj_prompt_kernel,
        grid=(B, tpb),
        in_specs=[tok(d), pl.BlockSpec((1, d), lambda b, t: (0, 0)), pl.BlockSpec((d, sum(sp)), lambda b, t: (0, 0))],
        out_specs=[tok(sp[0]), tok(sp[1]), tok(sp[3]), tok(sp[4]), tok(sp[5]), tok(sp[6]), tok(sp[7]),
                   pl.BlockSpec((1, DSA_HEADS, tm, DSA_DH), lambda b, t: (b, 0, t, 0)),
                   pl.BlockSpec((1, DSA_KV_HEADS, DSA_DH, tm), lambda b, t: (b, 0, 0, t)),
                   pl.BlockSpec((1, DSA_KV_HEADS, tm, DSA_DH), lambda b, t: (b, 0, t, 0))],
        out_shape=[flat(sp[0]), flat(sp[1]), flat(sp[3]), flat(sp[4]), flat(sp[5]), flat(sp[6]), flat(sp[7]),
                   jax.ShapeDtypeStruct((B, DSA_HEADS, T, DSA_DH), F32),
                   jax.ShapeDtypeStruct((B, DSA_KV_HEADS, DSA_DH, T), F32),
                   jax.ShapeDtypeStruct((B, DSA_KV_HEADS, T, DSA_DH), F32)],
        compiler_params=_cparams("parallel", "parallel"),
        name="proj_prompt",
    )(x, g.reshape(1, d), w)


QW = 4 * GDN_DK


def _bd_stack(x):
    lane_head = (_iota(x.shape, 1) >> 6) & 3
    return jnp.concatenate([jnp.where(lane_head == h, x, 0.0) for h in range(4)], axis=0)


def _bd_dot(a, x):
    return jnp.dot(_bf(a), _bf(_bd_stack(x)), preferred_element_type=F32)


def _bd_split(x):
    xh = x.astype(BF16).astype(F32)
    return _bf(_bd_stack(xh)), _bf(_bd_stack(x - xh))


def _bd_dot3(a, xb):
    ah, al = _split2(a)
    bh, bl = xb
    d = lambda p, q: jnp.dot(p, q, preferred_element_type=F32)
    return d(ah, bh) + (d(ah, bl) + d(al, bh))


def _gdn_prompt_kernel(q_ref, k_ref, v_ref, qh_ref, kh_ref, vh_ref, z_ref, a_ref, b_ref,
                       cwq_ref, cwk_ref, cwv_ref, alog_ref, dtb_ref, nw_ref, o_ref, s_ref,
                       xq, xk, xv, s_scr, *, tt):
    ti = pl.program_id(1)
    C = GDN_CHUNK
    HALO = 8
    NG = GDN_HEADS // 4
    CPI = 4

    @pl.when(ti == 0)
    def _():
        s_scr[...] = jnp.zeros_like(s_scr)

    r = _iota((QW, QW), 0)
    c = _iota((QW, QW), 1)
    same = (r >> 6) == (c >> 6)
    ones_bd = jnp.where(same, 1.0, 0.0).astype(BF16)
    su_bd = jnp.where(same, jnp.where((r & 63) > (c & 63), 1.0, 0.0), 0.0).astype(BF16)
    rhs01 = jnp.concatenate([su_bd, ones_bd], axis=1)
    ident = jnp.where(r == c, 1.0, 0.0).astype(BF16)

    keep = (ti > 0).astype(F32)
    for idx, (src, halo, dst, cw) in enumerate(((q_ref, qh_ref, xq, cwq_ref), (k_ref, kh_ref, xk, cwk_ref),
                                                (v_ref, vh_ref, xv, cwv_ref))):
        dst[0:HALO, :] = halo[0] * keep
        dst[HALO:HALO + tt, :] = src[0]
        w = cw[...]
        y = w[0:1, :] * dst[pl.ds(HALO - 3, tt), :]
        for i in range(1, GDN_CONV):
            y = y + w[i:i + 1, :] * dst[pl.ds(HALO - 3 + i, tt), :]
        y = _silu(y)
        if idx < 2:
            ss = jnp.concatenate([_dot_exact01((y * y)[:, g * QW:(g + 1) * QW], ones_bd) for g in range(NG)], axis=1)
            y = y * lax.rsqrt(ss + EPS)
            if idx == 0:
                y = y * (GDN_DK ** -0.5)
        dst[HALO:HALO + tt, :] = y

    ri = _iota((C, QW), 0)
    li = _iota((C, QW), 1) & 63
    lane_head = _iota((C, QW), 1) >> 6
    incl = li <= ri
    strict = li < ri
    eye = li == ri
    nw = nw_ref[...]

    def solve_stage(cis):
        streams = [(ci, g) for ci in cis for g in range(NG)]
        n = range(len(streams))
        rows = [pl.multiple_of(HALO + ci * C, 8) for ci, _ in streams]
        cols = [slice(g * QW, (g + 1) * QW) for _, g in streams]
        qc = [xq[pl.ds(rows[i], C), cols[i]] for i in n]
        kc = [xk[pl.ds(rows[i], C), cols[i]] for i in n]
        vc = [xv[pl.ds(rows[i], C), cols[i]] for i in n]
        g_row = [-jnp.exp(alog_ref[g]) * _softplus(a_ref[0, ci, g] + dtb_ref[g]) for ci, g in streams]
        b_row = [_sigmoid(b_ref[0, ci, g]) for ci, g in streams]
        res = [_dot_exact01(jnp.concatenate([jnp.where(incl, g_row[i], 0.0), jnp.where(eye, b_row[i], 0.0)], axis=0),
                            rhs01) for i in n]
        gcol = [res[i][:C, QW:] for i in n]
        bcol = [res[i][C:, QW:] for i in n]
        gamma = [jnp.where(incl, jnp.exp(res[i][:C, :QW]), 0.0) for i in n]
        eg = [jnp.exp(gcol[i]) for i in n]
        glast = [gcol[i][C - 1:C, :] for i in n]
        kb = [kc[i] * bcol[i] for i in n]
        kt = [lax.dot_general(ident, jnp.concatenate([_bf(kc[i])] * 4, axis=0), _NT, preferred_element_type=F32)
              for i in n]
        kt = [_bf(jnp.where(same, kt[i], 0.0)) for i in n]
        kk = [jnp.dot(_bf(jnp.concatenate([kb[i], qc[i]], axis=0)), kt[i], preferred_element_type=F32)
              for i in n]
        r = [jnp.where(strict, -(kk[i][:C] * gamma[i]), 0.0) for i in n]
        t = [jnp.where(eye, 1.0, r[i]) for i in n]
        for k in range(6):
            rb = [_bd_split(r[i]) for i in n]
            if k > 0:
                t = [t[i] + _bd_dot3(t[i], rb[i]) for i in n]
            if k < 5:
                r = [_bd_dot3(r[i], rb[i]) for i in n]
        rhs = [_bd_split(jnp.concatenate([vc[i] * bcol[i], kb[i] * eg[i]], axis=1)) for i in n]
        y = [_bd_dot3(t[i], rhs[i]) for i in n]
        return [dict(ci=streams[i][0], g=streams[i][1], y=y[i], qg=qc[i] * eg[i], qk=kk[i][C:] * gamma[i],
                     kd=kc[i] * jnp.exp(glast[i] - gcol[i]), dec=jnp.exp(glast[i])) for i in n]

    def scan_stage(st):
        n = range(len(st))
        s = [s_scr[t["g"]] for t in st]
        ws = [_bd_dot(jnp.concatenate([st[i]["y"][:, QW:], st[i]["qg"]], axis=0), s[i]) for i in n]
        vn = [st[i]["y"][:, :QW] - ws[i][:C] for i in n]
        o = [ws[i][C:] + _bd_dot(st[i]["qk"], vn[i]) for i in n]
        kdt = [lax.dot_general(ident, _bf(st[i]["kd"]), _NT, preferred_element_type=F32) for i in n]
        full = [jnp.dot(_bf(kdt[i]), _bf(vn[i]), preferred_element_type=F32) for i in n]
        ms = [_dot_exact01(o[i] * o[i], ones_bd) * (1.0 / GDN_DV) for i in n]
        for i in n:
            g = st[i]["g"]
            sadd = jnp.where(lane_head == 0, full[i][0:C], 0.0)
            for h in range(1, 4):
                sadd = sadd + jnp.where(lane_head == h, full[i][h * C:(h + 1) * C], 0.0)
            s_scr[g] = s[i] * st[i]["dec"] + sadd
            o0 = pl.multiple_of(st[i]["ci"] * C, 8)
            cols = slice(g * QW, (g + 1) * QW)
            zc = z_ref[0, pl.ds(o0, C), cols]
            o_ref[0, pl.ds(o0, C), cols] = o[i] * lax.rsqrt(ms[i] + EPS) * nw * _silu(zc)

    def chunk_group(gi, carry):
        st = solve_stage([CPI * gi + j for j in range(CPI)])
        for j in range(CPI):
            scan_stage(st[j * NG:(j + 1) * NG])
        return carry

    lax.fori_loop(0, tt // (CPI * C), chunk_group, 0)
    s_ref[0] = s_scr[...]


def _gdn_prompt(qkv, z, a, b, cw, alog, dtb, nw, tt):
    B, T, _ = qkv.shape
    nct = tt // GDN_CHUNK
    NG = GDN_HEADS // 4
    W = NG * QW
    col = lambda j: pl.BlockSpec((1, tt, W), lambda bi, ti: (bi, ti, j))
    halo = lambda j: pl.BlockSpec((1, 8, W), lambda bi, ti: (bi, jnp.maximum(ti * (tt // 8) - 1, 0), j))
    chunked = pl.BlockSpec((1, nct, NG, 1, QW), lambda bi, ti: (bi, ti, 0, 0, 0))
    cwcol = lambda j: pl.BlockSpec((GDN_CONV, W), lambda bi, ti: (0, j))
    per_group = pl.BlockSpec((NG, 1, QW), lambda bi, ti: (0, 0, 0))
    return pl.pallas_call(
        functools.partial(_gdn_prompt_kernel, tt=tt),
        grid=(B, T // tt),
        in_specs=[col(0), col(1), col(2), halo(0), halo(1), halo(2), col(0), chunked, chunked,
                  cwcol(0), cwcol(1), cwcol(2), per_group, per_group,
                  pl.BlockSpec((1, QW), lambda bi, ti: (0, 0))],
        out_specs=[col(0), pl.BlockSpec((1, NG, GDN_DK, QW), lambda bi, ti: (bi, 0, 0, 0))],
        out_shape=[jax.ShapeDtypeStruct((B, T, W), F32), jax.ShapeDtypeStruct((B, NG, GDN_DK, QW), F32)],
        scratch_shapes=[pltpu.VMEM((8 + tt, W), F32)] * 3 + [pltpu.VMEM((NG, GDN_DK, QW), F32)],
        compiler_params=_cparams("parallel", "arbitrary"),
        name="gdn_prompt",
    )(qkv, qkv, qkv, qkv, qkv, qkv, z, a, b, cw, cw, cw, alog, dtb, nw)


def _gdn_sample_kernel(q_ref, k_ref, v_ref, cq_ref, ck_ref, cv_ref, wq_ref, wk_ref, wv_ref,
                       z_ref, a_ref, b_ref, alog_ref, dtb_ref, nw_ref, s_ref, o_ref, so_ref):
    def conv(x_ref, c_ref, w_ref):
        y = w_ref[GDN_CONV - 1] * x_ref[...]
        for i in range(GDN_CONV - 1):
            y = y + w_ref[i] * c_ref[i]
        return _silu(y)

    q = conv(q_ref, cq_ref, wq_ref)
    q = q * lax.rsqrt(jnp.sum(q * q, axis=0, keepdims=True) + EPS) * (GDN_DK ** -0.5)
    k = conv(k_ref, ck_ref, wk_ref)
    k = k * lax.rsqrt(jnp.sum(k * k, axis=0, keepdims=True) + EPS)
    v = conv(v_ref, cv_ref, wv_ref)
    beta = _sigmoid(b_ref[0])
    g = -jnp.exp(alog_ref[0]) * _softplus(a_ref[0] + dtb_ref[0])
    dec = jnp.exp(g)
    ks = jnp.zeros_like(v)
    for i in range(GDN_DK):
        ks = ks + k[i:i + 1, :] * (s_ref[0, i] * dec)
    vn = beta * (v - ks)
    o = jnp.zeros_like(v)
    for i in range(GDN_DK):
        sn = s_ref[0, i] * dec + k[i:i + 1, :] * vn
        so_ref[0, i] = sn
        o = o + q[i:i + 1, :] * sn
    on = o * lax.rsqrt(jnp.mean(o * o, axis=0, keepdims=True) + EPS) * nw_ref[...]
    o_ref[...] = on * _silu(z_ref[...])


def _gdn_sample(qkv_t, conv_t, cw_t, z_t, a_t, b_t, alog_t, dtb_t, nw_t, s_t):
    nb = qkv_t.shape[1]
    H, DK, DV = GDN_HEADS, GDN_DK, GDN_DV
    row = lambda off: pl.BlockSpec((DK, nb), lambda h: (off + h, 0))
    crow = lambda off: pl.BlockSpec((GDN_CONV - 1, DK, nb), lambda h: (0, off + h, 0))
    wrow = lambda off: pl.BlockSpec((GDN_CONV, DK, nb), lambda h: (0, off + h, 0))
    per_head = pl.BlockSpec((1, 1, nb), lambda h: (h, 0, 0))
    return pl.pallas_call(
        _gdn_sample_kernel,
        grid=(H,),
        in_specs=[row(0), row(H), row(2 * H), crow(0), crow(H), crow(2 * H), wrow(0), wrow(H), wrow(2 * H),
                  row(0), per_head, per_head, per_head, per_head,
                  pl.BlockSpec((DV, nb), lambda h: (0, 0)),
                  pl.BlockSpec((1, DK, DV, nb), lambda h: (h, 0, 0, 0))],
        out_specs=[row(0), pl.BlockSpec((1, DK, DV, nb), lambda h: (h, 0, 0, 0))],
        out_shape=[jax.ShapeDtypeStruct((H * DV, nb), F32), jax.ShapeDtypeStruct((H, DK, DV, nb), F32)],
        compiler_params=_cparams("parallel"),
        name="gdn_sample",
    )(qkv_t, qkv_t, qkv_t, conv_t, conv_t, conv_t, cw_t, cw_t, cw_t, z_t, a_t, b_t, alog_t, dtb_t, nw_t, s_t)


def _count_ge(x, thr):
    return jnp.sum(jnp.where(x >= thr, 1.0, 0.0), axis=1, keepdims=True)


def _topk_mask(score, valid, k):
    rows, width = score.shape
    kf = float(k)
    x = jnp.where(valid, score, -jnp.inf)
    validf = jnp.where(valid, 1.0, 0.0)
    nvalid = jnp.sum(validf, axis=1, keepdims=True)
    few = nvalid <= kf
    mx = jnp.max(x, axis=1, keepdims=True)
    mn = jnp.min(jnp.where(valid, score, jnp.inf), axis=1, keepdims=True)
    lo0 = mn
    hi0 = mx + (jnp.abs(mx) * 1e-6 + 1e-30)
    chi0 = jnp.zeros_like(mx)
    clo0 = nvalid

    def step(_, carry):
        lo, hi, chi, clo = carry
        mid = 0.5 * lo + 0.5 * hi
        c = _count_ge(x, mid)
        ge = c >= kf
        return jnp.where(ge, mid, lo), jnp.where(ge, hi, mid), jnp.where(ge, chi, c), jnp.where(ge, c, clo)

    def finish(hi, chi):
        tau = jnp.max(jnp.where(x < hi, x, -jnp.inf), axis=1, keepdims=True)
        ceq = jnp.sum(jnp.where(x == tau, 1.0, 0.0), axis=1, keepdims=True)
        return tau, ceq

    def not_done(hi, chi, clo):
        tau, ceq = finish(hi, chi)
        bad = jnp.where(few | (clo == kf) | (chi + ceq >= kf), 0.0, 1.0)
        return jnp.max(bad) > 0.0

    def exact_cond(carry):
        it, _, _, _, clo = carry
        open_rows = jnp.max(jnp.where(few | (clo == kf), 0.0, 1.0)) > 0.0
        return jnp.logical_and(it < 6, open_rows)

    def rounds(nsteps):
        def body(carry):
            it, lo, hi, chi, clo = carry
            lo, hi, chi, clo = lax.fori_loop(0, nsteps, step, (lo, hi, chi, clo))
            return it + 1, lo, hi, chi, clo
        return body

    _, lo, hi, chi, clo = lax.while_loop(exact_cond, rounds(4), (jnp.int32(0), lo0, hi0, chi0, clo0))

    def w_cond(carry):
        it, _, hi, chi, clo = carry
        open_rows = jnp.max(jnp.where(few | (clo == kf), 0.0, 1.0)) > 0.0
        return jnp.logical_and(open_rows, jnp.logical_and(it < 48, not_done(hi, chi, clo)))

    _, lo, hi, chi, clo = lax.while_loop(w_cond, rounds(8), (jnp.int32(0), lo, hi, chi, clo))
    exact = clo == kf
    tau, ceq = finish(hi, chi)
    tau = jnp.where(exact, lo, tau)
    need = jnp.where(exact, 0.0, kf - chi)
    gtf = jnp.where(x >= jnp.where(exact, lo, hi), 1.0, 0.0)
    eqf = jnp.where(jnp.logical_and(x == tau, jnp.logical_not(exact)), 1.0, 0.0)
    has_tie = jnp.max(jnp.where(few | exact | (ceq <= need), 0.0, 1.0)) > 0.0

    def tie_path(_):
        nchunk = width // LANES
        su = jnp.where(_iota((LANES, LANES), 0) < _iota((LANES, LANES), 1), 1.0, 0.0).astype(BF16)
        run = jnp.zeros_like(need)
        pieces = []
        for c in range(nchunk):
            e = eqf[:, c * LANES:(c + 1) * LANES]
            before = jnp.dot(e.astype(BF16), su, preferred_element_type=F32) + run
            pieces.append(jnp.where(before < need, e, 0.0))
            run = run + jnp.sum(e, axis=1, keepdims=True)
        return jnp.concatenate(pieces, axis=1)

    sel_eq = lax.cond(has_tie, tie_path, lambda _: eqf, 0)
    return jnp.where(few, validf, gtf + sel_eq)


def _dsa_prompt_kernel(qi_ref, smq_ref, sma_ref, qh_ref, kt_ref, v_ref, o_ref, *, n_sel, j0, qb):
    j = j0 + pl.program_id(1)
    S = kt_ref.shape[-1]
    qi = qi_ref[0]
    wi = smq_ref[0][:, _SM_IW:_SM_IW + IDX_HEADS] * (IDX_HEADS ** -0.5 * IDX_DIM ** -0.5)
    ki = sma_ref[0][:, _SM_IK:_SM_IK + IDX_DIM]
    score = jnp.zeros((qb, S), F32)
    for h in range(IDX_HEADS):
        s = _dot_nt(qi[:, h * IDX_DIM:(h + 1) * IDX_DIM], ki)
        score = score + jnp.maximum(s, 0.0) * wi[:, h:h + 1]
    tpos = j * qb + _iota((qb, 1), 0)
    valid = _iota((1, S), 1) <= tpos
    mask = _topk_mask(score, valid, n_sel)
    bias = jnp.where(mask > 0.0, 0.0, -jnp.inf)
    bias2 = jnp.concatenate([bias, bias], axis=0)
    G = DSA_HEADS // DSA_KV_HEADS
    outs = []
    for kv in range(DSA_KV_HEADS):
        q2 = qh_ref[0, G * kv:G * (kv + 1)].reshape(G * qb, DSA_DH) * (DSA_DH ** -0.5)
        s = _dot(q2, kt_ref[0, kv]) + bias2
        p = jnp.exp(s - jnp.max(s, axis=1, keepdims=True))
        o = _dot(p, v_ref[0, kv]) / jnp.sum(p, axis=1, keepdims=True)
        outs += [o[g * qb:(g + 1) * qb] for g in range(G)]
    o_ref[0] = jnp.concatenate(outs, axis=1)


def _dsa_prompt(qi, small, qh, kt, vh, n_sel, qb):
    B, T, _ = qi.shape
    nq = T // qb
    nseg = 8 if nq % 8 == 0 else (4 if nq % 4 == 0 else 1)
    qps = nq // nseg
    outs = []
    for seg in range(nseg):
        j0 = seg * qps
        S = (seg + 1) * qps * qb
        outs.append(pl.pallas_call(
            functools.partial(_dsa_prompt_kernel, n_sel=n_sel, j0=j0, qb=qb),
            grid=(B, qps),
            in_specs=[pl.BlockSpec((1, qb, IDX_HEADS * IDX_DIM), lambda b, j, j0=j0: (b, j0 + j, 0)),
                      pl.BlockSpec((1, qb, LANES), lambda b, j, j0=j0: (b, j0 + j, 0)),
                      pl.BlockSpec((1, S, LANES), lambda b, j: (b, 0, 0)),
                      pl.BlockSpec((1, DSA_HEADS, qb, DSA_DH), lambda b, j, j0=j0: (b, 0, j0 + j, 0)),
                      pl.BlockSpec((1, DSA_KV_HEADS, DSA_DH, S), lambda b, j: (b, 0, 0, 0)),
                      pl.BlockSpec((1, DSA_KV_HEADS, S, DSA_DH), lambda b, j: (b, 0, 0, 0))],
            out_specs=pl.BlockSpec((1, qb, DSA_HEADS * DSA_DH), lambda b, j: (b, j, 0)),
            out_shape=jax.ShapeDtypeStruct((B, qps * qb, DSA_HEADS * DSA_DH), F32),
            compiler_params=_cparams("parallel", "arbitrary"),
            name="dsa_prompt",
        )(qi, small, small, qh, kt, vh))
    return outs[0] if nseg == 1 else jnp.concatenate(outs, axis=1)


def _dsa_sample_score_kernel(pt_ref, qi_ref, wi_ref, kit_hbm, o_ref, buf, sem, *, n_pages):
    b = pl.program_id(0)

    def fetch(sample, slot):
        def issue(p, carry):
            pltpu.make_async_copy(kit_hbm.at[pt_ref[sample, p]], buf.at[slot, p], sem.at[slot]).start()
            return carry

        lax.fori_loop(0, n_pages, issue, 0, unroll=8)

    @pl.when(b == 0)
    def _():
        fetch(0, 0)

    @pl.when(b + 1 < pl.num_programs(0))
    def _():
        fetch(b + 1, (b + 1) % 2)

    slot = b % 2
    pltpu.make_async_copy(kit_hbm.at[pl.ds(0, n_pages)], buf.at[slot], sem.at[slot]).wait()
    qi = qi_ref[0]
    wi = wi_ref[0] * (IDX_HEADS ** -0.5 * IDX_DIM ** -0.5)
    for p in range(n_pages):
        s = _dot(qi, buf[slot, p])
        o_ref[0, :, p * PAGE_SIZE:(p + 1) * PAGE_SIZE] = jnp.sum(jnp.maximum(s, 0.0) * wi, axis=0, keepdims=True)


def _dsa_sample_scores(page_table, qi8, wi8, cache_kit):
    Bd, n_pages = page_table.shape
    grid_spec = pltpu.PrefetchScalarGridSpec(
        num_scalar_prefetch=1,
        grid=(Bd,),
        in_specs=[pl.BlockSpec((1, 8, IDX_DIM), lambda b, pt: (b, 0, 0)),
                  pl.BlockSpec((1, 8, 1), lambda b, pt: (b, 0, 0)),
                  pl.BlockSpec(memory_space=pl.ANY)],
        out_specs=pl.BlockSpec((1, 1, n_pages * PAGE_SIZE), lambda b, pt: (b, 0, 0)),
        scratch_shapes=[pltpu.VMEM((2, n_pages, IDX_DIM, PAGE_SIZE), F32), pltpu.SemaphoreType.DMA((2,))],
    )
    return pl.pallas_call(
        functools.partial(_dsa_sample_score_kernel, n_pages=n_pages),
        grid_spec=grid_spec,
        out_shape=jax.ShapeDtypeStruct((Bd, 1, n_pages * PAGE_SIZE), F32),
        compiler_params=_cparams("arbitrary"),
        name="dsa_sample_scores",
    )(page_table, qi8, wi8, cache_kit)


def _dsa_sample_select_kernel(sc_ref, qi_ref, sm_ref, sel_ref, seln_ref, *, n_sel):
    Bd, past = sc_ref.shape
    width = past + LANES
    sm = sm_ref[...]
    qi = _bf(qi_ref[...]).astype(F32)
    ki = _bf(sm[:, _SM_IK:_SM_IK + IDX_DIM]).astype(F32)
    wi = sm[:, _SM_IW:_SM_IW + IDX_HEADS] * (IDX_HEADS ** -0.5 * IDX_DIM ** -0.5)
    snew = jnp.zeros((Bd, 1), F32)
    for h in range(IDX_HEADS):
        s = jnp.sum(qi[:, h * IDX_DIM:(h + 1) * IDX_DIM] * ki, axis=1, keepdims=True)
        snew = snew + jnp.maximum(s, 0.0) * wi[:, h:h + 1]
    tail = jnp.where(_iota((Bd, LANES), 1) == 0, snew, -jnp.inf)
    x = jnp.concatenate([sc_ref[...], tail], axis=1)
    valid = jnp.broadcast_to(_iota((1, width), 1) <= past, (Bd, width))
    mask = _topk_mask(x, valid, n_sel)
    sel_ref[...] = mask[:, :past]
    seln_ref[...] = mask[:, past:]


def _dsa_sample_select(scores, qi, small, n_sel):
    Bd, past = scores.shape
    return pl.pallas_call(
        functools.partial(_dsa_sample_select_kernel, n_sel=n_sel),
        out_shape=[jax.ShapeDtypeStruct((Bd, past), F32), jax.ShapeDtypeStruct((Bd, LANES), F32)],
        compiler_params=pltpu.CompilerParams(vmem_limit_bytes=VMEM_LIMIT_BYTES),
        name="dsa_sample_select",
    )(scores, qi, small)


_MASKED = -1e30


def _dsa_sample_attn_kernel(pt_ref, q_ref, sel_ref, seln_ref, kn_ref, vn_ref, kt_hbm, vt_hbm, o_ref,
                            kbuf, vbuf, sem, m_scr, l_scr, acc_scr, *, n_pg):
    jj = pl.program_id(1)
    njj = pl.num_programs(1)
    step = pl.program_id(0) * njj + jj
    G = DSA_HEADS // DSA_KV_HEADS
    row_kv = _iota((DSA_HEADS, DSA_DH), 0) // G

    def fetch(st, slot):
        sample = st // njj
        first = (st % njj) * n_pg

        def issue(p, carry):
            page = pt_ref[sample, first + p]
            pltpu.make_async_copy(kt_hbm.at[page], kbuf.at[slot, p], sem.at[0, slot]).start()
            pltpu.make_async_copy(vt_hbm.at[page], vbuf.at[slot, p], sem.at[1, slot]).start()
            return carry

        lax.fori_loop(0, n_pg, issue, 0, unroll=8)

    @pl.when(step == 0)
    def _():
        fetch(0, 0)

    @pl.when(step + 1 < pl.num_programs(0) * njj)
    def _():
        fetch(step + 1, (step + 1) % 2)

    slot = step % 2
    pltpu.make_async_copy(kt_hbm.at[pl.ds(0, n_pg)], kbuf.at[slot], sem.at[0, slot]).wait()
    pltpu.make_async_copy(vt_hbm.at[pl.ds(0, n_pg)], vbuf.at[slot], sem.at[1, slot]).wait()

    @pl.when(jj == 0)
    def _():
        m_scr[...] = jnp.full_like(m_scr, _MASKED)
        l_scr[...] = jnp.zeros_like(l_scr)
        acc_scr[...] = jnp.zeros_like(acc_scr)

    def block(kts, vts, msk):
        s = jnp.dot(_bf(q_ref[0, 0]), kts[0], preferred_element_type=F32)
        for kv in range(1, DSA_KV_HEADS):
            s = s + jnp.dot(_bf(q_ref[0, kv]), kts[kv], preferred_element_type=F32)
        s = s * (DSA_DH ** -0.5)
        on = msk > 0.0
        m_old = m_scr[...]
        m_new = jnp.maximum(m_old, jnp.max(jnp.where(on, s, _MASKED), axis=1, keepdims=True))
        alpha = jnp.exp(m_old - m_new)
        p = jnp.where(on, jnp.exp(s - m_new), 0.0)
        l_scr[...] = l_scr[...] * alpha + jnp.sum(p, axis=1, keepdims=True)
        pb = _bf(p)
        acc = acc_scr[...] * alpha
        for kv in range(DSA_KV_HEADS):
            o_kv = lax.dot_general(pb, vts[kv], _NT, preferred_element_type=F32)
            acc = acc + jnp.where(row_kv == kv, o_kv, 0.0)
        acc_scr[...] = acc
        m_scr[...] = m_new

    gather = lambda pages, kv: jnp.concatenate([_bf(pages[slot, p, kv]) for p in range(n_pg)], axis=1)
    block([gather(kbuf, kv) for kv in range(DSA_KV_HEADS)], [gather(vbuf, kv) for kv in range(DSA_KV_HEADS)],
          sel_ref[0])

    @pl.when(jj == njj - 1)
    def _():
        block([_bf(kn_ref[0, kv]) for kv in range(DSA_KV_HEADS)], [_bf(vn_ref[0, kv]) for kv in range(DSA_KV_HEADS)],
              seln_ref[0])
        o_ref[0] = acc_scr[...] / l_scr[...]


def _dsa_sample_attn(page_table, q8, sel, sel_new, k_new_b, v_new_b, cache_kt, cache_vt, n_pg):
    Bd, n_pages = page_table.shape
    new_tok = pl.BlockSpec((1, DSA_KV_HEADS, DSA_DH, PAGE_SIZE), lambda b, jj, pt: (b, 0, 0, 0))
    pages = pltpu.VMEM((2, n_pg, DSA_KV_HEADS, DSA_DH, PAGE_SIZE), F32)
    grid_spec = pltpu.PrefetchScalarGridSpec(
        num_scalar_prefetch=1,
        grid=(Bd, n_pages // n_pg),
        in_specs=[pl.BlockSpec((1, DSA_KV_HEADS, DSA_HEADS, DSA_DH), lambda b, jj, pt: (b, 0, 0, 0)),
                  pl.BlockSpec((1, 1, n_pg * PAGE_SIZE), lambda b, jj, pt: (b, 0, jj)),
                  pl.BlockSpec((1, 1, PAGE_SIZE), lambda b, jj, pt: (b, 0, 0)), new_tok, new_tok,
                  pl.BlockSpec(memory_space=pl.ANY), pl.BlockSpec(memory_space=pl.ANY)],
        out_specs=pl.BlockSpec((1, DSA_HEADS, DSA_DH), lambda b, jj, pt: (b, 0, 0)),
        scratch_shapes=[pages, pages, pltpu.SemaphoreType.DMA((2, 2)),
                        pltpu.VMEM((DSA_HEADS, 1), F32), pltpu.VMEM((DSA_HEADS, 1), F32),
                        pltpu.VMEM((DSA_HEADS, DSA_DH), F32)],
    )
    return pl.pallas_call(
        functools.partial(_dsa_sample_attn_kernel, n_pg=n_pg),
        grid_spec=grid_spec,
        out_shape=jax.ShapeDtypeStruct((Bd, DSA_HEADS, DSA_DH), F32),
        compiler_params=_cparams("arbitrary", "arbitrary"),
        name="dsa_sample_attn",
    )(page_table, q8, sel.reshape(Bd, 1, -1), sel_new.reshape(Bd, 1, PAGE_SIZE), k_new_b, v_new_b, cache_kt, cache_vt)


def _attend_rows(q, kk, vv, scale):
    s = jnp.sum(kk * q[None], axis=-1, keepdims=True) * scale
    m = jnp.max(s, axis=0, keepdims=True)
    p = jnp.exp(s - m)
    l = jnp.sum(p, axis=0)
    return jnp.sum(p * vv, axis=0) / l


def _mem_prompt_kernel(q_ref, mk_ref, mv_ref, o_ref):
    q = q_ref[0]
    mk = mk_ref[0]
    mv = mv_ref[0]
    outs = []
    for h in range(MEM_HEADS):
        sl = slice(h * MEM_DH, (h + 1) * MEM_DH)
        s = _dot_nt(q[:, sl], mk[:, sl]) * (MEM_DH ** -0.5)
        m = jnp.max(s, axis=1, keepdims=True)
        p = jnp.exp(s - m)
        p = p / jnp.sum(p, axis=1, keepdims=True)
        outs.append(_dot(p, mv[:, sl]))
    o_ref[0] = jnp.concatenate(outs, axis=1)


def _mem_prompt(q, mk, mv, tq):
    B, T, W = q.shape
    M = mk.shape[1]
    kv = pl.BlockSpec((1, M, W), lambda b, i: (b, 0, 0))
    return pl.pallas_call(
        _mem_prompt_kernel,
        grid=(B, T // tq),
        in_specs=[pl.BlockSpec((1, tq, W), lambda b, i: (b, i, 0)), kv, kv],
        out_specs=pl.BlockSpec((1, tq, W), lambda b, i: (b, i, 0)),
        out_shape=jax.ShapeDtypeStruct((B, T, W), F32),
        compiler_params=_cparams("parallel", "parallel"),
        name="mem_prompt",
    )(q, mk, mv)


def _mem_sample_kernel(q_ref, k_ref, v_ref, o_ref, *, ns):
    for i in range(ns):
        o_ref[i] = _attend_rows(q_ref[i], k_ref[i], v_ref[i], MEM_DH ** -0.5)


def _mem_sample(q, ck, cv, ns):
    Bd, M = ck.shape[:2]
    kv = pl.BlockSpec((ns, M, MEM_HEADS, MEM_DH), lambda b: (b, 0, 0, 0))
    qs = pl.BlockSpec((ns, MEM_HEADS, MEM_DH), lambda b: (b, 0, 0))
    return pl.pallas_call(
        functools.partial(_mem_sample_kernel, ns=ns),
        grid=(Bd // ns,),
        in_specs=[qs, kv, kv],
        out_specs=qs,
        out_shape=jax.ShapeDtypeStruct((Bd, MEM_HEADS, MEM_DH), F32),
        compiler_params=_cparams("parallel"),
        name="mem_sample",
    )(q, ck, cv)


def _merge_kernel(x_ref, og_ref, od_ref, om_ref, gn_ref, wg_ref, wb_ref, wo_ref, fn_ref, xo_ref, h2_ref):
    x = x_ref[...]
    d = x.shape[1]
    hb = _rms(x, gn_ref[...]).astype(BF16)
    acc = jnp.zeros_like(x)
    for n, o_ref in enumerate((og_ref, od_ref, om_ref)):
        gate = _sigmoid(jnp.dot(hb, wg_ref[:, n * d:(n + 1) * d], preferred_element_type=F32))
        acc = acc + gate * jnp.dot(_bf(o_ref[...]), wb_ref[n], preferred_element_type=F32)
    xo = x + jnp.dot(_bf(acc), wo_ref[...], preferred_element_type=F32)
    xo_ref[...] = xo
    h2_ref[...] = _rms(xo, fn_ref[...])


def _merge(x, o_g, o_d, o_m, gn, wg, wb, wo, fn, tm):
    n, d = x.shape
    tok = lambda w: pl.BlockSpec((tm, w), lambda i: (i, 0))
    full = lambda shape: pl.BlockSpec(shape, lambda i: (0,) * len(shape))
    return pl.pallas_call(
        _merge_kernel,
        grid=(n // tm,),
        in_specs=[tok(d), tok(BRANCH_W), tok(BRANCH_W), tok(BRANCH_W), full((1, d)), full((d, N_BRANCH * d)),
                  full((N_BRANCH, BRANCH_W, d)), full((d, d)), full((1, d))],
        out_specs=[tok(d), tok(d)],
        out_shape=[jax.ShapeDtypeStruct((n, d), F32)] * 2,
        compiler_params=_cparams("parallel"),
        name="merge",
    )(x, o_g, o_d, o_m, gn.reshape(1, d), wg, wb, wo, fn.reshape(1, d))


def _route_kernel(h_ref, wr_ref, br_ref, e_ref, p_ref, r_ref, cnt_ref, run_scr, *, tr):
    i = pl.program_id(0)

    @pl.when(i == 0)
    def _():
        run_scr[...] = jnp.zeros_like(run_scr)

    logits = _dot_nt(wr_ref[...], h_ref[...]) + br_ref[...]
    eidx = _iota((N_EXPERTS, LANES), 0)
    su = jnp.where(_iota((LANES, LANES), 0) < _iota((LANES, LANES), 1), 1.0, 0.0).astype(BF16)
    run = run_scr[...]
    for c in range(tr // LANES):
        sl = slice(c * LANES, (c + 1) * LANES)
        l = logits[:, sl]
        vals, idxs = [], []
        for _ in range(TOP_K):
            m = jnp.max(l, axis=0, keepdims=True)
            idx = jnp.min(jnp.where(l == m, eidx, N_EXPERTS), axis=0, keepdims=True)
            vals.append(m)
            idxs.append(idx)
            l = jnp.where(eidx == idx, -jnp.inf, l)
        ex = [jnp.exp(v - vals[0]) for v in vals]
        den = ex[0] + ex[1] + ex[2] + ex[3]
        oh = jnp.zeros((N_EXPERTS, LANES), F32)
        for k in range(TOP_K):
            p_ref[k:k + 1, sl] = ex[k] / den
            e_ref[k:k + 1, sl] = idxs[k]
            oh = oh + jnp.where(eidx == idxs[k], 1.0, 0.0)
        before = jnp.dot(oh.astype(BF16), su, preferred_element_type=F32) + run
        for k in range(TOP_K):
            rk = jnp.sum(jnp.where(eidx == idxs[k], before, 0.0), axis=0, keepdims=True)
            r_ref[k:k + 1, sl] = rk.astype(I32)
        run = run + jnp.sum(oh, axis=1, keepdims=True)
    run_scr[...] = run
    cnt_ref[...] = jnp.broadcast_to(run, cnt_ref.shape)


def _route(h2, wr_t, br, tr):
    n, d = h2.shape
    tokrow = pl.BlockSpec((TOP_K, tr), lambda i: (0, i))
    return pl.pallas_call(
        functools.partial(_route_kernel, tr=tr),
        grid=(n // tr,),
        in_specs=[pl.BlockSpec((tr, d), lambda i: (i, 0)),
                  pl.BlockSpec((N_EXPERTS, d), lambda i: (0, 0)),
                  pl.BlockSpec((N_EXPERTS, 1), lambda i: (0, 0))],
        out_specs=[tokrow, tokrow, tokrow, pl.BlockSpec((N_EXPERTS, LANES), lambda i: (0, 0))],
        out_shape=[jax.ShapeDtypeStruct((TOP_K, n), I32), jax.ShapeDtypeStruct((TOP_K, n), F32),
                   jax.ShapeDtypeStruct((TOP_K, n), I32), jax.ShapeDtypeStruct((N_EXPERTS, LANES), F32)],
        scratch_shapes=[pltpu.VMEM((N_EXPERTS, 1), F32)],
        compiler_params=_cparams("arbitrary"),
        name="moe_route",
    )(h2, wr_t, br)


def _dispatch_kernel(plo_ref, pn_ref, nb_ref, dest_ref, h_ref, xg_hbm, zblk, sem, *, td):
    blk = zblk.shape[0]
    n_blocks = xg_hbm.shape[0] // blk

    def row_copy(t, dst):
        return pltpu.make_async_copy(h_ref.at[pl.ds(t, 1)], xg_hbm.at[pl.ds(dst, 1)], sem.at[0])

    def zero_block(i):
        return pltpu.make_async_copy(zblk, xg_hbm.at[pl.ds(pl.multiple_of(i * blk, blk), blk)], sem.at[2])

    def issue(t, carry):
        for k in range(TOP_K):
            row_copy(t, dest_ref[k, t]).start()
        return carry

    lax.fori_loop(0, td, issue, 0, unroll=8)

    @pl.when(pl.program_id(0) == 0)
    def _():
        zblk[...] = jnp.zeros_like(zblk)

        def tail_start(i, c):
            zero_block(i).start()
            return c

        def tail_wait(i, c):
            zero_block(0).wait()
            return c

        lax.fori_loop(nb_ref[0], n_blocks, tail_start, 0)
        lax.fori_loop(nb_ref[0], n_blocks, tail_wait, 0)

        big = [1 << i for i in reversed(range(3, blk.bit_length() - 1))]

        def per_expert(e, carry):
            lo = plo_ref[e]
            n = pn_ref[e]
            head = jnp.minimum(n, (-lo) & 7)
            body = n - head
            for wait in (False, True):
                def zero(off, size, on):
                    copy = pltpu.make_async_copy(zblk.at[pl.ds(0, size)], xg_hbm.at[pl.ds(off, size)], sem.at[1])

                    @pl.when(on)
                    def _():
                        copy.wait() if wait else copy.start()

                for r in range(7):
                    zero(lo + r, 1, r < head)
                off = lo + head
                for size in big:
                    zero(pl.multiple_of(off, 8), size, (body & size) != 0)
                    off = off + (body & size)
                for r in range(7):
                    zero(off + r, 1, r < (body & 7))
            return carry

        lax.fori_loop(0, N_EXPERTS, per_expert, 0)

    for k in range(TOP_K):
        pltpu.make_async_copy(h_ref, xg_hbm.at[pl.ds(0, td)], sem.at[0]).wait()


def _dispatch(h2, dest, pad_lo, pad_n, nb_used, n_slots, blk, td):
    n, d = h2.shape
    grid_spec = pltpu.PrefetchScalarGridSpec(
        num_scalar_prefetch=3,
        grid=(n // td,),
        in_specs=[pl.BlockSpec((TOP_K, td), lambda i, plo, pn, nb: (0, i), memory_space=pltpu.SMEM),
                  pl.BlockSpec((td, d), lambda i, plo, pn, nb: (i, 0))],
        out_specs=pl.BlockSpec(memory_space=pl.ANY),
        scratch_shapes=[pltpu.VMEM((blk, d), F32), pltpu.SemaphoreType.DMA((3,))],
    )
    return pl.pallas_call(
        functools.partial(_dispatch_kernel, td=td),
        grid_spec=grid_spec,
        out_shape=jax.ShapeDtypeStruct((n_slots, d), F32),
        compiler_params=_cparams("arbitrary"),
        name="moe_dispatch",
    )(pad_lo, pad_n, nb_used, dest, h2)


def _expert_kernel(be_ref, nb_ref, x_ref, wgu_ref, bgu_ref, wdn_ref, bdn_ref, o_ref):
    del be_ref
    f = wdn_ref.shape[1]

    used = pl.program_id(0) < nb_ref[0]

    @pl.when(jnp.logical_not(used))
    def _():
        o_ref[...] = jnp.zeros_like(o_ref)

    @pl.when(used)
    def _():
        gu = jnp.dot(_bf(x_ref[...]), _bf(wgu_ref[0]), preferred_element_type=F32) + bgu_ref[0]
        gate = jnp.minimum(gu[:, :f], SWIGLU_LIMIT)
        up = jnp.clip(gu[:, f:], -SWIGLU_LIMIT, SWIGLU_LIMIT)
        glu = gate * _sigmoid(SWIGLU_ALPHA * gate)
        o_ref[...] = jnp.dot(_bf((up + 1.0) * glu), _bf(wdn_ref[0]), preferred_element_type=F32) + bdn_ref[0]


def _experts(xg, block_e, nb_used, wgu, bgu, wdn, bdn, blk):
    n_slots, d = xg.shape
    f = wdn.shape[1]
    blk_of = lambda i, nb: jnp.minimum(i, nb[0] - 1)
    tok = pl.BlockSpec((blk, d), lambda i, be, nb: (blk_of(i, nb), 0))
    per_e = lambda shape: pl.BlockSpec((1,) + shape, lambda i, be, nb: (be[blk_of(i, nb)], 0, 0))
    grid_spec = pltpu.PrefetchScalarGridSpec(
        num_scalar_prefetch=2,
        grid=(n_slots // blk,),
        in_specs=[tok, per_e((d, 2 * f)), per_e((1, 2 * f)), per_e((f, d)), per_e((1, d))],
        out_specs=pl.BlockSpec((blk, d), lambda i, be, nb: (i, 0)),
    )
    return pl.pallas_call(
        _expert_kernel,
        grid_spec=grid_spec,
        out_shape=jax.ShapeDtypeStruct((n_slots, d), F32),
        compiler_params=_cparams("arbitrary"),
        name="moe_experts",
    )(block_e, nb_used, xg, wgu, bgu, wdn, bdn)


def _combine_kernel(dest_ref, x_ref, p_ref, g_ref, yb_hbm, o_ref, buf, sem, *, tc):
    def row_copy(src, k, t):
        return pltpu.make_async_copy(yb_hbm.at[pl.ds(src, 1)], buf.at[k, pl.ds(t, 1)], sem.at[0])

    def issue(t, carry):
        for k in range(TOP_K):
            row_copy(dest_ref[k, t], k, t).start()
        return carry

    lax.fori_loop(0, tc, issue, 0, unroll=8)
    for k in range(TOP_K):
        pltpu.make_async_copy(yb_hbm.at[pl.ds(0, tc)], buf.at[k], sem.at[0]).wait()
    p = p_ref[...]
    acc = p[:, 0:1] * buf[0]
    for k in range(1, TOP_K):
        acc = acc + p[:, k:k + 1] * buf[k]
    o_ref[...] = _rms(x_ref[...] + acc, g_ref[...])


def _combine(x, yb, dest, p_t, g, tc):
    n, d = x.shape
    return pl.pallas_call(
        functools.partial(_combine_kernel, tc=tc),
        grid=(n // tc,),
        in_specs=[pl.BlockSpec((TOP_K, tc), lambda i: (0, i), memory_space=pltpu.SMEM),
                  pl.BlockSpec((tc, d), lambda i: (i, 0)),
                  pl.BlockSpec((tc, TOP_K), lambda i: (i, 0)),
                  pl.BlockSpec((1, d), lambda i: (0, 0)),
                  pl.BlockSpec(memory_space=pl.ANY)],
        out_specs=pl.BlockSpec((tc, d), lambda i: (i, 0)),
        out_shape=jax.ShapeDtypeStruct((n, d), F32),
        scratch_shapes=[pltpu.VMEM((TOP_K, tc, d), F32), pltpu.SemaphoreType.DMA((1,))],
        compiler_params=_cparams("arbitrary"),
        name="moe_combine",
    )(dest, x, p_t, g.reshape(1, d), yb)


def _moe_final(x, h2, g_final, wr_t, br, wgu, bgu, wdn, bdn, blk, tile):
    n, d = x.shape
    e, p, rank, cnt = _route(h2, wr_t, br, tile)
    counts = cnt[:, 0].astype(I32)
    padded = (counts + blk - 1) // blk * blk
    pad_end = jnp.cumsum(padded)
    pad_start = pad_end - padded
    n_blocks = -(-n * TOP_K // blk) + N_EXPERTS
    expert_ids = jnp.arange(N_EXPERTS, dtype=I32)[:, None, None]
    dest = rank + jnp.sum(jnp.where(e[None] == expert_ids, pad_start[:, None, None], 0), axis=0)
    block_lo = jnp.arange(n_blocks, dtype=I32)[:, None] * blk
    block_e = jnp.minimum(jnp.sum((pad_end[None, :] <= block_lo).astype(I32), axis=1), N_EXPERTS - 1)
    nb_used = (pad_end[-1:] // blk).astype(I32)
    xg = _dispatch(h2, dest, pad_start + counts, padded - counts, nb_used, n_blocks * blk, blk, tile)
    yb = _experts(xg, block_e, nb_used, wgu, bgu, wdn, bdn, blk)
    return _combine(x, yb, dest, p.T, g_final, min(tile, 256))


def _prep_w_in(w):
    cuts = np.cumsum((0,) + _SPLITS)
    seg = lambda i: w[:, int(cuts[i]):int(cuts[i + 1])]
    small = jnp.concatenate([seg(8), seg(9), seg(2), seg(3), jnp.zeros((w.shape[0], LANES - _SM_END), w.dtype)], axis=1)
    w1 = jnp.concatenate([seg(0), seg(1), seg(4), seg(5), seg(6), seg(7), seg(10), small], axis=1)
    return w1.astype(BF16), w[:, int(cuts[-1]):].astype(BF16)


def kernel(x_prompt, x_sample, cache_k, cache_v, cache_k_idx, cache_mem_k, cache_mem_v, state_gdn, state_conv,
           page_table, mem_prompt, norm_attn, w_in, conv_w, gdn_a_log, gdn_dt_bias, gdn_norm, norm_mem, w_mem_kv,
           w_branch, w_out, norm_ffn, w_router, b_router, w_gate_up, b_gate_up, w_down, b_down, norm_final):
    B, T, D = x_prompt.shape
    Bd, Ts, _ = x_sample.shape
    assert Ts == 1 and w_in.shape[0] == 1, "one layer, one new token per sample"
    H = GDN_HEADS
    G = DSA_HEADS // DSA_KV_HEADS
    n_pages = page_table.shape[1]
    M = mem_prompt.shape[1]

    w1, wg = _prep_w_in(w_in[0])
    wb = w_branch[0].astype(BF16)
    wo = w_out[0].astype(BF16)
    wr_t = w_router[0].T
    br = b_router[0].reshape(N_EXPERTS, 1)
    wgu = w_gate_up[0]
    wdn = w_down[0]
    bgu = b_gate_up[0].reshape(N_EXPERTS, 1, -1)
    bdn = b_down[0].reshape(N_EXPERTS, 1, -1)
    alog = gdn_a_log[0]
    dtb = gdn_dt_bias[0]

    xp = x_prompt.reshape(B * T, D)
    g_qkv, g_z, d_k, d_v, i_q, m_q, small, qh, kt, vh = _proj_prompt(xp, norm_attn[0], w1, B, T, min(512, T))

    quad_rows = lambda a: (a.reshape(B, T // GDN_CHUNK, GDN_CHUNK, 2, 4).transpose(0, 1, 3, 4, 2)
                           .reshape(B, T // GDN_CHUNK, 2, 1, QW))
    per_group = lambda a: jnp.repeat(a, GDN_DK).reshape(2, 1, QW)
    o_g, ssm_q = _gdn_prompt(g_qkv.reshape(B, T, GDN_QKV), g_z.reshape(B, T, H * GDN_DV),
                             quad_rows(small[:, _SM_GA:_SM_GA + H]), quad_rows(small[:, _SM_GB:_SM_GB + H]),
                             conv_w[0], per_group(alog), per_group(dtb), jnp.tile(gdn_norm[0], 4).reshape(1, QW),
                             min(512, T))
    o_g = o_g.reshape(B * T, H * GDN_DV)
    ssm_p = ssm_q.reshape(B, 2, GDN_DK, 4, GDN_DV).transpose(0, 1, 3, 2, 4).reshape(B, H, GDN_DK, GDN_DV)

    o_d = _dsa_prompt(i_q.reshape(B, T, -1), small.reshape(B, T, LANES), qh, kt, vh, min(TOPK_MAX, T // 4),
                      min(256, T))
    o_d = o_d.reshape(B * T, DSA_HEADS * DSA_DH)

    mk, mv = _norm_matmul(mem_prompt.reshape(B * M, D), norm_mem[0], w_mem_kv[0].astype(BF16),
                          (MEM_HEADS * MEM_DH,) * 2, min(512, B * M))
    o_m = _mem_prompt(m_q.reshape(B, T, -1), mk.reshape(B, M, -1), mv.reshape(B, M, -1), min(512, T))
    o_m = o_m.reshape(B * T, MEM_HEADS * MEM_DH)

    xres, h2 = _merge(xp, o_g, o_d, o_m, norm_attn[0], wg, wb, wo, norm_ffn[0], min(256, B * T))
    y_prompt = _moe_final(xres, h2, norm_final, wr_t, br, wgu, bgu, wdn, bdn, 256, min(512, B * T))

    k_prompt = d_k.reshape(1, B, T, DSA_KV_HEADS, DSA_DH)
    v_prompt = d_v.reshape(1, B, T, DSA_KV_HEADS, DSA_DH)
    kidx_prompt = small[:, _SM_IK:_SM_IK + IDX_DIM].reshape(1, B, T, IDX_DIM)
    memk_prompt = mk.reshape(1, B, M, MEM_HEADS, MEM_DH)
    memv_prompt = mv.reshape(1, B, M, MEM_HEADS, MEM_DH)
    conv_prompt = g_qkv.reshape(B, T, GDN_QKV)[:, T - (GDN_CONV - 1):, :][None]

    xs = x_sample.reshape(Bd, D)
    s_qkv, s_z, sd_q, sd_k, sd_v, si_q, sm_q, ssmall = _norm_matmul(xs, norm_attn[0], w1, _PROJ_SPLITS, Bd)

    lanes_b = lambda a: jnp.broadcast_to(a[..., None], a.shape + (Bd,))
    og_t, s_t = _gdn_sample(
        s_qkv.T, state_conv[0].transpose(1, 2, 0), lanes_b(conv_w[0]), s_z.T,
        ssmall[:, _SM_GA:_SM_GA + H].T.reshape(H, 1, Bd), ssmall[:, _SM_GB:_SM_GB + H].T.reshape(H, 1, Bd),
        lanes_b(alog.reshape(H, 1)), lanes_b(dtb.reshape(H, 1)), lanes_b(gdn_norm[0]),
        state_gdn[0].transpose(1, 2, 3, 0))
    so_g = og_t.T
    ssm_sample = s_t.transpose(3, 0, 1, 2)[None]
    conv_sample = jnp.concatenate([state_conv[0][:, 1:], s_qkv[:, None, :]], axis=1)[None]

    qi8 = jnp.pad(si_q.reshape(Bd, IDX_HEADS, IDX_DIM), ((0, 0), (0, 8 - IDX_HEADS), (0, 0)))
    wi8 = jnp.pad(ssmall[:, _SM_IW:_SM_IW + IDX_HEADS], ((0, 0), (0, 8 - IDX_HEADS)))[..., None]
    n_pg = 32 if n_pages % 32 == 0 else n_pages
    cache_kit = cache_k_idx[0].transpose(0, 2, 1)
    cache_kt = cache_k[0].transpose(0, 2, 3, 1)
    cache_vt = cache_v[0].transpose(0, 2, 3, 1)
    scores = _dsa_sample_scores(page_table, qi8, wi8, cache_kit)
    scores = scores.reshape(Bd, n_pages * PAGE_SIZE)
    n_sel = min(TOPK_MAX, (n_pages * PAGE_SIZE + 1) // 4)
    sel, sel_new = _dsa_sample_select(scores, si_q, ssmall, n_sel)
    over_lanes = lambda a: jnp.broadcast_to(a[..., None], a.shape + (PAGE_SIZE,))
    head_kv = jnp.arange(DSA_HEADS, dtype=I32) // G
    q8 = jnp.where((head_kv[None, :] == jnp.arange(DSA_KV_HEADS, dtype=I32)[:, None])[None, :, :, None],
                   sd_q.reshape(Bd, 1, DSA_HEADS, DSA_DH), 0.0)
    k_new = sd_k.reshape(Bd, DSA_KV_HEADS, DSA_DH)
    v_new = sd_v.reshape(Bd, DSA_KV_HEADS, DSA_DH)
    so_d = _dsa_sample_attn(page_table, q8, sel, sel_new, over_lanes(k_new), over_lanes(v_new),
                            cache_kt, cache_vt, n_pg).reshape(Bd, DSA_HEADS * DSA_DH)

    so_m = _mem_sample(sm_q.reshape(Bd, MEM_HEADS, MEM_DH), cache_mem_k[0], cache_mem_v[0], 4)
    so_m = so_m.reshape(Bd, MEM_HEADS * MEM_DH)

    sres, sh2 = _merge(xs, so_g, so_d, so_m, norm_attn[0], wg, wb, wo, norm_ffn[0], Bd)
    y_sample = _moe_final(sres, sh2, norm_final, wr_t, br, wgu, bgu, wdn, bdn, 256, Bd)

    return (y_prompt.reshape(B, T, D), y_sample.reshape(Bd, 1, D), k_prompt, v_prompt, kidx_prompt,
            memk_prompt, memv_prompt, ssm_p[None], conv_prompt,
            k_new.reshape(1, Bd, 1, DSA_KV_HEADS, DSA_DH), v_new.reshape(1, Bd, 1, DSA_KV_HEADS, DSA_DH),
            ssmall[:, _SM_IK:_SM_IK + IDX_DIM].reshape(1, Bd, 1, IDX_DIM), ssm_sample, conv_sample)
```

```python
import functools

import numpy as np
import jax
import jax.numpy as jnp
from jax import lax
from jax.experimental import pallas as pl
from jax.experimental.pallas import tpu as pltpu

F32 = jnp.float32
BF16 = jnp.bfloat16
I32 = jnp.int32

EPS = 1e-6
GDN_HEADS = 8
GDN_DK = 64
GDN_DV = 64
GDN_CONV = 4
GDN_CHUNK = 64
GDN_QKV = 2 * GDN_HEADS * GDN_DK + GDN_HEADS * GDN_DV
DSA_HEADS = 8
DSA_KV_HEADS = 4
DSA_DH = 64
IDX_HEADS = 4
IDX_DIM = 64
TOPK_MAX = 256
Q_BLOCK = 128
MEM_HEADS = 4
MEM_DH = 128
N_BRANCH = 3
BRANCH_W = 512
N_EXPERTS = 32
TOP_K = 4
SWIGLU_LIMIT = 7.0
SWIGLU_ALPHA = 1.702
PAGE_SIZE = 128
LANES = 128

_SPLITS = (GDN_QKV, GDN_HEADS * GDN_DV, GDN_HEADS, GDN_HEADS,
           DSA_HEADS * DSA_DH, DSA_KV_HEADS * DSA_DH, DSA_KV_HEADS * DSA_DH,
           IDX_HEADS * IDX_DIM, IDX_DIM, IDX_HEADS,
           MEM_HEADS * MEM_DH)
_PROJ_SPLITS = (GDN_QKV, 512, 512, 256, 256, 256, 512, LANES)
_SM_IK = 0
_SM_IW = IDX_DIM
_SM_GB = _SM_IW + IDX_HEADS
_SM_GA = _SM_GB + GDN_HEADS
_SM_END = _SM_GA + GDN_HEADS

VMEM_LIMIT_BYTES = 56 * 1024 * 1024


def _cparams(*sem):
    return pltpu.CompilerParams(dimension_semantics=sem, vmem_limit_bytes=VMEM_LIMIT_BYTES)


def _bf(x):
    return x.astype(BF16)


def _dot(a, b):
    return jnp.dot(_bf(a), _bf(b), preferred_element_type=F32)


_NT = (((1,), (1,)), ((), ()))


def _dot_nt(a, b):
    return lax.dot_general(_bf(a), _bf(b), _NT, preferred_element_type=F32)


def _split2(x):
    hi = x.astype(BF16)
    lo = (x - hi.astype(F32)).astype(BF16)
    return hi, lo


def _split3(x):
    hi = x.astype(BF16)
    r = x - hi.astype(F32)
    mid = r.astype(BF16)
    lo = (r - mid.astype(F32)).astype(BF16)
    return hi, mid, lo


def _dot_nt3(a, b):
    ah, al = _split2(a)
    bh, bl = _split2(b)
    d = lambda x, y: lax.dot_general(x, y, _NT, preferred_element_type=F32)
    return d(ah, bh) + (d(ah, bl) + d(al, bh))


def _dot3(a, b):
    ah, al = _split2(a)
    bh, bl = _split2(b)
    d = lambda x, y: jnp.dot(x, y, preferred_element_type=F32)
    return d(ah, bh) + (d(ah, bl) + d(al, bh))


def _dot_exact01(a, b01):
    hi, mid, lo = _split3(a)
    d = lambda x: jnp.dot(x, b01, preferred_element_type=F32)
    return d(hi) + (d(mid) + d(lo))


def _rms(x, g):
    return x * lax.rsqrt(jnp.mean(x * x, axis=-1, keepdims=True) + EPS) * g


def _sigmoid(x):
    return 1.0 / (1.0 + jnp.exp(-x))


def _silu(x):
    return x * _sigmoid(x)


def _softplus(x):
    return jnp.maximum(x, 0.0) + jnp.log(1.0 + jnp.exp(-jnp.abs(x)))


def _iota(shape, axis):
    return lax.broadcasted_iota(I32, shape, axis)


TT = 8


def _tt_load(ref, n):
    return jnp.concatenate([ref[pl.ds(c, n, stride=TT), :] for c in range(TT)], axis=1)


def _tt_store(ref, val):
    n = val.shape[0]
    for c in range(TT):
        ref[pl.ds(c, n, stride=TT), :] = val[:, c * LANES:(c + 1) * LANES]


def _norm_matmul_kernel(x_ref, g_ref, w_ref, *o_refs, splits):
    hb = _rms(x_ref[...], g_ref[...]).astype(BF16)
    off = 0
    for o_ref, n in zip(o_refs, splits):
        o_ref[...] = jnp.dot(hb, w_ref[:, off:off + n], preferred_element_type=F32)
        off += n


def _norm_matmul(x, g, w, splits, tm):
    n, d = x.shape
    return pl.pallas_call(
        functools.partial(_norm_matmul_kernel, splits=splits),
        grid=(n // tm,),
        in_specs=[pl.BlockSpec((tm, d), lambda i: (i, 0)),
                  pl.BlockSpec((1, d), lambda i: (0, 0)),
                  pl.BlockSpec((d, sum(splits)), lambda i: (0, 0))],
        out_specs=[pl.BlockSpec((tm, s), lambda i: (i, 0)) for s in splits],
        out_shape=[jax.ShapeDtypeStruct((n, s), F32) for s in splits],
        compiler_params=_cparams("parallel"),
        name="norm_matmul",
    )(x, g.reshape(1, d), w)


def _proj_prompt_kernel(x_ref, g_ref, w_ref, qkv_ref, z_ref, dk_ref, dv_ref, iq_ref, mq_ref, sm_ref,
                        qh_ref, kt_ref, vh_ref):
    hb = _rms(x_ref[...], g_ref[...]).astype(BF16)
    offs = np.cumsum((0,) + _PROJ_SPLITS)
    part = lambda i: jnp.dot(hb, w_ref[:, int(offs[i]):int(offs[i + 1])], preferred_element_type=F32)
    qkv_ref[...] = part(0)
    z_ref[...] = part(1)
    dq = part(2)
    for h in range(DSA_HEADS):
        qh_ref[0, h] = dq[:, h * DSA_DH:(h + 1) * DSA_DH]
    dk = part(3)
    dk_ref[...] = dk
    kt_ref[0] = dk.T.reshape(DSA_KV_HEADS, DSA_DH, dk.shape[0])
    dv = part(4)
    dv_ref[...] = dv
    for kv in range(DSA_KV_HEADS):
        vh_ref[0, kv] = dv[:, kv * DSA_DH:(kv + 1) * DSA_DH]
    iq_ref[...] = part(5)
    mq_ref[...] = part(6)
    sm_ref[...] = part(7)


def _proj_prompt(x, g, w, B, T, tm):
    n, d = x.shape
    tpb = T // tm
    tok = lambda width: pl.BlockSpec((tm, width), lambda b, t: (b * tpb + t, 0))
    flat = lambda width: jax.ShapeDtypeStruct((n, width), F32)
    sp = _PROJ_SPLITS
    return pl.pallas_call(
        _proj_prompt_kernel,
        grid=(B, tpb),
        in_specs=[tok(d), pl.BlockSpec((1, d), lambda b, t: (0, 0)), pl.BlockSpec((d, sum(sp)), lambda b, t: (0, 0))],
        out_specs=[tok(sp[0]), tok(sp[1]), tok(sp[3]), tok(sp[4]), tok(sp[5]), tok(sp[6]), tok(sp[7]),
                   pl.BlockSpec((1, DSA_HEADS, tm, DSA_DH), lambda b, t: (b, 0, t, 0)),
                   pl.BlockSpec((1, DSA_KV_HEADS, DSA_DH, tm), lambda b, t: (b, 0, 0, t)),
                   pl.BlockSpec((1, DSA_KV_HEADS, tm, DSA_DH), lambda b, t: (b, 0, t, 0))],
        out_shape=[flat(sp[0]), flat(sp[1]), flat(sp[3]), flat(sp[4]), flat(sp[5]), flat(sp[6]), flat(sp[7]),
                   jax.ShapeDtypeStruct((B, DSA_HEADS, T, DSA_DH), F32),
                   jax.ShapeDtypeStruct((B, DSA_KV_HEADS, DSA_DH, T), F32),
                   jax.ShapeDtypeStruct((B, DSA_KV_HEADS, T, DSA_DH), F32)],
        compiler_params=_cparams("parallel", "parallel"),
        name="proj_prompt",
    )(x, g.reshape(1, d), w)


QW = 4 * GDN_DK


def _bd_stack(x):
    lane_head = (_iota(x.shape, 1) >> 6) & 3
    return jnp.concatenate([jnp.where(lane_head == h, x, 0.0) for h in range(4)], axis=0)


def _bd_dot(a, x):
    return jnp.dot(_bf(a), _bf(_bd_stack(x)), preferred_element_type=F32)


def _bd_split(x):
    xh = x.astype(BF16).astype(F32)
    return _bf(_bd_stack(xh)), _bf(_bd_stack(x - xh))


def _bd_dot3(a, xb):
    ah, al = _split2(a)
    bh, bl = xb
    d = lambda p, q: jnp.dot(p, q, preferred_element_type=F32)
    return d(ah, bh) + (d(ah, bl) + d(al, bh))


def _gdn_prompt_kernel(q_ref, k_ref, v_ref, qh_ref, kh_ref, vh_ref, z_ref, a_ref, b_ref,
                       cwq_ref, cwk_ref, cwv_ref, alog_ref, dtb_ref, nw_ref, o_ref, s_ref,
                       xq, xk, xv, s_scr, *, tt):
    ti = pl.program_id(1)
    C = GDN_CHUNK
    HALO = 8
    NG = GDN_HEADS // 4
    CPI = 2

    @pl.when(ti == 0)
    def _():
        s_scr[...] = jnp.zeros_like(s_scr)

    r = _iota((QW, QW), 0)
    c = _iota((QW, QW), 1)
    same = (r >> 6) == (c >> 6)
    ones_bd = jnp.where(same, 1.0, 0.0).astype(BF16)
    su_bd = jnp.where(same, jnp.where((r & 63) > (c & 63), 1.0, 0.0), 0.0).astype(BF16)
    rhs01 = jnp.concatenate([su_bd, ones_bd], axis=1)
    ident = jnp.where(r == c, 1.0, 0.0).astype(BF16)

    keep = (ti > 0).astype(F32)
    for idx, (src, halo, dst, cw) in enumerate(((q_ref, qh_ref, xq, cwq_ref), (k_ref, kh_ref, xk, cwk_ref),
                                                (v_ref, vh_ref, xv, cwv_ref))):
        dst[0:HALO, :] = halo[0] * keep
        dst[HALO:HALO + tt, :] = src[0]
        w = cw[...]
        y = w[0:1, :] * dst[pl.ds(HALO - 3, tt), :]
        for i in range(1, GDN_CONV):
            y = y + w[i:i + 1, :] * dst[pl.ds(HALO - 3 + i, tt), :]
        y = _silu(y)
        if idx < 2:
            ss = jnp.concatenate([_dot_exact01((y * y)[:, g * QW:(g + 1) * QW], ones_bd) for g in range(NG)], axis=1)
            y = y * lax.rsqrt(ss + EPS)
            if idx == 0:
                y = y * (GDN_DK ** -0.5)
        dst[HALO:HALO + tt, :] = y

    ri = _iota((C, QW), 0)
    li = _iota((C, QW), 1) & 63
    lane_head = _iota((C, QW), 1) >> 6
    incl = li <= ri
    strict = li < ri
    eye = li == ri
    nw = nw_ref[...]

    def solve_steps(cis, out):
        streams = [(ci, g) for ci in cis for g in range(NG)]
        n = range(len(streams))
        rows = [slice(HALO + ci * C, HALO + (ci + 1) * C) for ci, _ in streams]
        cols = [slice(g * QW, (g + 1) * QW) for _, g in streams]
        qc = [xq[rows[i], cols[i]] for i in n]
        kc = [xk[rows[i], cols[i]] for i in n]
        vc = [xv[rows[i], cols[i]] for i in n]
        g_row = [-jnp.exp(alog_ref[g]) * _softplus(a_ref[0, ci, g] + dtb_ref[g]) for ci, g in streams]
        b_row = [_sigmoid(b_ref[0, ci, g]) for ci, g in streams]
        res = [_dot_exact01(jnp.concatenate([jnp.where(incl, g_row[i], 0.0), jnp.where(eye, b_row[i], 0.0)], axis=0),
                            rhs01) for i in n]
        yield
        gcol = [res[i][:C, QW:] for i in n]
        bcol = [res[i][C:, QW:] for i in n]
        gamma = [jnp.where(incl, jnp.exp(res[i][:C, :QW]), 0.0) for i in n]
        eg = [jnp.exp(gcol[i]) for i in n]
        glast = [gcol[i][C - 1:C, :] for i in n]
        kb = [kc[i] * bcol[i] for i in n]
        kt = [lax.dot_general(ident, jnp.concatenate([_bf(kc[i])] * 4, axis=0), _NT, preferred_element_type=F32)
              for i in n]
        yield
        kt = [_bf(jnp.where(same, kt[i], 0.0)) for i in n]
        kk = [jnp.dot(_bf(jnp.concatenate([kb[i], qc[i]], axis=0)), kt[i], preferred_element_type=F32)
              for i in n]
        yield
        r = [jnp.where(strict, -(kk[i][:C] * gamma[i]), 0.0) for i in n]
        t = [jnp.where(eye, 1.0, r[i]) for i in n]
        for k in range(6):
            rb = [_bd_split(r[i]) for i in n]
            if k > 0:
                t = [t[i] + _bd_dot3(t[i], rb[i]) for i in n]
            if k < 5:
                r = [_bd_dot3(r[i], rb[i]) for i in n]
            yield
        rhs = [_bd_split(jnp.concatenate([vc[i] * bcol[i], kb[i] * eg[i]], axis=1)) for i in n]
        y = [_bd_dot3(t[i], rhs[i]) for i in n]
        out.extend(dict(ci=streams[i][0], g=streams[i][1], y=y[i], qg=qc[i] * eg[i], qk=kk[i][C:] * gamma[i],
                        kd=kc[i] * jnp.exp(glast[i] - gcol[i]), dec=jnp.exp(glast[i])) for i in n)

    def scan_steps(st):
        n = range(len(st))
        s = [s_scr[t["g"]] for t in st]
        ws = [_bd_dot(jnp.concatenate([st[i]["y"][:, QW:], st[i]["qg"]], axis=0), s[i]) for i in n]
        yield
        vn = [st[i]["y"][:, :QW] - ws[i][:C] for i in n]
        o = [ws[i][C:] + _bd_dot(st[i]["qk"], vn[i]) for i in n]
        kdt = [lax.dot_general(ident, _bf(st[i]["kd"]), _NT, preferred_element_type=F32) for i in n]
        yield
        full = [jnp.dot(_bf(kdt[i]), _bf(vn[i]), preferred_element_type=F32) for i in n]
        ms = [_dot_exact01(o[i] * o[i], ones_bd) * (1.0 / GDN_DV) for i in n]
        yield
        for i in n:
            g = st[i]["g"]
            sadd = jnp.where(lane_head == 0, full[i][0:C], 0.0)
            for h in range(1, 4):
                sadd = sadd + jnp.where(lane_head == h, full[i][h * C:(h + 1) * C], 0.0)
            s_scr[g] = s[i] * st[i]["dec"] + sadd
            rows = slice(st[i]["ci"] * C, (st[i]["ci"] + 1) * C)
            cols = slice(g * QW, (g + 1) * QW)
            o_ref[0, rows, cols] = o[i] * lax.rsqrt(ms[i] + EPS) * nw * _silu(z_ref[0, rows, cols])

    def scan_chunks(st):
        for j in range(len(st) // NG):
            yield from scan_steps(st[j * NG:(j + 1) * NG])

    def run(*gens):
        live = list(gens)
        while live:
            for gen in list(live):
                try:
                    next(gen)
                except StopIteration:
                    live.remove(gen)

    n_groups = tt // (CPI * C)
    solved = [[] for _ in range(n_groups)]
    run(solve_steps(range(CPI), solved[0]))
    for gi in range(1, n_groups):
        run(solve_steps(range(gi * CPI, (gi + 1) * CPI), solved[gi]), scan_chunks(solved[gi - 1]))
    run(scan_chunks(solved[-1]))
    s_ref[0] = s_scr[...]


def _gdn_prompt(qkv, z, a, b, cw, alog, dtb, nw, tt):
    B, T, _ = qkv.shape
    nct = tt // GDN_CHUNK
    NG = GDN_HEADS // 4
    W = NG * QW
    col = lambda j: pl.BlockSpec((1, tt, W), lambda bi, ti: (bi, ti, j))
    halo = lambda j: pl.BlockSpec((1, 8, W), lambda bi, ti: (bi, jnp.maximum(ti * (tt // 8) - 1, 0), j))
    chunked = pl.BlockSpec((1, nct, NG, 1, QW), lambda bi, ti: (bi, ti, 0, 0, 0))
    cwcol = lambda j: pl.BlockSpec((GDN_CONV, W), lambda bi, ti: (0, j))
    per_group = pl.BlockSpec((NG, 1, QW), lambda bi, ti: (0, 0, 0))
    return pl.pallas_call(
        functools.partial(_gdn_prompt_kernel, tt=tt),
        grid=(B, T // tt),
        in_specs=[col(0), col(1), col(2), halo(0), halo(1), halo(2), col(0), chunked, chunked,
                  cwcol(0), cwcol(1), cwcol(2), per_group, per_group,
                  pl.BlockSpec((1, QW), lambda bi, ti: (0, 0))],
        out_specs=[col(0), pl.BlockSpec((1, NG, GDN_DK, QW), lambda bi, ti: (bi, 0, 0, 0))],
        out_shape=[jax.ShapeDtypeStruct((B, T, W), F32), jax.ShapeDtypeStruct((B, NG, GDN_DK, QW), F32)],
        scratch_shapes=[pltpu.VMEM((8 + tt, W), F32)] * 3 + [pltpu.VMEM((NG, GDN_DK, QW), F32)],
        compiler_params=_cparams("parallel", "arbitrary"),
        name="gdn_prompt",
    )(qkv, qkv, qkv, qkv, qkv, qkv, z, a, b, cw, cw, cw, alog, dtb, nw)


def _gdn_sample_kernel(q_ref, k_ref, v_ref, cq_ref, ck_ref, cv_ref, wq_ref, wk_ref, wv_ref,
                       z_ref, a_ref, b_ref, alog_ref, dtb_ref, nw_ref, s_ref, o_ref, so_ref):
    def conv(x_ref, c_ref, w_ref):
        y = w_ref[GDN_CONV - 1] * x_ref[...]
        for i in range(GDN_CONV - 1):
            y = y + w_ref[i] * c_ref[i]
        return _silu(y)

    q = conv(q_ref, cq_ref, wq_ref)
    q = q * lax.rsqrt(jnp.sum(q * q, axis=0, keepdims=True) + EPS) * (GDN_DK ** -0.5)
    k = conv(k_ref, ck_ref, wk_ref)
    k = k * lax.rsqrt(jnp.sum(k * k, axis=0, keepdims=True) + EPS)
    v = conv(v_ref, cv_ref, wv_ref)
    beta = _sigmoid(b_ref[0])
    g = -jnp.exp(alog_ref[0]) * _softplus(a_ref[0] + dtb_ref[0])
    dec = jnp.exp(g)
    ks = jnp.zeros_like(v)
    for i in range(GDN_DK):
        ks = ks + k[i:i + 1, :] * (s_ref[0, i] * dec)
    vn = beta * (v - ks)
    o = jnp.zeros_like(v)
    for i in range(GDN_DK):
        sn = s_ref[0, i] * dec + k[i:i + 1, :] * vn
        so_ref[0, i] = sn
        o = o + q[i:i + 1, :] * sn
    on = o * lax.rsqrt(jnp.mean(o * o, axis=0, keepdims=True) + EPS) * nw_ref[...]
    o_ref[...] = on * _silu(z_ref[...])


def _gdn_sample(qkv_t, conv_t, cw_t, z_t, a_t, b_t, alog_t, dtb_t, nw_t, s_t):
    nb = qkv_t.shape[1]
    H, DK, DV = GDN_HEADS, GDN_DK, GDN_DV
    row = lambda off: pl.BlockSpec((DK, nb), lambda h: (off + h, 0))
    crow = lambda off: pl.BlockSpec((GDN_CONV - 1, DK, nb), lambda h: (0, off + h, 0))
    wrow = lambda off: pl.BlockSpec((GDN_CONV, DK, nb), lambda h: (0, off + h, 0))
    per_head = pl.BlockSpec((1, 1, nb), lambda h: (h, 0, 0))
    return pl.pallas_call(
        _gdn_sample_kernel,
        grid=(H,),
        in_specs=[row(0), row(H), row(2 * H), crow(0), crow(H), crow(2 * H), wrow(0), wrow(H), wrow(2 * H),
                  row(0), per_head, per_head, per_head, per_head,
                  pl.BlockSpec((DV, nb), lambda h: (0, 0)),
                  pl.BlockSpec((1, DK, DV, nb), lambda h: (h, 0, 0, 0))],
        out_specs=[row(0), pl.BlockSpec((1, DK, DV, nb), lambda h: (h, 0, 0, 0))],
        out_shape=[jax.ShapeDtypeStruct((H * DV, nb), F32), jax.ShapeDtypeStruct((H, DK, DV, nb), F32)],
        compiler_params=_cparams("parallel"),
        name="gdn_sample",
    )(qkv_t, qkv_t, qkv_t, conv_t, conv_t, conv_t, cw_t, cw_t, cw_t, z_t, a_t, b_t, alog_t, dtb_t, nw_t, s_t)


def _count_ge(x, thr):
    return jnp.sum(jnp.where(x >= thr, 1.0, 0.0), axis=1, keepdims=True)


def _topk_mask(score, valid, k):
    rows, width = score.shape
    kf = float(k)
    x = jnp.where(valid, score, -jnp.inf)
    validf = jnp.where(valid, 1.0, 0.0)
    nvalid = jnp.sum(validf, axis=1, keepdims=True)
    few = nvalid <= kf
    mx = jnp.max(x, axis=1, keepdims=True)
    mn = jnp.min(jnp.where(valid, score, jnp.inf), axis=1, keepdims=True)
    lo0 = mn
    hi0 = mx + (jnp.abs(mx) * 1e-6 + 1e-30)
    chi0 = jnp.zeros_like(mx)
    clo0 = nvalid

    def step(_, carry):
        lo, hi, chi, clo = carry
        mid = 0.5 * lo + 0.5 * hi
        c = _count_ge(x, mid)
        ge = c >= kf
        return jnp.where(ge, mid, lo), jnp.where(ge, hi, mid), jnp.where(ge, chi, c), jnp.where(ge, c, clo)

    def finish(hi, chi):
        tau = jnp.max(jnp.where(x < hi, x, -jnp.inf), axis=1, keepdims=True)
        ceq = jnp.sum(jnp.where(x == tau, 1.0, 0.0), axis=1, keepdims=True)
        return tau, ceq

    def not_done(hi, chi, clo):
        tau, ceq = finish(hi, chi)
        bad = jnp.where(few | (clo == kf) | (chi + ceq >= kf), 0.0, 1.0)
        return jnp.max(bad) > 0.0

    def exact_cond(carry):
        it, _, _, _, clo = carry
        open_rows = jnp.max(jnp.where(few | (clo == kf), 0.0, 1.0)) > 0.0
        return jnp.logical_and(it < 6, open_rows)

    def rounds(nsteps):
        def body(carry):
            it, lo, hi, chi, clo = carry
            lo, hi, chi, clo = lax.fori_loop(0, nsteps, step, (lo, hi, chi, clo))
            return it + 1, lo, hi, chi, clo
        return body

    _, lo, hi, chi, clo = lax.while_loop(exact_cond, rounds(4), (jnp.int32(0), lo0, hi0, chi0, clo0))

    def w_cond(carry):
        it, _, hi, chi, clo = carry
        open_rows = jnp.max(jnp.where(few | (clo == kf), 0.0, 1.0)) > 0.0
        return jnp.logical_and(open_rows, jnp.logical_and(it < 48, not_done(hi, chi, clo)))

    _, lo, hi, chi, clo = lax.while_loop(w_cond, rounds(8), (jnp.int32(0), lo, hi, chi, clo))
    exact = clo == kf
    tau, ceq = finish(hi, chi)
    tau = jnp.where(exact, lo, tau)
    need = jnp.where(exact, 0.0, kf - chi)
    gtf = jnp.where(x >= jnp.where(exact, lo, hi), 1.0, 0.0)
    eqf = jnp.where(jnp.logical_and(x == tau, jnp.logical_not(exact)), 1.0, 0.0)
    has_tie = jnp.max(jnp.where(few | exact | (ceq <= need), 0.0, 1.0)) > 0.0

    def tie_path(_):
        nchunk = width // LANES
        su = jnp.where(_iota((LANES, LANES), 0) < _iota((LANES, LANES), 1), 1.0, 0.0).astype(BF16)
        run = jnp.zeros_like(need)
        pieces = []
        for c in range(nchunk):
            e = eqf[:, c * LANES:(c + 1) * LANES]
            before = jnp.dot(e.astype(BF16), su, preferred_element_type=F32) + run
            pieces.append(jnp.where(before < need, e, 0.0))
            run = run + jnp.sum(e, axis=1, keepdims=True)
        return jnp.concatenate(pieces, axis=1)

    sel_eq = lax.cond(has_tie, tie_path, lambda _: eqf, 0)
    return jnp.where(few, validf, gtf + sel_eq)


def _dsa_prompt_kernel(qi_ref, smq_ref, sma_ref, qh_ref, kt_ref, v_ref, o_ref, *, n_sel, j0, qb):
    j = j0 + pl.program_id(1)
    S = kt_ref.shape[-1]
    qi = qi_ref[0]
    wi = smq_ref[0][:, _SM_IW:_SM_IW + IDX_HEADS] * (IDX_HEADS ** -0.5 * IDX_DIM ** -0.5)
    ki = sma_ref[0][:, _SM_IK:_SM_IK + IDX_DIM]
    score = jnp.zeros((qb, S), F32)
    for h in range(IDX_HEADS):
        s = _dot_nt(qi[:, h * IDX_DIM:(h + 1) * IDX_DIM], ki)
        score = score + jnp.maximum(s, 0.0) * wi[:, h:h + 1]
    tpos = j * qb + _iota((qb, 1), 0)
    valid = _iota((1, S), 1) <= tpos
    mask = _topk_mask(score, valid, n_sel)
    bias = jnp.where(mask > 0.0, 0.0, -jnp.inf)
    bias2 = jnp.concatenate([bias, bias], axis=0)
    G = DSA_HEADS // DSA_KV_HEADS
    outs = []
    for kv in range(DSA_KV_HEADS):
        q2 = qh_ref[0, G * kv:G * (kv + 1)].reshape(G * qb, DSA_DH) * (DSA_DH ** -0.5)
        s = _dot(q2, kt_ref[0, kv]) + bias2
        p = jnp.exp(s - jnp.max(s, axis=1, keepdims=True))
        o = _dot(p, v_ref[0, kv]) / jnp.sum(p, axis=1, keepdims=True)
        outs += [o[g * qb:(g + 1) * qb] for g in range(G)]
    o_ref[0] = jnp.concatenate(outs, axis=1)


def _dsa_prompt(qi, small, qh, kt, vh, n_sel, qb):
    B, T, _ = qi.shape
    nq = T // qb
    nseg = 8 if nq % 8 == 0 else (4 if nq % 4 == 0 else 1)
    qps = nq // nseg
    outs = []
    for seg in range(nseg):
        j0 = seg * qps
        S = (seg + 1) * qps * qb
        outs.append(pl.pallas_call(
            functools.partial(_dsa_prompt_kernel, n_sel=n_sel, j0=j0, qb=qb),
            grid=(B, qps),
            in_specs=[pl.BlockSpec((1, qb, IDX_HEADS * IDX_DIM), lambda b, j, j0=j0: (b, j0 + j, 0)),
                      pl.BlockSpec((1, qb, LANES), lambda b, j, j0=j0: (b, j0 + j, 0)),
                      pl.BlockSpec((1, S, LANES), lambda b, j: (b, 0, 0)),
                      pl.BlockSpec((1, DSA_HEADS, qb, DSA_DH), lambda b, j, j0=j0: (b, 0, j0 + j, 0)),
                      pl.BlockSpec((1, DSA_KV_HEADS, DSA_DH, S), lambda b, j: (b, 0, 0, 0)),
                      pl.BlockSpec((1, DSA_KV_HEADS, S, DSA_DH), lambda b, j: (b, 0, 0, 0))],
            out_specs=pl.BlockSpec((1, qb, DSA_HEADS * DSA_DH), lambda b, j: (b, j, 0)),
            out_shape=jax.ShapeDtypeStruct((B, qps * qb, DSA_HEADS * DSA_DH), F32),
            compiler_params=_cparams("parallel", "arbitrary"),
            name="dsa_prompt",
        )(qi, small, small, qh, kt, vh))
    return outs[0] if nseg == 1 else jnp.concatenate(outs, axis=1)


def _dsa_sample_score_kernel(pt_ref, qi_ref, wi_ref, kit_hbm, o_ref, buf, sem, *, n_pages):
    b = pl.program_id(0)

    def fetch(sample, slot):
        def issue(p, carry):
            pltpu.make_async_copy(kit_hbm.at[pt_ref[sample, p]], buf.at[slot, p], sem.at[slot]).start()
            return carry

        lax.fori_loop(0, n_pages, issue, 0, unroll=8)

    @pl.when(b == 0)
    def _():
        fetch(0, 0)

    @pl.when(b + 1 < pl.num_programs(0))
    def _():
        fetch(b + 1, (b + 1) % 2)

    slot = b % 2
    pltpu.make_async_copy(kit_hbm.at[pl.ds(0, n_pages)], buf.at[slot], sem.at[slot]).wait()
    qi = qi_ref[0]
    wi = wi_ref[0] * (IDX_HEADS ** -0.5 * IDX_DIM ** -0.5)
    for p in range(n_pages):
        s = _dot(qi, buf[slot, p])
        o_ref[0, :, p * PAGE_SIZE:(p + 1) * PAGE_SIZE] = jnp.sum(jnp.maximum(s, 0.0) * wi, axis=0, keepdims=True)


def _dsa_sample_scores(page_table, qi8, wi8, cache_kit):
    Bd, n_pages = page_table.shape
    grid_spec = pltpu.PrefetchScalarGridSpec(
        num_scalar_prefetch=1,
        grid=(Bd,),
        in_specs=[pl.BlockSpec((1, 8, IDX_DIM), lambda b, pt: (b, 0, 0)),
                  pl.BlockSpec((1, 8, 1), lambda b, pt: (b, 0, 0)),
                  pl.BlockSpec(memory_space=pl.ANY)],
        out_specs=pl.BlockSpec((1, 1, n_pages * PAGE_SIZE), lambda b, pt: (b, 0, 0)),
        scratch_shapes=[pltpu.VMEM((2, n_pages, IDX_DIM, PAGE_SIZE), F32), pltpu.SemaphoreType.DMA((2,))],
    )
    return pl.pallas_call(
        functools.partial(_dsa_sample_score_kernel, n_pages=n_pages),
        grid_spec=grid_spec,
        out_shape=jax.ShapeDtypeStruct((Bd, 1, n_pages * PAGE_SIZE), F32),
        compiler_params=_cparams("arbitrary"),
        name="dsa_sample_scores",
    )(page_table, qi8, wi8, cache_kit)


def _dsa_sample_select_kernel(sc_ref, qi_ref, sm_ref, sel_ref, seln_ref, *, n_sel):
    Bd, past = sc_ref.shape
    width = past + LANES
    sm = sm_ref[...]
    qi = _bf(qi_ref[...]).astype(F32)
    ki = _bf(sm[:, _SM_IK:_SM_IK + IDX_DIM]).astype(F32)
    wi = sm[:, _SM_IW:_SM_IW + IDX_HEADS] * (IDX_HEADS ** -0.5 * IDX_DIM ** -0.5)
    snew = jnp.zeros((Bd, 1), F32)
    for h in range(IDX_HEADS):
        s = jnp.sum(qi[:, h * IDX_DIM:(h + 1) * IDX_DIM] * ki, axis=1, keepdims=True)
        snew = snew + jnp.maximum(s, 0.0) * wi[:, h:h + 1]
    tail = jnp.where(_iota((Bd, LANES), 1) == 0, snew, -jnp.inf)
    x = jnp.concatenate([sc_ref[...], tail], axis=1)
    valid = jnp.broadcast_to(_iota((1, width), 1) <= past, (Bd, width))
    mask = _topk_mask(x, valid, n_sel)
    sel_ref[...] = mask[:, :past]
    seln_ref[...] = mask[:, past:]


def _dsa_sample_select(scores, qi, small, n_sel):
    Bd, past = scores.shape
    return pl.pallas_call(
        functools.partial(_dsa_sample_select_kernel, n_sel=n_sel),
        out_shape=[jax.ShapeDtypeStruct((Bd, past), F32), jax.ShapeDtypeStruct((Bd, LANES), F32)],
        compiler_params=pltpu.CompilerParams(vmem_limit_bytes=VMEM_LIMIT_BYTES),
        name="dsa_sample_select",
    )(scores, qi, small)


_MASKED = -1e30


def _dsa_sample_attn_kernel(pt_ref, q_ref, sel_ref, seln_ref, kn_ref, vn_ref, kt_hbm, vt_hbm, o_ref,
                            kbuf, vbuf, sem, m_scr, l_scr, acc_scr, *, n_pg):
    jj = pl.program_id(1)
    njj = pl.num_programs(1)
    step = pl.program_id(0) * njj + jj
    G = DSA_HEADS // DSA_KV_HEADS
    row_kv = _iota((DSA_HEADS, DSA_DH), 0) // G

    def fetch(st, slot):
        sample = st // njj
        first = (st % njj) * n_pg

        def issue(p, carry):
            page = pt_ref[sample, first + p]
            pltpu.make_async_copy(kt_hbm.at[page], kbuf.at[slot, p], sem.at[0, slot]).start()
            pltpu.make_async_copy(vt_hbm.at[page], vbuf.at[slot, p], sem.at[1, slot]).start()
            return carry

        lax.fori_loop(0, n_pg, issue, 0, unroll=8)

    @pl.when(step == 0)
    def _():
        fetch(0, 0)

    @pl.when(step + 1 < pl.num_programs(0) * njj)
    def _():
        fetch(step + 1, (step + 1) % 2)

    slot = step % 2
    pltpu.make_async_copy(kt_hbm.at[pl.ds(0, n_pg)], kbuf.at[slot], sem.at[0, slot]).wait()
    pltpu.make_async_copy(vt_hbm.at[pl.ds(0, n_pg)], vbuf.at[slot], sem.at[1, slot]).wait()

    @pl.when(jj == 0)
    def _():
        m_scr[...] = jnp.full_like(m_scr, _MASKED)
        l_scr[...] = jnp.zeros_like(l_scr)
        acc_scr[...] = jnp.zeros_like(acc_scr)

    def block(kts, vts, msk):
        s = jnp.dot(_bf(q_ref[0, 0]), kts[0], preferred_element_type=F32)
        for kv in range(1, DSA_KV_HEADS):
            s = s + jnp.dot(_bf(q_ref[0, kv]), kts[kv], preferred_element_type=F32)
        s = s * (DSA_DH ** -0.5)
        on = msk > 0.0
        m_old = m_scr[...]
        m_new = jnp.maximum(m_old, jnp.max(jnp.where(on, s, _MASKED), axis=1, keepdims=True))
        alpha = jnp.exp(m_old - m_new)
        p = jnp.where(on, jnp.exp(s - m_new), 0.0)
        l_scr[...] = l_scr[...] * alpha + jnp.sum(p, axis=1, keepdims=True)
        pb = _bf(p)
        acc = acc_scr[...] * alpha
        for kv in range(DSA_KV_HEADS):
            o_kv = lax.dot_general(pb, vts[kv], _NT, preferred_element_type=F32)
            acc = acc + jnp.where(row_kv == kv, o_kv, 0.0)
        acc_scr[...] = acc
        m_scr[...] = m_new

    gather = lambda pages, kv: jnp.concatenate([_bf(pages[slot, p, kv]) for p in range(n_pg)], axis=1)
    block([gather(kbuf, kv) for kv in range(DSA_KV_HEADS)], [gather(vbuf, kv) for kv in range(DSA_KV_HEADS)],
          sel_ref[0])

    @pl.when(jj == njj - 1)
    def _():
        block([_bf(kn_ref[0, kv]) for kv in range(DSA_KV_HEADS)], [_bf(vn_ref[0, kv]) for kv in range(DSA_KV_HEADS)],
              seln_ref[0])
        o_ref[0] = acc_scr[...] / l_scr[...]


def _dsa_sample_attn(page_table, q8, sel, sel_new, k_new_b, v_new_b, cache_kt, cache_vt, n_pg):
    Bd, n_pages = page_table.shape
    new_tok = pl.BlockSpec((1, DSA_KV_HEADS, DSA_DH, PAGE_SIZE), lambda b, jj, pt: (b, 0, 0, 0))
    pages = pltpu.VMEM((2, n_pg, DSA_KV_HEADS, DSA_DH, PAGE_SIZE), F32)
    grid_spec = pltpu.PrefetchScalarGridSpec(
        num_scalar_prefetch=1,
        grid=(Bd, n_pages // n_pg),
        in_specs=[pl.BlockSpec((1, DSA_KV_HEADS, DSA_HEADS, DSA_DH), lambda b, jj, pt: (b, 0, 0, 0)),
                  pl.BlockSpec((1, 1, n_pg * PAGE_SIZE), lambda b, jj, pt: (b, 0, jj)),
                  pl.BlockSpec((1, 1, PAGE_SIZE), lambda b, jj, pt: (b, 0, 0)), new_tok, new_tok,
                  pl.BlockSpec(memory_space=pl.ANY), pl.BlockSpec(memory_space=pl.ANY)],
        out_specs=pl.BlockSpec((1, DSA_HEADS, DSA_DH), lambda b, jj, pt: (b, 0, 0)),
        scratch_shapes=[pages, pages, pltpu.SemaphoreType.DMA((2, 2)),
                        pltpu.VMEM((DSA_HEADS, 1), F32), pltpu.VMEM((DSA_HEADS, 1), F32),
                        pltpu.VMEM((DSA_HEADS, DSA_DH), F32)],
    )
    return pl.pallas_call(
        functools.partial(_dsa_sample_attn_kernel, n_pg=n_pg),
        grid_spec=grid_spec,
        out_shape=jax.ShapeDtypeStruct((Bd, DSA_HEADS, DSA_DH), F32),
        compiler_params=_cparams("arbitrary", "arbitrary"),
        name="dsa_sample_attn",
    )(page_table, q8, sel.reshape(Bd, 1, -1), sel_new.reshape(Bd, 1, PAGE_SIZE), k_new_b, v_new_b, cache_kt, cache_vt)


def _attend_rows(q, kk, vv, scale):
    s = jnp.sum(kk * q[None], axis=-1, keepdims=True) * scale
    m = jnp.max(s, axis=0, keepdims=True)
    p = jnp.exp(s - m)
    l = jnp.sum(p, axis=0)
    return jnp.sum(p * vv, axis=0) / l


def _mem_prompt_kernel(q_ref, mk_ref, mv_ref, o_ref):
    q = q_ref[0]
    mk = mk_ref[0]
    mv = mv_ref[0]
    outs = []
    for h in range(MEM_HEADS):
        sl = slice(h * MEM_DH, (h + 1) * MEM_DH)
        s = _dot_nt(q[:, sl], mk[:, sl]) * (MEM_DH ** -0.5)
        m = jnp.max(s, axis=1, keepdims=True)
        p = jnp.exp(s - m)
        p = p / jnp.sum(p, axis=1, keepdims=True)
        outs.append(_dot(p, mv[:, sl]))
    o_ref[0] = jnp.concatenate(outs, axis=1)


def _mem_prompt(q, mk, mv, tq):
    B, T, W = q.shape
    M = mk.shape[1]
    kv = pl.BlockSpec((1, M, W), lambda b, i: (b, 0, 0))
    return pl.pallas_call(
        _mem_prompt_kernel,
        grid=(B, T // tq),
        in_specs=[pl.BlockSpec((1, tq, W), lambda b, i: (b, i, 0)), kv, kv],
        out_specs=pl.BlockSpec((1, tq, W), lambda b, i: (b, i, 0)),
        out_shape=jax.ShapeDtypeStruct((B, T, W), F32),
        compiler_params=_cparams("parallel", "parallel"),
        name="mem_prompt",
    )(q, mk, mv)


def _mem_sample_kernel(q_ref, k_ref, v_ref, o_ref, *, ns):
    for i in range(ns):
        o_ref[i] = _attend_rows(q_ref[i], k_ref[i], v_ref[i], MEM_DH ** -0.5)


def _mem_sample(q, ck, cv, ns):
    Bd, M = ck.shape[:2]
    kv = pl.BlockSpec((ns, M, MEM_HEADS, MEM_DH), lambda b: (b, 0, 0, 0))
    qs = pl.BlockSpec((ns, MEM_HEADS, MEM_DH), lambda b: (b, 0, 0))
    return pl.pallas_call(
        functools.partial(_mem_sample_kernel, ns=ns),
        grid=(Bd // ns,),
        in_specs=[qs, kv, kv],
        out_specs=qs,
        out_shape=jax.ShapeDtypeStruct((Bd, MEM_HEADS, MEM_DH), F32),
        compiler_params=_cparams("parallel"),
        name="mem_sample",
    )(q, ck, cv)


def _merge_kernel(x_ref, og_ref, od_ref, om_ref, gn_ref, wg_ref, wb_ref, wo_ref, fn_ref, xo_ref, h2_ref):
    x = x_ref[...]
    d = x.shape[1]
    hb = _rms(x, gn_ref[...]).astype(BF16)
    acc = jnp.zeros_like(x)
    for n, o_ref in enumerate((og_ref, od_ref, om_ref)):
        gate = _sigmoid(jnp.dot(hb, wg_ref[:, n * d:(n + 1) * d], preferred_element_type=F32))
        acc = acc + gate * jnp.dot(_bf(o_ref[...]), wb_ref[n], preferred_element_type=F32)
    xo = x + jnp.dot(_bf(acc), wo_ref[...], preferred_element_type=F32)
    _tt_store(xo_ref, xo)
    _tt_store(h2_ref, _rms(xo, fn_ref[...]))


def _merge(x, o_g, o_d, o_m, gn, wg, wb, wo, fn, tm):
    n, d = x.shape
    tok = lambda w: pl.BlockSpec((tm, w), lambda i: (i, 0))
    full = lambda shape: pl.BlockSpec(shape, lambda i: (0,) * len(shape))
    return pl.pallas_call(
        _merge_kernel,
        grid=(n // tm,),
        in_specs=[tok(d), tok(BRANCH_W), tok(BRANCH_W), tok(BRANCH_W), full((1, d)), full((d, N_BRANCH * d)),
                  full((N_BRANCH, BRANCH_W, d)), full((d, d)), full((1, d))],
        out_specs=[pl.BlockSpec((tm * TT, LANES), lambda i: (i, 0))] * 2,
        out_shape=[jax.ShapeDtypeStruct((n * TT, LANES), F32)] * 2,
        compiler_params=_cparams("parallel"),
        name="merge",
    )(x, o_g, o_d, o_m, gn.reshape(1, d), wg, wb, wo, fn.reshape(1, d))


def _route_kernel(h_ref, wr_ref, br_ref, e_ref, p_ref, r_ref, cnt_ref, run_scr, *, tr):
    i = pl.program_id(0)

    @pl.when(i == 0)
    def _():
        run_scr[...] = jnp.zeros_like(run_scr)

    logits = _dot_nt(wr_ref[...], _tt_load(h_ref, tr)) + br_ref[...]
    eidx = _iota((N_EXPERTS, LANES), 0)
    su = jnp.where(_iota((LANES, LANES), 0) < _iota((LANES, LANES), 1), 1.0, 0.0).astype(BF16)
    run = run_scr[...]
    for c in range(tr // LANES):
        sl = slice(c * LANES, (c + 1) * LANES)
        l = logits[:, sl]
        vals, idxs = [], []
        for _ in range(TOP_K):
            m = jnp.max(l, axis=0, keepdims=True)
            idx = jnp.min(jnp.where(l == m, eidx, N_EXPERTS), axis=0, keepdims=True)
            vals.append(m)
            idxs.append(idx)
            l = jnp.where(eidx == idx, -jnp.inf, l)
        ex = [jnp.exp(v - vals[0]) for v in vals]
        den = ex[0] + ex[1] + ex[2] + ex[3]
        oh = jnp.zeros((N_EXPERTS, LANES), F32)
        for k in range(TOP_K):
            p_ref[k:k + 1, sl] = ex[k] / den
            e_ref[k:k + 1, sl] = idxs[k]
            oh = oh + jnp.where(eidx == idxs[k], 1.0, 0.0)
        before = jnp.dot(oh.astype(BF16), su, preferred_element_type=F32) + run
        for k in range(TOP_K):
            rk = jnp.sum(jnp.where(eidx == idxs[k], before, 0.0), axis=0, keepdims=True)
            r_ref[k:k + 1, sl] = rk.astype(I32)
        run = run + jnp.sum(oh, axis=1, keepdims=True)
    run_scr[...] = run
    cnt_ref[...] = jnp.broadcast_to(run, cnt_ref.shape)


def _route(h2, wr_t, br, tr):
    n = h2.shape[0] // TT
    d = TT * LANES
    tokrow = pl.BlockSpec((TOP_K, tr), lambda i: (0, i))
    return pl.pallas_call(
        functools.partial(_route_kernel, tr=tr),
        grid=(n // tr,),
        in_specs=[pl.BlockSpec((tr * TT, LANES), lambda i: (i, 0)),
                  pl.BlockSpec((N_EXPERTS, d), lambda i: (0, 0)),
                  pl.BlockSpec((N_EXPERTS, 1), lambda i: (0, 0))],
        out_specs=[tokrow, tokrow, tokrow, pl.BlockSpec((N_EXPERTS, LANES), lambda i: (0, 0))],
        out_shape=[jax.ShapeDtypeStruct((TOP_K, n), I32), jax.ShapeDtypeStruct((TOP_K, n), F32),
                   jax.ShapeDtypeStruct((TOP_K, n), I32), jax.ShapeDtypeStruct((N_EXPERTS, LANES), F32)],
        scratch_shapes=[pltpu.VMEM((N_EXPERTS, 1), F32)],
        compiler_params=_cparams("arbitrary"),
        name="moe_route",
    )(h2, wr_t, br)


def _dispatch_kernel(plo_ref, pn_ref, nb_ref, dest_ref, h_ref, xg_hbm, zblk, sem, *, td):
    blk = zblk.shape[0] // TT
    n_blocks = xg_hbm.shape[0] // (blk * TT)
    tile = lambda i: pl.ds(pl.multiple_of(i * TT, TT), TT)

    def issue(t, carry):
        for k in range(TOP_K):
            pltpu.make_async_copy(h_ref.at[tile(t)], xg_hbm.at[tile(dest_ref[k, t])], sem.at[0]).start()
        return carry

    lax.fori_loop(0, td, issue, 0, unroll=8)

    @pl.when(pl.program_id(0) == 0)
    def _():
        zblk[...] = jnp.zeros_like(zblk)

        def zero_block(i):
            rows = pl.ds(pl.multiple_of(i * (blk * TT), blk * TT), blk * TT)
            return pltpu.make_async_copy(zblk, xg_hbm.at[rows], sem.at[2])

        def tail_start(i, c):
            zero_block(i).start()
            return c

        def tail_wait(i, c):
            zero_block(0).wait()
            return c

        lax.fori_loop(nb_ref[0], n_blocks, tail_start, 0)
        lax.fori_loop(nb_ref[0], n_blocks, tail_wait, 0)

        pieces = [1 << i for i in reversed(range(blk.bit_length() - 1))]

        def per_expert(e, carry):
            n = pn_ref[e]
            for wait in (False, True):
                off = plo_ref[e]
                for size in pieces:
                    rows = pl.ds(pl.multiple_of(off * TT, TT), size * TT)
                    copy = pltpu.make_async_copy(zblk.at[pl.ds(0, size * TT)], xg_hbm.at[rows], sem.at[1])

                    @pl.when((n & size) != 0)
                    def _():
                        copy.wait() if wait else copy.start()

                    off = off + (n & size)
            return carry

        lax.fori_loop(0, N_EXPERTS, per_expert, 0)

    for k in range(TOP_K):
        pltpu.make_async_copy(h_ref, xg_hbm.at[pl.ds(0, td * TT)], sem.at[0]).wait()


def _dispatch(h2, dest, pad_lo, pad_n, nb_used, n_slots, blk, td):
    n = h2.shape[0] // TT
    grid_spec = pltpu.PrefetchScalarGridSpec(
        num_scalar_prefetch=3,
        grid=(n // td,),
        in_specs=[pl.BlockSpec((TOP_K, td), lambda i, plo, pn, nb: (0, i), memory_space=pltpu.SMEM),
                  pl.BlockSpec((td * TT, LANES), lambda i, plo, pn, nb: (i, 0))],
        out_specs=pl.BlockSpec(memory_space=pl.ANY),
        scratch_shapes=[pltpu.VMEM((blk * TT, LANES), F32), pltpu.SemaphoreType.DMA((3,))],
    )
    return pl.pallas_call(
        functools.partial(_dispatch_kernel, td=td),
        grid_spec=grid_spec,
        out_shape=jax.ShapeDtypeStruct((n_slots * TT, LANES), F32),
        compiler_params=_cparams("arbitrary"),
        name="moe_dispatch",
    )(pad_lo, pad_n, nb_used, dest, h2)


def _expert_kernel(be_ref, nb_ref, x_ref, wgu_ref, bgu_ref, wdn_ref, bdn_ref, o_ref):
    del be_ref
    f = wdn_ref.shape[1]
    blk = x_ref.shape[0] // TT

    used = pl.program_id(0) < nb_ref[0]

    @pl.when(jnp.logical_not(used))
    def _():
        o_ref[...] = jnp.zeros_like(o_ref)

    @pl.when(used)
    def _():
        gu = jnp.dot(_bf(_tt_load(x_ref, blk)), _bf(wgu_ref[0]), preferred_element_type=F32) + bgu_ref[0]
        gate = jnp.minimum(gu[:, :f], SWIGLU_LIMIT)
        up = jnp.clip(gu[:, f:], -SWIGLU_LIMIT, SWIGLU_LIMIT)
        glu = gate * _sigmoid(SWIGLU_ALPHA * gate)
        _tt_store(o_ref, jnp.dot(_bf((up + 1.0) * glu), _bf(wdn_ref[0]), preferred_element_type=F32) + bdn_ref[0])


def _experts(xg, block_e, nb_used, wgu, bgu, wdn, bdn, blk):
    n_slots = xg.shape[0] // TT
    d = TT * LANES
    f = wdn.shape[1]
    blk_of = lambda i, nb: jnp.minimum(i, nb[0] - 1)
    tok = pl.BlockSpec((blk * TT, LANES), lambda i, be, nb: (blk_of(i, nb), 0))
    per_e = lambda shape: pl.BlockSpec((1,) + shape, lambda i, be, nb: (be[blk_of(i, nb)], 0, 0))
    grid_spec = pltpu.PrefetchScalarGridSpec(
        num_scalar_prefetch=2,
        grid=(n_slots // blk,),
        in_specs=[tok, per_e((d, 2 * f)), per_e((1, 2 * f)), per_e((f, d)), per_e((1, d))],
        out_specs=pl.BlockSpec((blk * TT, LANES), lambda i, be, nb: (i, 0)),
    )
    return pl.pallas_call(
        _expert_kernel,
        grid_spec=grid_spec,
        out_shape=jax.ShapeDtypeStruct((n_slots * TT, LANES), F32),
        compiler_params=_cparams("arbitrary"),
        name="moe_experts",
    )(block_e, nb_used, xg, wgu, bgu, wdn, bdn)


def _combine_kernel(dest_ref, x_ref, p_ref, g_ref, yb_hbm, o_ref, buf, sem, *, tc):
    tile = lambda i: pl.ds(pl.multiple_of(i * TT, TT), TT)

    def issue(t, carry):
        for k in range(TOP_K):
            pltpu.make_async_copy(yb_hbm.at[tile(dest_ref[k, t])], buf.at[k, tile(t)], sem.at[0]).start()
        return carry

    lax.fori_loop(0, tc, issue, 0, unroll=8)
    for k in range(TOP_K):
        pltpu.make_async_copy(yb_hbm.at[pl.ds(0, tc * TT)], buf.at[k], sem.at[0]).wait()
    p = p_ref[...]
    acc = p[:, 0:1] * _tt_load(buf.at[0], tc)
    for k in range(1, TOP_K):
        acc = acc + p[:, k:k + 1] * _tt_load(buf.at[k], tc)
    o_ref[...] = _rms(_tt_load(x_ref, tc) + acc, g_ref[...])


def _combine(x, yb, dest, p_t, g, tc):
    n = x.shape[0] // TT
    d = TT * LANES
    return pl.pallas_call(
        functools.partial(_combine_kernel, tc=tc),
        grid=(n // tc,),
        in_specs=[pl.BlockSpec((TOP_K, tc), lambda i: (0, i), memory_space=pltpu.SMEM),
                  pl.BlockSpec((tc * TT, LANES), lambda i: (i, 0)),
                  pl.BlockSpec((tc, TOP_K), lambda i: (i, 0)),
                  pl.BlockSpec((1, d), lambda i: (0, 0)),
                  pl.BlockSpec(memory_space=pl.ANY)],
        out_specs=pl.BlockSpec((tc, d), lambda i: (i, 0)),
        out_shape=jax.ShapeDtypeStruct((n, d), F32),
        scratch_shapes=[pltpu.VMEM((TOP_K, tc * TT, LANES), F32), pltpu.SemaphoreType.DMA((1,))],
        compiler_params=_cparams("arbitrary"),
        name="moe_combine",
    )(dest, x, p_t, g.reshape(1, d), yb)


def _moe_final(x, h2, g_final, wr_t, br, wgu, bgu, wdn, bdn, blk, tile):
    n = x.shape[0] // TT
    e, p, rank, cnt = _route(h2, wr_t, br, tile)
    counts = cnt[:, 0].astype(I32)
    padded = (counts + blk - 1) // blk * blk
    pad_end = jnp.cumsum(padded)
    pad_start = pad_end - padded
    n_blocks = -(-n * TOP_K // blk) + N_EXPERTS
    expert_ids = jnp.arange(N_EXPERTS, dtype=I32)[:, None, None]
    dest = rank + jnp.sum(jnp.where(e[None] == expert_ids, pad_start[:, None, None], 0), axis=0)
    block_lo = jnp.arange(n_blocks, dtype=I32)[:, None] * blk
    block_e = jnp.minimum(jnp.sum((pad_end[None, :] <= block_lo).astype(I32), axis=1), N_EXPERTS - 1)
    nb_used = (pad_end[-1:] // blk).astype(I32)
    xg = _dispatch(h2, dest, pad_start + counts, padded - counts, nb_used, n_blocks * blk, blk, tile)
    yb = _experts(xg, block_e, nb_used, wgu, bgu, wdn, bdn, blk)
    return _combine(x, yb, dest, p.T, g_final, min(tile, 256))


def _prep_w_in(w):
    cuts = np.cumsum((0,) + _SPLITS)
    seg = lambda i: w[:, int(cuts[i]):int(cuts[i + 1])]
    small = jnp.concatenate([seg(8), seg(9), seg(2), seg(3), jnp.zeros((w.shape[0], LANES - _SM_END), w.dtype)], axis=1)
    w1 = jnp.concatenate([seg(0), seg(1), seg(4), seg(5), seg(6), seg(7), seg(10), small], axis=1)
    return w1.astype(BF16), w[:, int(cuts[-1]):].astype(BF16)


def kernel(x_prompt, x_sample, cache_k, cache_v, cache_k_idx, cache_mem_k, cache_mem_v, state_gdn, state_conv,
           page_table, mem_prompt, norm_attn, w_in, conv_w, gdn_a_log, gdn_dt_bias, gdn_norm, norm_mem, w_mem_kv,
           w_branch, w_out, norm_ffn, w_router, b_router, w_gate_up, b_gate_up, w_down, b_down, norm_final):
    B, T, D = x_prompt.shape
    Bd, Ts, _ = x_sample.shape
    assert Ts == 1 and w_in.shape[0] == 1, "one layer, one new token per sample"
    H = GDN_HEADS
    G = DSA_HEADS // DSA_KV_HEADS
    n_pages = page_table.shape[1]
    M = mem_prompt.shape[1]

    w1, wg = _prep_w_in(w_in[0])
    wb = w_branch[0].astype(BF16)
    wo = w_out[0].astype(BF16)
    wr_t = w_router[0].T
    br = b_router[0].reshape(N_EXPERTS, 1)
    wgu = w_gate_up[0]
    wdn = w_down[0]
    bgu = b_gate_up[0].reshape(N_EXPERTS, 1, -1)
    bdn = b_down[0].reshape(N_EXPERTS, 1, -1)
    alog = gdn_a_log[0]
    dtb = gdn_dt_bias[0]

    xp = x_prompt.reshape(B * T, D)
    g_qkv, g_z, d_k, d_v, i_q, m_q, small, qh, kt, vh = _proj_prompt(xp, norm_attn[0], w1, B, T, min(512, T))

    quad_rows = lambda a: (a.reshape(B, T // GDN_CHUNK, GDN_CHUNK, 2, 4).transpose(0, 1, 3, 4, 2)
                           .reshape(B, T // GDN_CHUNK, 2, 1, QW))
    per_group = lambda a: jnp.repeat(a, GDN_DK).reshape(2, 1, QW)
    o_g, ssm_q = _gdn_prompt(g_qkv.reshape(B, T, GDN_QKV), g_z.reshape(B, T, H * GDN_DV),
                             quad_rows(small[:, _SM_GA:_SM_GA + H]), quad_rows(small[:, _SM_GB:_SM_GB + H]),
                             conv_w[0], per_group(alog), per_group(dtb), jnp.tile(gdn_norm[0], 4).reshape(1, QW),
                             min(512, T))
    o_g = o_g.reshape(B * T, H * GDN_DV)
    ssm_p = ssm_q.reshape(B, 2, GDN_DK, 4, GDN_DV).transpose(0, 1, 3, 2, 4).reshape(B, H, GDN_DK, GDN_DV)

    o_d = _dsa_prompt(i_q.reshape(B, T, -1), small.reshape(B, T, LANES), qh, kt, vh, min(TOPK_MAX, T // 4),
                      min(256, T))
    o_d = o_d.reshape(B * T, DSA_HEADS * DSA_DH)

    mk, mv = _norm_matmul(mem_prompt.reshape(B * M, D), norm_mem[0], w_mem_kv[0].astype(BF16),
                          (MEM_HEADS * MEM_DH,) * 2, min(512, B * M))
    o_m = _mem_prompt(m_q.reshape(B, T, -1), mk.reshape(B, M, -1), mv.reshape(B, M, -1), min(512, T))
    o_m = o_m.reshape(B * T, MEM_HEADS * MEM_DH)

    xres, h2 = _merge(xp, o_g, o_d, o_m, norm_attn[0], wg, wb, wo, norm_ffn[0], min(256, B * T))
    y_prompt = _moe_final(xres, h2, norm_final, wr_t, br, wgu, bgu, wdn, bdn, 256, min(512, B * T))

    k_prompt = d_k.reshape(1, B, T, DSA_KV_HEADS, DSA_DH)
    v_prompt = d_v.reshape(1, B, T, DSA_KV_HEADS, DSA_DH)
    kidx_prompt = small[:, _SM_IK:_SM_IK + IDX_DIM].reshape(1, B, T, IDX_DIM)
    memk_prompt = mk.reshape(1, B, M, MEM_HEADS, MEM_DH)
    memv_prompt = mv.reshape(1, B, M, MEM_HEADS, MEM_DH)
    conv_prompt = g_qkv.reshape(B, T, GDN_QKV)[:, T - (GDN_CONV - 1):, :][None]

    xs = x_sample.reshape(Bd, D)
    s_qkv, s_z, sd_q, sd_k, sd_v, si_q, sm_q, ssmall = _norm_matmul(xs, norm_attn[0], w1, _PROJ_SPLITS, Bd)

    lanes_b = lambda a: jnp.broadcast_to(a[..., None], a.shape + (Bd,))
    og_t, s_t = _gdn_sample(
        s_qkv.T, state_conv[0].transpose(1, 2, 0), lanes_b(conv_w[0]), s_z.T,
        ssmall[:, _SM_GA:_SM_GA + H].T.reshape(H, 1, Bd), ssmall[:, _SM_GB:_SM_GB + H].T.reshape(H, 1, Bd),
        lanes_b(alog.reshape(H, 1)), lanes_b(dtb.reshape(H, 1)), lanes_b(gdn_norm[0]),
        state_gdn[0].transpose(1, 2, 3, 0))
    so_g = og_t.T
    ssm_sample = s_t.transpose(3, 0, 1, 2)[None]
    conv_sample = jnp.concatenate([state_conv[0][:, 1:], s_qkv[:, None, :]], axis=1)[None]

    qi8 = jnp.pad(si_q.reshape(Bd, IDX_HEADS, IDX_DIM), ((0, 0), (0, 8 - IDX_HEADS), (0, 0)))
    wi8 = jnp.pad(ssmall[:, _SM_IW:_SM_IW + IDX_HEADS], ((0, 0), (0, 8 - IDX_HEADS)))[..., None]
    n_pg = 32 if n_pages % 32 == 0 else n_pages
    cache_kit = cache_k_idx[0].transpose(0, 2, 1)
    cache_kt = cache_k[0].transpose(0, 2, 3, 1)
    cache_vt = cache_v[0].transpose(0, 2, 3, 1)
    scores = _dsa_sample_scores(page_table, qi8, wi8, cache_kit)
    scores = scores.reshape(Bd, n_pages * PAGE_SIZE)
    n_sel = min(TOPK_MAX, (n_pages * PAGE_SIZE + 1) // 4)
    sel, sel_new = _dsa_sample_select(scores, si_q, ssmall, n_sel)
    over_lanes = lambda a: jnp.broadcast_to(a[..., None], a.shape + (PAGE_SIZE,))
    head_kv = jnp.arange(DSA_HEADS, dtype=I32) // G
    q8 = jnp.where((head_kv[None, :] == jnp.arange(DSA_KV_HEADS, dtype=I32)[:, None])[None, :, :, None],
                   sd_q.reshape(Bd, 1, DSA_HEADS, DSA_DH), 0.0)
    k_new = sd_k.reshape(Bd, DSA_KV_HEADS, DSA_DH)
    v_new = sd_v.reshape(Bd, DSA_KV_HEADS, DSA_DH)
    so_d = _dsa_sample_attn(page_table, q8, sel, sel_new, over_lanes(k_new), over_lanes(v_new),
                            cache_kt, cache_vt, n_pg).reshape(Bd, DSA_HEADS * DSA_DH)

    so_m = _mem_sample(sm_q.reshape(Bd, MEM_HEADS, MEM_DH), cache_mem_k[0], cache_mem_v[0], 4)
    so_m = so_m.reshape(Bd, MEM_HEADS * MEM_DH)

    sres, sh2 = _merge(xs, so_g, so_d, so_m, norm_attn[0], wg, wb, wo, norm_ffn[0], Bd)
    y_sample = _moe_final(sres, sh2, norm_final, wr_t, br, wgu, bgu, wdn, bdn, 256, Bd)

    return (y_prompt.reshape(B, T, D), y_sample.reshape(Bd, 1, D), k_prompt, v_prompt, kidx_prompt,
            memk_prompt, memv_prompt, ssm_p[None], conv_prompt,
            k_new.reshape(1, Bd, 1, DSA_KV_HEADS, DSA_DH), v_new.reshape(1, Bd, 1, DSA_KV_HEADS, DSA_DH),
            ssmall[:, _SM_IK:_SM_IK + IDX_DIM].reshape(1, Bd, 1, IDX_DIM), ssm_sample, conv_sample)
```

```python
import functools

import numpy as np
import jax
import jax.numpy as jnp
from jax import lax
from jax.experimental import pallas as pl
from jax.experimental.pallas import tpu as pltpu

F32 = jnp.float32
BF16 = jnp.bfloat16
I32 = jnp.int32

EPS = 1e-6
GDN_HEADS = 8
GDN_DK = 64
GDN_DV = 64
GDN_CONV = 4
GDN_CHUNK = 64
GDN_QKV = 2 * GDN_HEADS * GDN_DK + GDN_HEADS * GDN_DV
DSA_HEADS = 8
DSA_KV_HEADS = 4
DSA_DH = 64
IDX_HEADS = 4
IDX_DIM = 64
TOPK_MAX = 256
Q_BLOCK = 128
MEM_HEADS = 4
MEM_DH = 128
N_BRANCH = 3
BRANCH_W = 512
N_EXPERTS = 32
TOP_K = 4
SWIGLU_LIMIT = 7.0
SWIGLU_ALPHA = 1.702
PAGE_SIZE = 128
LANES = 128

_SPLITS = (GDN_QKV, GDN_HEADS * GDN_DV, GDN_HEADS, GDN_HEADS,
           DSA_HEADS * DSA_DH, DSA_KV_HEADS * DSA_DH, DSA_KV_HEADS * DSA_DH,
           IDX_HEADS * IDX_DIM, IDX_DIM, IDX_HEADS,
           MEM_HEADS * MEM_DH)
_PROJ_SPLITS = (GDN_QKV, 512, 512, 256, 256, 256, 512, LANES)
_SM_IK = 0
_SM_IW = IDX_DIM
_SM_GB = _SM_IW + IDX_HEADS
_SM_GA = _SM_GB + GDN_HEADS
_SM_END = _SM_GA + GDN_HEADS

VMEM_LIMIT_BYTES = 56 * 1024 * 1024


def _cparams(*sem):
    return pltpu.CompilerParams(dimension_semantics=sem, vmem_limit_bytes=VMEM_LIMIT_BYTES)


def _bf(x):
    return x.astype(BF16)


def _dot(a, b):
    return jnp.dot(_bf(a), _bf(b), preferred_element_type=F32)


_NT = (((1,), (1,)), ((), ()))


def _dot_nt(a, b):
    return lax.dot_general(_bf(a), _bf(b), _NT, preferred_element_type=F32)


def _split2(x):
    hi = x.astype(BF16)
    lo = (x - hi.astype(F32)).astype(BF16)
    return hi, lo


def _split3(x):
    hi = x.astype(BF16)
    r = x - hi.astype(F32)
    mid = r.astype(BF16)
    lo = (r - mid.astype(F32)).astype(BF16)
    return hi, mid, lo


def _dot_exact01(a, b01):
    r = a.shape[0]
    parts = jnp.dot(jnp.concatenate(_split3(a), axis=0), b01, preferred_element_type=F32)
    return parts[:r] + (parts[r:2 * r] + parts[2 * r:])


def _rms(x, g):
    return x * lax.rsqrt(jnp.mean(x * x, axis=-1, keepdims=True) + EPS) * g


def _sigmoid(x):
    return 1.0 / (1.0 + jnp.exp(-x))


def _silu(x):
    return x * _sigmoid(x)


def _softplus(x):
    return jnp.maximum(x, 0.0) + jnp.log(1.0 + jnp.exp(-jnp.abs(x)))


def _iota(shape, axis):
    return lax.broadcasted_iota(I32, shape, axis)


TT = 8


def _tt_load(ref, n):
    return jnp.concatenate([ref[pl.ds(c, n, stride=TT), :] for c in range(TT)], axis=1)


def _tt_store(ref, val):
    n = val.shape[0]
    for c in range(TT):
        ref[pl.ds(c, n, stride=TT), :] = val[:, c * LANES:(c + 1) * LANES]


def _norm_matmul_kernel(x_ref, g_ref, w_ref, *o_refs, splits):
    hb = _rms(x_ref[...], g_ref[...]).astype(BF16)
    off = 0
    for o_ref, n in zip(o_refs, splits):
        o_ref[...] = jnp.dot(hb, w_ref[:, off:off + n], preferred_element_type=F32)
        off += n


def _norm_matmul(x, g, w, splits, tm):
    n, d = x.shape
    return pl.pallas_call(
        functools.partial(_norm_matmul_kernel, splits=splits),
        grid=(n // tm,),
        in_specs=[pl.BlockSpec((tm, d), lambda i: (i, 0)),
                  pl.BlockSpec((1, d), lambda i: (0, 0)),
                  pl.BlockSpec((d, sum(splits)), lambda i: (0, 0))],
        out_specs=[pl.BlockSpec((tm, s), lambda i: (i, 0)) for s in splits],
        out_shape=[jax.ShapeDtypeStruct((n, s), F32) for s in splits],
        compiler_params=_cparams("parallel"),
        name="norm_matmul",
    )(x, g.reshape(1, d), w)


def _proj_prompt_kernel(x_ref, g_ref, w_ref, qkv_ref, z_ref, dk_ref, dv_ref, iq_ref, mq_ref, sm_ref,
                        qh_ref, kt_ref, vh_ref):
    hb = _rms(x_ref[...], g_ref[...]).astype(BF16)
    offs = np.cumsum((0,) + _PROJ_SPLITS)
    part = lambda i: jnp.dot(hb, w_ref[:, int(offs[i]):int(offs[i + 1])], preferred_element_type=F32)
    qkv_ref[...] = part(0)
    z_ref[...] = part(1)
    dq = part(2)
    for h in range(DSA_HEADS):
        qh_ref[0, h] = dq[:, h * DSA_DH:(h + 1) * DSA_DH]
    dk = part(3)
    dk_ref[...] = dk
    kt_ref[0] = dk.T.reshape(DSA_KV_HEADS, DSA_DH, dk.shape[0])
    dv = part(4)
    dv_ref[...] = dv
    for kv in range(DSA_KV_HEADS):
        vh_ref[0, kv] = dv[:, kv * DSA_DH:(kv + 1) * DSA_DH]
    iq_ref[...] = part(5)
    mq_ref[...] = part(6)
    sm_ref[...] = part(7)


def _proj_prompt(x, g, w, B, T, tm):
    n, d = x.shape
    tpb = T // tm
    tok = lambda width: pl.BlockSpec((tm, width), lambda b, t: (b * tpb + t, 0))
    flat = lambda width: jax.ShapeDtypeStruct((n, width), F32)
    sp = _PROJ_SPLITS
    return pl.pallas_call(
        _proj_prompt_kernel,
        grid=(B, tpb),
        in_specs=[tok(d), pl.BlockSpec((1, d), lambda b, t: (0, 0)), pl.BlockSpec((d, sum(sp)), lambda b, t: (0, 0))],
        out_specs=[tok(sp[0]), tok(sp[1]), tok(sp[3]), tok(sp[4]), tok(sp[5]), tok(sp[6]), tok(sp[7]),
                   pl.BlockSpec((1, DSA_HEADS, tm, DSA_DH), lambda b, t: (b, 0, t, 0)),
                   pl.BlockSpec((1, DSA_KV_HEADS, DSA_DH, tm), lambda b, t: (b, 0, 0, t)),
                   pl.BlockSpec((1, DSA_KV_HEADS, tm, DSA_DH), lambda b, t: (b, 0, t, 0))],
        out_shape=[flat(sp[0]), flat(sp[1]), flat(sp[3]), flat(sp[4]), flat(sp[5]), flat(sp[6]), flat(sp[7]),
                   jax.ShapeDtypeStruct((B, DSA_HEADS, T, DSA_DH), F32),
                   jax.ShapeDtypeStruct((B, DSA_KV_HEADS, DSA_DH, T), F32),
                   jax.ShapeDtypeStruct((B, DSA_KV_HEADS, T, DSA_DH), F32)],
        compiler_params=_cparams("parallel", "parallel"),
        name="proj_prompt",
    )(x, g.reshape(1, d), w)


QW = 4 * GDN_DK


def _bd_stack(x):
    lane_head = (_iota(x.shape, 1) >> 6) & 3
    return jnp.concatenate([jnp.where(lane_head == h, x, 0.0) for h in range(4)], axis=0)


def _bd_dot(a, x):
    return jnp.dot(_bf(a), _bf(_bd_stack(x)), preferred_element_type=F32)


def _bd_split(x):
    xh = x.astype(BF16).astype(F32)
    return _bf(_bd_stack(xh)), _bf(_bd_stack(x - xh))


def _bd_dot3(lhs, xb):
    bh, bl = xb
    r = lhs[0].shape[0]
    parts = [_split2(a) for a in lhs]
    hi = jnp.dot(jnp.concatenate([p for hl in parts for p in hl], axis=0), bh, preferred_element_type=F32)
    lo = jnp.dot(jnp.concatenate([hl[0] for hl in parts], axis=0), bl, preferred_element_type=F32)
    return [hi[2 * i * r:(2 * i + 1) * r] + (lo[i * r:(i + 1) * r] + hi[(2 * i + 1) * r:(2 * i + 2) * r])
            for i in range(len(lhs))]


def _gdn_prompt_kernel(q_ref, k_ref, v_ref, qh_ref, kh_ref, vh_ref, z_ref, a_ref, b_ref,
                       cwq_ref, cwk_ref, cwv_ref, alog_ref, dtb_ref, nw_ref, o_ref, s_ref,
                       xq, xk, xv, s_scr, *, tt):
    ti = pl.program_id(1)
    C = GDN_CHUNK
    HALO = 8
    NG = GDN_HEADS // 4
    CPI = 2

    @pl.when(ti == 0)
    def _():
        s_scr[...] = jnp.zeros_like(s_scr)

    r = _iota((QW, QW), 0)
    c = _iota((QW, QW), 1)
    same = (r >> 6) == (c >> 6)
    ones_bd = jnp.where(same, 1.0, 0.0).astype(BF16)
    su_bd = jnp.where(same, jnp.where((r & 63) > (c & 63), 1.0, 0.0), 0.0).astype(BF16)
    rhs01 = jnp.concatenate([su_bd, ones_bd], axis=1)
    ident = jnp.where(r == c, 1.0, 0.0).astype(BF16)

    keep = (ti > 0).astype(F32)
    for idx, (src, halo, dst, cw) in enumerate(((q_ref, qh_ref, xq, cwq_ref), (k_ref, kh_ref, xk, cwk_ref),
                                                (v_ref, vh_ref, xv, cwv_ref))):
        dst[0:HALO, :] = halo[0] * keep
        dst[HALO:HALO + tt, :] = src[0]
        w = cw[...]
        y = w[0:1, :] * dst[pl.ds(HALO - 3, tt), :]
        for i in range(1, GDN_CONV):
            y = y + w[i:i + 1, :] * dst[pl.ds(HALO - 3 + i, tt), :]
        y = _silu(y)
        if idx < 2:
            ss = jnp.concatenate([_dot_exact01((y * y)[:, g * QW:(g + 1) * QW], ones_bd) for g in range(NG)], axis=1)
            y = y * lax.rsqrt(ss + EPS)
            if idx == 0:
                y = y * (GDN_DK ** -0.5)
        dst[HALO:HALO + tt, :] = y

    ri = _iota((C, QW), 0)
    li = _iota((C, QW), 1) & 63
    lane_head = _iota((C, QW), 1) >> 6
    incl = li <= ri
    strict = li < ri
    eye = li == ri
    nw = nw_ref[...]

    def solve_steps(cis, out):
        streams = [(ci, g) for ci in cis for g in range(NG)]
        n = range(len(streams))
        rows = [slice(HALO + ci * C, HALO + (ci + 1) * C) for ci, _ in streams]
        cols = [slice(g * QW, (g + 1) * QW) for _, g in streams]
        qc = [xq[rows[i], cols[i]] for i in n]
        kc = [xk[rows[i], cols[i]] for i in n]
        vc = [xv[rows[i], cols[i]] for i in n]
        g_row = [-jnp.exp(alog_ref[g]) * _softplus(a_ref[0, ci, g] + dtb_ref[g]) for ci, g in streams]
        b_row = [_sigmoid(b_ref[0, ci, g]) for ci, g in streams]
        res = [_dot_exact01(jnp.concatenate([jnp.where(incl, g_row[i], 0.0), jnp.where(eye, b_row[i], 0.0)], axis=0),
                            rhs01) for i in n]
        yield
        gcol = [res[i][:C, QW:] for i in n]
        bcol = [res[i][C:, QW:] for i in n]
        gamma = [jnp.where(incl, jnp.exp(res[i][:C, :QW]), 0.0) for i in n]
        eg = [jnp.exp(gcol[i]) for i in n]
        glast = [gcol[i][C - 1:C, :] for i in n]
        kb = [kc[i] * bcol[i] for i in n]
        kt = [lax.dot_general(ident, jnp.concatenate([_bf(kc[i])] * 4, axis=0), _NT, preferred_element_type=F32)
              for i in n]
        yield
        kt = [_bf(jnp.where(same, kt[i], 0.0)) for i in n]
        kk = [jnp.dot(_bf(jnp.concatenate([kb[i], qc[i]], axis=0)), kt[i], preferred_element_type=F32)
              for i in n]
        yield
        r = [jnp.where(strict, -(kk[i][:C] * gamma[i]), 0.0) for i in n]
        t = [jnp.where(eye, 1.0, r[i]) for i in n]
        for k in range(6):
            rb = [_bd_split(r[i]) for i in n]
            prod = [_bd_dot3(([t[i]] if k > 0 else []) + ([r[i]] if k < 5 else []), rb[i]) for i in n]
            if k > 0:
                t = [t[i] + prod[i][0] for i in n]
            if k < 5:
                r = [prod[i][-1] for i in n]
            yield
        rhs = [_bd_split(jnp.concatenate([vc[i] * bcol[i], kb[i] * eg[i]], axis=1)) for i in n]
        y = [_bd_dot3([t[i]], rhs[i])[0] for i in n]
        out.extend(dict(ci=streams[i][0], g=streams[i][1], y=y[i], qg=qc[i] * eg[i], qk=kk[i][C:] * gamma[i],
                        kd=kc[i] * jnp.exp(glast[i] - gcol[i]), dec=jnp.exp(glast[i])) for i in n)

    def scan_steps(st):
        n = range(len(st))
        s = [s_scr[t["g"]] for t in st]
        ws = [_bd_dot(jnp.concatenate([st[i]["y"][:, QW:], st[i]["qg"]], axis=0), s[i]) for i in n]
        yield
        vn = [st[i]["y"][:, :QW] - ws[i][:C] for i in n]
        o = [ws[i][C:] + _bd_dot(st[i]["qk"], vn[i]) for i in n]
        kdt = [lax.dot_general(ident, _bf(st[i]["kd"]), _NT, preferred_element_type=F32) for i in n]
        yield
        full = [jnp.dot(_bf(kdt[i]), _bf(vn[i]), preferred_element_type=F32) for i in n]
        ms = [_dot_exact01(o[i] * o[i], ones_bd) * (1.0 / GDN_DV) for i in n]
        yield
        for i in n:
            g = st[i]["g"]
            sadd = jnp.where(lane_head == 0, full[i][0:C], 0.0)
            for h in range(1, 4):
                sadd = sadd + jnp.where(lane_head == h, full[i][h * C:(h + 1) * C], 0.0)
            s_scr[g] = s[i] * st[i]["dec"] + sadd
            rows = slice(st[i]["ci"] * C, (st[i]["ci"] + 1) * C)
            cols = slice(g * QW, (g + 1) * QW)
            o_ref[0, rows, cols] = o[i] * lax.rsqrt(ms[i] + EPS) * nw * _silu(z_ref[0, rows, cols])

    def scan_chunks(st):
        for j in range(len(st) // NG):
            yield from scan_steps(st[j * NG:(j + 1) * NG])

    def run(*gens):
        live = list(gens)
        while live:
            for gen in list(live):
                try:
                    next(gen)
                except StopIteration:
                    live.remove(gen)

    n_groups = tt // (CPI * C)
    solved = [[] for _ in range(n_groups)]
    run(solve_steps(range(CPI), solved[0]))
    for gi in range(1, n_groups):
        run(solve_steps(range(gi * CPI, (gi + 1) * CPI), solved[gi]), scan_chunks(solved[gi - 1]))
    run(scan_chunks(solved[-1]))
    s_ref[0] = s_scr[...]


def _gdn_prompt(qkv, z, a, b, cw, alog, dtb, nw, tt):
    B, T, _ = qkv.shape
    nct = tt // GDN_CHUNK
    NG = GDN_HEADS // 4
    W = NG * QW
    col = lambda j: pl.BlockSpec((1, tt, W), lambda bi, ti: (bi, ti, j))
    halo = lambda j: pl.BlockSpec((1, 8, W), lambda bi, ti: (bi, jnp.maximum(ti * (tt // 8) - 1, 0), j))
    chunked = pl.BlockSpec((1, nct, NG, 1, QW), lambda bi, ti: (bi, ti, 0, 0, 0))
    cwcol = lambda j: pl.BlockSpec((GDN_CONV, W), lambda bi, ti: (0, j))
    per_group = pl.BlockSpec((NG, 1, QW), lambda bi, ti: (0, 0, 0))
    return pl.pallas_call(
        functools.partial(_gdn_prompt_kernel, tt=tt),
        grid=(B, T // tt),
        in_specs=[col(0), col(1), col(2), halo(0), halo(1), halo(2), col(0), chunked, chunked,
                  cwcol(0), cwcol(1), cwcol(2), per_group, per_group,
                  pl.BlockSpec((1, QW), lambda bi, ti: (0, 0))],
        out_specs=[col(0), pl.BlockSpec((1, NG, GDN_DK, QW), lambda bi, ti: (bi, 0, 0, 0))],
        out_shape=[jax.ShapeDtypeStruct((B, T, W), F32), jax.ShapeDtypeStruct((B, NG, GDN_DK, QW), F32)],
        scratch_shapes=[pltpu.VMEM((8 + tt, W), F32)] * 3 + [pltpu.VMEM((NG, GDN_DK, QW), F32)],
        compiler_params=_cparams("parallel", "arbitrary"),
        name="gdn_prompt",
    )(qkv, qkv, qkv, qkv, qkv, qkv, z, a, b, cw, cw, cw, alog, dtb, nw)


def _gdn_sample_kernel(q_ref, k_ref, v_ref, cq_ref, ck_ref, cv_ref, wq_ref, wk_ref, wv_ref,
                       z_ref, a_ref, b_ref, alog_ref, dtb_ref, nw_ref, s_ref, o_ref, so_ref):
    def conv(x_ref, c_ref, w_ref):
        y = w_ref[GDN_CONV - 1] * x_ref[...]
        for i in range(GDN_CONV - 1):
            y = y + w_ref[i] * c_ref[i]
        return _silu(y)

    q = conv(q_ref, cq_ref, wq_ref)
    q = q * lax.rsqrt(jnp.sum(q * q, axis=0, keepdims=True) + EPS) * (GDN_DK ** -0.5)
    k = conv(k_ref, ck_ref, wk_ref)
    k = k * lax.rsqrt(jnp.sum(k * k, axis=0, keepdims=True) + EPS)
    v = conv(v_ref, cv_ref, wv_ref)
    beta = _sigmoid(b_ref[0])
    g = -jnp.exp(alog_ref[0]) * _softplus(a_ref[0] + dtb_ref[0])
    dec = jnp.exp(g)
    ks = jnp.zeros_like(v)
    for i in range(GDN_DK):
        ks = ks + k[i:i + 1, :] * (s_ref[0, i] * dec)
    vn = beta * (v - ks)
    o = jnp.zeros_like(v)
    for i in range(GDN_DK):
        sn = s_ref[0, i] * dec + k[i:i + 1, :] * vn
        so_ref[0, i] = sn
        o = o + q[i:i + 1, :] * sn
    on = o * lax.rsqrt(jnp.mean(o * o, axis=0, keepdims=True) + EPS) * nw_ref[...]
    o_ref[...] = on * _silu(z_ref[...])


def _gdn_sample(qkv_t, conv_t, cw_t, z_t, a_t, b_t, alog_t, dtb_t, nw_t, s_t):
    nb = qkv_t.shape[1]
    H, DK, DV = GDN_HEADS, GDN_DK, GDN_DV
    row = lambda off: pl.BlockSpec((DK, nb), lambda h: (off + h, 0))
    crow = lambda off: pl.BlockSpec((GDN_CONV - 1, DK, nb), lambda h: (0, off + h, 0))
    wrow = lambda off: pl.BlockSpec((GDN_CONV, DK, nb), lambda h: (0, off + h, 0))
    per_head = pl.BlockSpec((1, 1, nb), lambda h: (h, 0, 0))
    return pl.pallas_call(
        _gdn_sample_kernel,
        grid=(H,),
        in_specs=[row(0), row(H), row(2 * H), crow(0), crow(H), crow(2 * H), wrow(0), wrow(H), wrow(2 * H),
                  row(0), per_head, per_head, per_head, per_head,
                  pl.BlockSpec((DV, nb), lambda h: (0, 0)),
                  pl.BlockSpec((1, DK, DV, nb), lambda h: (h, 0, 0, 0))],
        out_specs=[row(0), pl.BlockSpec((1, DK, DV, nb), lambda h: (h, 0, 0, 0))],
        out_shape=[jax.ShapeDtypeStruct((H * DV, nb), F32), jax.ShapeDtypeStruct((H, DK, DV, nb), F32)],
        compiler_params=_cparams("parallel"),
        name="gdn_sample",
    )(qkv_t, qkv_t, qkv_t, conv_t, conv_t, conv_t, cw_t, cw_t, cw_t, z_t, a_t, b_t, alog_t, dtb_t, nw_t, s_t)


def _count_ge(x, thr):
    return jnp.sum(jnp.where(x >= thr, 1.0, 0.0), axis=1, keepdims=True)


def _topk_mask(score, valid, k):
    rows, width = score.shape
    kf = float(k)
    x = jnp.where(valid, score, -jnp.inf)
    validf = jnp.where(valid, 1.0, 0.0)
    nvalid = jnp.sum(validf, axis=1, keepdims=True)
    few = nvalid <= kf
    mx = jnp.max(x, axis=1, keepdims=True)
    mn = jnp.min(jnp.where(valid, score, jnp.inf), axis=1, keepdims=True)
    lo0 = mn
    hi0 = mx + (jnp.abs(mx) * 1e-6 + 1e-30)
    chi0 = jnp.zeros_like(mx)
    clo0 = nvalid

    def step(_, carry):
        lo, hi, chi, clo = carry
        mid = 0.5 * lo + 0.5 * hi
        c = _count_ge(x, mid)
        ge = c >= kf
        return jnp.where(ge, mid, lo), jnp.where(ge, hi, mid), jnp.where(ge, chi, c), jnp.where(ge, c, clo)

    def finish(hi, chi):
        tau = jnp.max(jnp.where(x < hi, x, -jnp.inf), axis=1, keepdims=True)
        ceq = jnp.sum(jnp.where(x == tau, 1.0, 0.0), axis=1, keepdims=True)
        return tau, ceq

    def not_done(hi, chi, clo):
        tau, ceq = finish(hi, chi)
        bad = jnp.where(few | (clo == kf) | (chi + ceq >= kf), 0.0, 1.0)
        return jnp.max(bad) > 0.0

    def exact_cond(carry):
        it, _, _, _, clo = carry
        open_rows = jnp.max(jnp.where(few | (clo == kf), 0.0, 1.0)) > 0.0
        return jnp.logical_and(it < 6, open_rows)

    def rounds(nsteps):
        def body(carry):
            it, lo, hi, chi, clo = carry
            lo, hi, chi, clo = lax.fori_loop(0, nsteps, step, (lo, hi, chi, clo))
            return it + 1, lo, hi, chi, clo
        return body

    _, lo, hi, chi, clo = lax.while_loop(exact_cond, rounds(4), (jnp.int32(0), lo0, hi0, chi0, clo0))

    def w_cond(carry):
        it, _, hi, chi, clo = carry
        open_rows = jnp.max(jnp.where(few | (clo == kf), 0.0, 1.0)) > 0.0
        return jnp.logical_and(open_rows, jnp.logical_and(it < 48, not_done(hi, chi, clo)))

    _, lo, hi, chi, clo = lax.while_loop(w_cond, rounds(8), (jnp.int32(0), lo, hi, chi, clo))
    exact = clo == kf
    tau, ceq = finish(hi, chi)
    tau = jnp.where(exact, lo, tau)
    need = jnp.where(exact, 0.0, kf - chi)
    gtf = jnp.where(x >= jnp.where(exact, lo, hi), 1.0, 0.0)
    eqf = jnp.where(jnp.logical_and(x == tau, jnp.logical_not(exact)), 1.0, 0.0)
    has_tie = jnp.max(jnp.where(few | exact | (ceq <= need), 0.0, 1.0)) > 0.0

    def tie_path(_):
        nchunk = width // LANES
        su = jnp.where(_iota((LANES, LANES), 0) < _iota((LANES, LANES), 1), 1.0, 0.0).astype(BF16)
        run = jnp.zeros_like(need)
        pieces = []
        for c in range(nchunk):
            e = eqf[:, c * LANES:(c + 1) * LANES]
            before = jnp.dot(e.astype(BF16), su, preferred_element_type=F32) + run
            pieces.append(jnp.where(before < need, e, 0.0))
            run = run + jnp.sum(e, axis=1, keepdims=True)
        return jnp.concatenate(pieces, axis=1)

    sel_eq = lax.cond(has_tie, tie_path, lambda _: eqf, 0)
    return jnp.where(few, validf, gtf + sel_eq)


def _dsa_prompt_kernel(qi_ref, smq_ref, sma_ref, qh_ref, kt_ref, v_ref, o_ref, *, n_sel, j0, qb):
    j = j0 + pl.program_id(1)
    S = kt_ref.shape[-1]
    qi = qi_ref[0]
    wi = smq_ref[0][:, _SM_IW:_SM_IW + IDX_HEADS] * (IDX_HEADS ** -0.5 * IDX_DIM ** -0.5)
    ki = sma_ref[0][:, _SM_IK:_SM_IK + IDX_DIM]
    score = jnp.zeros((qb, S), F32)
    for h in range(IDX_HEADS):
        s = _dot_nt(qi[:, h * IDX_DIM:(h + 1) * IDX_DIM], ki)
        score = score + jnp.maximum(s, 0.0) * wi[:, h:h + 1]
    tpos = j * qb + _iota((qb, 1), 0)
    valid = _iota((1, S), 1) <= tpos
    mask = _topk_mask(score, valid, n_sel)
    bias = jnp.where(mask > 0.0, 0.0, -jnp.inf)
    bias2 = jnp.concatenate([bias, bias], axis=0)
    G = DSA_HEADS // DSA_KV_HEADS
    outs = []
    for kv in range(DSA_KV_HEADS):
        q2 = qh_ref[0, G * kv:G * (kv + 1)].reshape(G * qb, DSA_DH) * (DSA_DH ** -0.5)
        s = _dot(q2, kt_ref[0, kv]) + bias2
        p = jnp.exp(s - jnp.max(s, axis=1, keepdims=True))
        o = _dot(p, v_ref[0, kv]) / jnp.sum(p, axis=1, keepdims=True)
        outs += [o[g * qb:(g + 1) * qb] for g in range(G)]
    o_ref[0] = jnp.concatenate(outs, axis=1)


def _dsa_prompt(qi, small, qh, kt, vh, n_sel, qb):
    B, T, _ = qi.shape
    nq = T // qb
    nseg = 8 if nq % 8 == 0 else (4 if nq % 4 == 0 else 1)
    qps = nq // nseg
    outs = []
    for seg in range(nseg):
        j0 = seg * qps
        S = (seg + 1) * qps * qb
        outs.append(pl.pallas_call(
            functools.partial(_dsa_prompt_kernel, n_sel=n_sel, j0=j0, qb=qb),
            grid=(B, qps),
            in_specs=[pl.BlockSpec((1, qb, IDX_HEADS * IDX_DIM), lambda b, j, j0=j0: (b, j0 + j, 0)),
                      pl.BlockSpec((1, qb, LANES), lambda b, j, j0=j0: (b, j0 + j, 0)),
                      pl.BlockSpec((1, S, LANES), lambda b, j: (b, 0, 0)),
                      pl.BlockSpec((1, DSA_HEADS, qb, DSA_DH), lambda b, j, j0=j0: (b, 0, j0 + j, 0)),
                      pl.BlockSpec((1, DSA_KV_HEADS, DSA_DH, S), lambda b, j: (b, 0, 0, 0)),
                      pl.BlockSpec((1, DSA_KV_HEADS, S, DSA_DH), lambda b, j: (b, 0, 0, 0))],
            out_specs=pl.BlockSpec((1, qb, DSA_HEADS * DSA_DH), lambda b, j: (b, j, 0)),
            out_shape=jax.ShapeDtypeStruct((B, qps * qb, DSA_HEADS * DSA_DH), F32),
            compiler_params=_cparams("parallel", "arbitrary"),
            name="dsa_prompt",
        )(qi, small, small, qh, kt, vh))
    return outs[0] if nseg == 1 else jnp.concatenate(outs, axis=1)


def _dsa_sample_score_kernel(pt_ref, qi_ref, wi_ref, kit_hbm, o_ref, buf, sem, *, n_pages):
    b = pl.program_id(0)

    def fetch(sample, slot):
        def issue(p, carry):
            pltpu.make_async_copy(kit_hbm.at[pt_ref[sample, p]], buf.at[slot, p], sem.at[slot]).start()
            return carry

        lax.fori_loop(0, n_pages, issue, 0, unroll=8)

    @pl.when(b == 0)
    def _():
        fetch(0, 0)

    @pl.when(b + 1 < pl.num_programs(0))
    def _():
        fetch(b + 1, (b + 1) % 2)

    slot = b % 2
    pltpu.make_async_copy(kit_hbm.at[pl.ds(0, n_pages)], buf.at[slot], sem.at[slot]).wait()
    qi = qi_ref[0]
    wi = wi_ref[0] * (IDX_HEADS ** -0.5 * IDX_DIM ** -0.5)
    for p in range(n_pages):
        s = _dot(qi, buf[slot, p])
        o_ref[0, :, p * PAGE_SIZE:(p + 1) * PAGE_SIZE] = jnp.sum(jnp.maximum(s, 0.0) * wi, axis=0, keepdims=True)


def _dsa_sample_scores(page_table, qi8, wi8, cache_kit):
    Bd, n_pages = page_table.shape
    grid_spec = pltpu.PrefetchScalarGridSpec(
        num_scalar_prefetch=1,
        grid=(Bd,),
        in_specs=[pl.BlockSpec((1, 8, IDX_DIM), lambda b, pt: (b, 0, 0)),
                  pl.BlockSpec((1, 8, 1), lambda b, pt: (b, 0, 0)),
                  pl.BlockSpec(memory_space=pl.ANY)],
        out_specs=pl.BlockSpec((1, 1, n_pages * PAGE_SIZE), lambda b, pt: (b, 0, 0)),
        scratch_shapes=[pltpu.VMEM((2, n_pages, IDX_DIM, PAGE_SIZE), F32), pltpu.SemaphoreType.DMA((2,))],
    )
    return pl.pallas_call(
        functools.partial(_dsa_sample_score_kernel, n_pages=n_pages),
        grid_spec=grid_spec,
        out_shape=jax.ShapeDtypeStruct((Bd, 1, n_pages * PAGE_SIZE), F32),
        compiler_params=_cparams("arbitrary"),
        name="dsa_sample_scores",
    )(page_table, qi8, wi8, cache_kit)


def _dsa_sample_select_kernel(sc_ref, qi_ref, sm_ref, sel_ref, seln_ref, *, n_sel):
    Bd, past = sc_ref.shape
    width = past + LANES
    sm = sm_ref[...]
    qi = _bf(qi_ref[...]).astype(F32)
    ki = _bf(sm[:, _SM_IK:_SM_IK + IDX_DIM]).astype(F32)
    wi = sm[:, _SM_IW:_SM_IW + IDX_HEADS] * (IDX_HEADS ** -0.5 * IDX_DIM ** -0.5)
    snew = jnp.zeros((Bd, 1), F32)
    for h in range(IDX_HEADS):
        s = jnp.sum(qi[:, h * IDX_DIM:(h + 1) * IDX_DIM] * ki, axis=1, keepdims=True)
        snew = snew + jnp.maximum(s, 0.0) * wi[:, h:h + 1]
    tail = jnp.where(_iota((Bd, LANES), 1) == 0, snew, -jnp.inf)
    x = jnp.concatenate([sc_ref[...], tail], axis=1)
    valid = jnp.broadcast_to(_iota((1, width), 1) <= past, (Bd, width))
    mask = _topk_mask(x, valid, n_sel)
    sel_ref[...] = mask[:, :past]
    seln_ref[...] = mask[:, past:]


def _dsa_sample_select(scores, qi, small, n_sel):
    Bd, past = scores.shape
    return pl.pallas_call(
        functools.partial(_dsa_sample_select_kernel, n_sel=n_sel),
        out_shape=[jax.ShapeDtypeStruct((Bd, past), F32), jax.ShapeDtypeStruct((Bd, LANES), F32)],
        compiler_params=pltpu.CompilerParams(vmem_limit_bytes=VMEM_LIMIT_BYTES),
        name="dsa_sample_select",
    )(scores, qi, small)


_MASKED = -1e30


def _dsa_sample_attn_kernel(pt_ref, q_ref, sel_ref, seln_ref, kn_ref, vn_ref, kt_hbm, vt_hbm, o_ref,
                            kbuf, vbuf, sem, m_scr, l_scr, acc_scr, *, n_pg):
    jj = pl.program_id(1)
    njj = pl.num_programs(1)
    step = pl.program_id(0) * njj + jj
    G = DSA_HEADS // DSA_KV_HEADS
    row_kv = _iota((DSA_HEADS, DSA_DH), 0) // G

    def fetch(st, slot):
        sample = st // njj
        first = (st % njj) * n_pg

        def issue(p, carry):
            page = pt_ref[sample, first + p]
            pltpu.make_async_copy(kt_hbm.at[page], kbuf.at[slot, p], sem.at[0, slot]).start()
            pltpu.make_async_copy(vt_hbm.at[page], vbuf.at[slot, p], sem.at[1, slot]).start()
            return carry

        lax.fori_loop(0, n_pg, issue, 0, unroll=8)

    @pl.when(step == 0)
    def _():
        fetch(0, 0)

    @pl.when(step + 1 < pl.num_programs(0) * njj)
    def _():
        fetch(step + 1, (step + 1) % 2)

    slot = step % 2
    pltpu.make_async_copy(kt_hbm.at[pl.ds(0, n_pg)], kbuf.at[slot], sem.at[0, slot]).wait()
    pltpu.make_async_copy(vt_hbm.at[pl.ds(0, n_pg)], vbuf.at[slot], sem.at[1, slot]).wait()

    @pl.when(jj == 0)
    def _():
        m_scr[...] = jnp.full_like(m_scr, _MASKED)
        l_scr[...] = jnp.zeros_like(l_scr)
        acc_scr[...] = jnp.zeros_like(acc_scr)

    def block(kts, vts, msk):
        s = jnp.dot(_bf(q_ref[0, 0]), kts[0], preferred_element_type=F32)
        for kv in range(1, DSA_KV_HEADS):
            s = s + jnp.dot(_bf(q_ref[0, kv]), kts[kv], preferred_element_type=F32)
        s = s * (DSA_DH ** -0.5)
        on = msk > 0.0
        m_old = m_scr[...]
        m_new = jnp.maximum(m_old, jnp.max(jnp.where(on, s, _MASKED), axis=1, keepdims=True))
        alpha = jnp.exp(m_old - m_new)
        p = jnp.where(on, jnp.exp(s - m_new), 0.0)
        l_scr[...] = l_scr[...] * alpha + jnp.sum(p, axis=1, keepdims=True)
        pb = _bf(p)
        acc = acc_scr[...] * alpha
        for kv in range(DSA_KV_HEADS):
            o_kv = lax.dot_general(pb, vts[kv], _NT, preferred_element_type=F32)
            acc = acc + jnp.where(row_kv == kv, o_kv, 0.0)
        acc_scr[...] = acc
        m_scr[...] = m_new

    gather = lambda pages, kv: jnp.concatenate([_bf(pages[slot, p, kv]) for p in range(n_pg)], axis=1)
    block([gather(kbuf, kv) for kv in range(DSA_KV_HEADS)], [gather(vbuf, kv) for kv in range(DSA_KV_HEADS)],
          sel_ref[0])

    @pl.when(jj == njj - 1)
    def _():
        block([_bf(kn_ref[0, kv]) for kv in range(DSA_KV_HEADS)], [_bf(vn_ref[0, kv]) for kv in range(DSA_KV_HEADS)],
              seln_ref[0])
        o_ref[0] = acc_scr[...] / l_scr[...]


def _dsa_sample_attn(page_table, q8, sel, sel_new, k_new_b, v_new_b, cache_kt, cache_vt, n_pg):
    Bd, n_pages = page_table.shape
    new_tok = pl.BlockSpec((1, DSA_KV_HEADS, DSA_DH, PAGE_SIZE), lambda b, jj, pt: (b, 0, 0, 0))
    pages = pltpu.VMEM((2, n_pg, DSA_KV_HEADS, DSA_DH, PAGE_SIZE), F32)
    grid_spec = pltpu.PrefetchScalarGridSpec(
        num_scalar_prefetch=1,
        grid=(Bd, n_pages // n_pg),
        in_specs=[pl.BlockSpec((1, DSA_KV_HEADS, DSA_HEADS, DSA_DH), lambda b, jj, pt: (b, 0, 0, 0)),
                  pl.BlockSpec((1, 1, n_pg * PAGE_SIZE), lambda b, jj, pt: (b, 0, jj)),
                  pl.BlockSpec((1, 1, PAGE_SIZE), lambda b, jj, pt: (b, 0, 0)), new_tok, new_tok,
                  pl.BlockSpec(memory_space=pl.ANY), pl.BlockSpec(memory_space=pl.ANY)],
        out_specs=pl.BlockSpec((1, DSA_HEADS, DSA_DH), lambda b, jj, pt: (b, 0, 0)),
        scratch_shapes=[pages, pages, pltpu.SemaphoreType.DMA((2, 2)),
                        pltpu.VMEM((DSA_HEADS, 1), F32), pltpu.VMEM((DSA_HEADS, 1), F32),
                        pltpu.VMEM((DSA_HEADS, DSA_DH), F32)],
    )
    return pl.pallas_call(
        functools.partial(_dsa_sample_attn_kernel, n_pg=n_pg),
        grid_spec=grid_spec,
        out_shape=jax.ShapeDtypeStruct((Bd, DSA_HEADS, DSA_DH), F32),
        compiler_params=_cparams("arbitrary", "arbitrary"),
        name="dsa_sample_attn",
    )(page_table, q8, sel.reshape(Bd, 1, -1), sel_new.reshape(Bd, 1, PAGE_SIZE), k_new_b, v_new_b, cache_kt, cache_vt)


def _attend_rows(q, kk, vv, scale):
    s = jnp.sum(kk * q[None], axis=-1, keepdims=True) * scale
    m = jnp.max(s, axis=0, keepdims=True)
    p = jnp.exp(s - m)
    l = jnp.sum(p, axis=0)
    return jnp.sum(p * vv, axis=0) / l


def _mem_prompt_kernel(q_ref, mk_ref, mv_ref, o_ref):
    q = q_ref[0]
    mk = mk_ref[0]
    mv = mv_ref[0]
    outs = []
    for h in range(MEM_HEADS):
        sl = slice(h * MEM_DH, (h + 1) * MEM_DH)
        s = _dot_nt(q[:, sl], mk[:, sl]) * (MEM_DH ** -0.5)
        m = jnp.max(s, axis=1, keepdims=True)
        p = jnp.exp(s - m)
        p = p / jnp.sum(p, axis=1, keepdims=True)
        outs.append(_dot(p, mv[:, sl]))
    o_ref[0] = jnp.concatenate(outs, axis=1)


def _mem_prompt(q, mk, mv, tq):
    B, T, W = q.shape
    M = mk.shape[1]
    kv = pl.BlockSpec((1, M, W), lambda b, i: (b, 0, 0))
    return pl.pallas_call(
        _mem_prompt_kernel,
        grid=(B, T // tq),
        in_specs=[pl.BlockSpec((1, tq, W), lambda b, i: (b, i, 0)), kv, kv],
        out_specs=pl.BlockSpec((1, tq, W), lambda b, i: (b, i, 0)),
        out_shape=jax.ShapeDtypeStruct((B, T, W), F32),
        compiler_params=_cparams("parallel", "parallel"),
        name="mem_prompt",
    )(q, mk, mv)


def _mem_sample_kernel(q_ref, k_ref, v_ref, o_ref, *, ns):
    for i in range(ns):
        o_ref[i] = _attend_rows(q_ref[i], k_ref[i], v_ref[i], MEM_DH ** -0.5)


def _mem_sample(q, ck, cv, ns):
    Bd, M = ck.shape[:2]
    kv = pl.BlockSpec((ns, M, MEM_HEADS, MEM_DH), lambda b: (b, 0, 0, 0))
    qs = pl.BlockSpec((ns, MEM_HEADS, MEM_DH), lambda b: (b, 0, 0))
    return pl.pallas_call(
        functools.partial(_mem_sample_kernel, ns=ns),
        grid=(Bd // ns,),
        in_specs=[qs, kv, kv],
        out_specs=qs,
        out_shape=jax.ShapeDtypeStruct((Bd, MEM_HEADS, MEM_DH), F32),
        compiler_params=_cparams("parallel"),
        name="mem_sample",
    )(q, ck, cv)


def _merge_kernel(x_ref, og_ref, od_ref, om_ref, gn_ref, wg_ref, wb_ref, wo_ref, fn_ref, xo_ref, h2_ref):
    x = x_ref[...]
    d = x.shape[1]
    hb = _rms(x, gn_ref[...]).astype(BF16)
    acc = jnp.zeros_like(x)
    for n, o_ref in enumerate((og_ref, od_ref, om_ref)):
        gate = _sigmoid(jnp.dot(hb, wg_ref[:, n * d:(n + 1) * d], preferred_element_type=F32))
        acc = acc + gate * jnp.dot(_bf(o_ref[...]), wb_ref[n], preferred_element_type=F32)
    xo = x + jnp.dot(_bf(acc), wo_ref[...], preferred_element_type=F32)
    _tt_store(xo_ref, xo)
    _tt_store(h2_ref, _rms(xo, fn_ref[...]))


def _merge(x, o_g, o_d, o_m, gn, wg, wb, wo, fn, tm):
    n, d = x.shape
    tok = lambda w: pl.BlockSpec((tm, w), lambda i: (i, 0))
    full = lambda shape: pl.BlockSpec(shape, lambda i: (0,) * len(shape))
    return pl.pallas_call(
        _merge_kernel,
        grid=(n // tm,),
        in_specs=[tok(d), tok(BRANCH_W), tok(BRANCH_W), tok(BRANCH_W), full((1, d)), full((d, N_BRANCH * d)),
                  full((N_BRANCH, BRANCH_W, d)), full((d, d)), full((1, d))],
        out_specs=[pl.BlockSpec((tm * TT, LANES), lambda i: (i, 0))] * 2,
        out_shape=[jax.ShapeDtypeStruct((n * TT, LANES), F32)] * 2,
        compiler_params=_cparams("parallel"),
        name="merge",
    )(x, o_g, o_d, o_m, gn.reshape(1, d), wg, wb, wo, fn.reshape(1, d))


def _route_kernel(h_ref, wr_ref, br_ref, e_ref, p_ref, r_ref, cnt_ref, run_scr, *, tr):
    i = pl.program_id(0)

    @pl.when(i == 0)
    def _():
        run_scr[...] = jnp.zeros_like(run_scr)

    logits = _dot_nt(wr_ref[...], _tt_load(h_ref, tr)) + br_ref[...]
    eidx = _iota((N_EXPERTS, LANES), 0)
    su = jnp.where(_iota((LANES, LANES), 0) < _iota((LANES, LANES), 1), 1.0, 0.0).astype(BF16)
    run = run_scr[...]
    for c in range(tr // LANES):
        sl = slice(c * LANES, (c + 1) * LANES)
        l = logits[:, sl]
        vals, idxs = [], []
        for _ in range(TOP_K):
            m = jnp.max(l, axis=0, keepdims=True)
            idx = jnp.min(jnp.where(l == m, eidx, N_EXPERTS), axis=0, keepdims=True)
            vals.append(m)
            idxs.append(idx)
            l = jnp.where(eidx == idx, -jnp.inf, l)
        ex = [jnp.exp(v - vals[0]) for v in vals]
        den = ex[0] + ex[1] + ex[2] + ex[3]
        oh = jnp.zeros((N_EXPERTS, LANES), F32)
        for k in range(TOP_K):
            p_ref[k:k + 1, sl] = ex[k] / den
            e_ref[k:k + 1, sl] = idxs[k]
            oh = oh + jnp.where(eidx == idxs[k], 1.0, 0.0)
        before = jnp.dot(oh.astype(BF16), su, preferred_element_type=F32) + run
        for k in range(TOP_K):
            rk = jnp.sum(jnp.where(eidx == idxs[k], before, 0.0), axis=0, keepdims=True)
            r_ref[k:k + 1, sl] = rk.astype(I32)
        run = run + jnp.sum(oh, axis=1, keepdims=True)
    run_scr[...] = run
    cnt_ref[...] = jnp.broadcast_to(run, cnt_ref.shape)


def _route(h2, wr_t, br, tr):
    n = h2.shape[0] // TT
    d = TT * LANES
    tokrow = pl.BlockSpec((TOP_K, tr), lambda i: (0, i))
    return pl.pallas_call(
        functools.partial(_route_kernel, tr=tr),
        grid=(n // tr,),
        in_specs=[pl.BlockSpec((tr * TT, LANES), lambda i: (i, 0)),
                  pl.BlockSpec((N_EXPERTS, d), lambda i: (0, 0)),
                  pl.BlockSpec((N_EXPERTS, 1), lambda i: (0, 0))],
        out_specs=[tokrow, tokrow, tokrow, pl.BlockSpec((N_EXPERTS, LANES), lambda i: (0, 0))],
        out_shape=[jax.ShapeDtypeStruct((TOP_K, n), I32), jax.ShapeDtypeStruct((TOP_K, n), F32),
                   jax.ShapeDtypeStruct((TOP_K, n), I32), jax.ShapeDtypeStruct((N_EXPERTS, LANES), F32)],
        scratch_shapes=[pltpu.VMEM((N_EXPERTS, 1), F32)],
        compiler_params=_cparams("arbitrary"),
        name="moe_route",
    )(h2, wr_t, br)


def _dispatch_kernel(plo_ref, pn_ref, nb_ref, dest_ref, h_ref, xg_hbm, zblk, sem, *, td):
    blk = zblk.shape[0] // TT
    n_blocks = xg_hbm.shape[0] // (blk * TT)
    tile = lambda i: pl.ds(pl.multiple_of(i * TT, TT), TT)

    def issue(t, carry):
        for k in range(TOP_K):
            pltpu.make_async_copy(h_ref.at[tile(t)], xg_hbm.at[tile(dest_ref[k, t])], sem.at[0]).start()
        return carry

    lax.fori_loop(0, td, issue, 0, unroll=8)

    @pl.when(pl.program_id(0) == 0)
    def _():
        zblk[...] = jnp.zeros_like(zblk)

        def zero_block(i):
            rows = pl.ds(pl.multiple_of(i * (blk * TT), blk * TT), blk * TT)
            return pltpu.make_async_copy(zblk, xg_hbm.at[rows], sem.at[2])

        def tail_start(i, c):
            zero_block(i).start()
            return c

        def tail_wait(i, c):
            zero_block(0).wait()
            return c

        lax.fori_loop(nb_ref[0], n_blocks, tail_start, 0)
        lax.fori_loop(nb_ref[0], n_blocks, tail_wait, 0)

        pieces = [1 << i for i in reversed(range(blk.bit_length() - 1))]

        def per_expert(e, carry):
            n = pn_ref[e]
            for wait in (False, True):
                off = plo_ref[e]
                for size in pieces:
                    rows = pl.ds(pl.multiple_of(off * TT, TT), size * TT)
                    copy = pltpu.make_async_copy(zblk.at[pl.ds(0, size * TT)], xg_hbm.at[rows], sem.at[1])

                    @pl.when((n & size) != 0)
                    def _():
                        copy.wait() if wait else copy.start()

                    off = off + (n & size)
            return carry

        lax.fori_loop(0, N_EXPERTS, per_expert, 0)

    for k in range(TOP_K):
        pltpu.make_async_copy(h_ref, xg_hbm.at[pl.ds(0, td * TT)], sem.at[0]).wait()


def _dispatch(h2, dest, pad_lo, pad_n, nb_used, n_slots, blk, td):
    n = h2.shape[0] // TT
    grid_spec = pltpu.PrefetchScalarGridSpec(
        num_scalar_prefetch=3,
        grid=(n // td,),
        in_specs=[pl.BlockSpec((TOP_K, td), lambda i, plo, pn, nb: (0, i), memory_space=pltpu.SMEM),
                  pl.BlockSpec((td * TT, LANES), lambda i, plo, pn, nb: (i, 0))],
        out_specs=pl.BlockSpec(memory_space=pl.ANY),
        scratch_shapes=[pltpu.VMEM((blk * TT, LANES), F32), pltpu.SemaphoreType.DMA((3,))],
    )
    return pl.pallas_call(
        functools.partial(_dispatch_kernel, td=td),
        grid_spec=grid_spec,
        out_shape=jax.ShapeDtypeStruct((n_slots * TT, LANES), F32),
        compiler_params=_cparams("arbitrary"),
        name="moe_dispatch",
    )(pad_lo, pad_n, nb_used, dest, h2)


def _expert_kernel(be_ref, nb_ref, x_ref, wgu_ref, bgu_ref, wdn_ref, bdn_ref, o_ref):
    del be_ref
    f = wdn_ref.shape[1]
    blk = x_ref.shape[0] // TT

    used = pl.program_id(0) < nb_ref[0]

    @pl.when(jnp.logical_not(used))
    def _():
        o_ref[...] = jnp.zeros_like(o_ref)

    @pl.when(used)
    def _():
        gu = jnp.dot(_bf(_tt_load(x_ref, blk)), _bf(wgu_ref[0]), preferred_element_type=F32) + bgu_ref[0]
        gate = jnp.minimum(gu[:, :f], SWIGLU_LIMIT)
        up = jnp.clip(gu[:, f:], -SWIGLU_LIMIT, SWIGLU_LIMIT)
        glu = gate * _sigmoid(SWIGLU_ALPHA * gate)
        _tt_store(o_ref, jnp.dot(_bf((up + 1.0) * glu), _bf(wdn_ref[0]), preferred_element_type=F32) + bdn_ref[0])


def _experts(xg, block_e, nb_used, wgu, bgu, wdn, bdn, blk):
    n_slots = xg.shape[0] // TT
    d = TT * LANES
    f = wdn.shape[1]
    blk_of = lambda i, nb: jnp.minimum(i, nb[0] - 1)
    tok = pl.BlockSpec((blk * TT, LANES), lambda i, be, nb: (blk_of(i, nb), 0))
    per_e = lambda shape: pl.BlockSpec((1,) + shape, lambda i, be, nb: (be[blk_of(i, nb)], 0, 0))
    grid_spec = pltpu.PrefetchScalarGridSpec(
        num_scalar_prefetch=2,
        grid=(n_slots // blk,),
        in_specs=[tok, per_e((d, 2 * f)), per_e((1, 2 * f)), per_e((f, d)), per_e((1, d))],
        out_specs=pl.BlockSpec((blk * TT, LANES), lambda i, be, nb: (i, 0)),
    )
    return pl.pallas_call(
        _expert_kernel,
        grid_spec=grid_spec,
        out_shape=jax.ShapeDtypeStruct((n_slots * TT, LANES), F32),
        compiler_params=_cparams("arbitrary"),
        name="moe_experts",
    )(block_e, nb_used, xg, wgu, bgu, wdn, bdn)


def _combine_kernel(dest_ref, x_ref, p_ref, g_ref, yb_hbm, o_ref, buf, sem, *, tc):
    tile = lambda i: pl.ds(pl.multiple_of(i * TT, TT), TT)

    def issue(t, carry):
        for k in range(TOP_K):
            pltpu.make_async_copy(yb_hbm.at[tile(dest_ref[k, t])], buf.at[k, tile(t)], sem.at[0]).start()
        return carry

    lax.fori_loop(0, tc, issue, 0, unroll=8)
    for k in range(TOP_K):
        pltpu.make_async_copy(yb_hbm.at[pl.ds(0, tc * TT)], buf.at[k], sem.at[0]).wait()
    p = p_ref[...]
    acc = p[:, 0:1] * _tt_load(buf.at[0], tc)
    for k in range(1, TOP_K):
        acc = acc + p[:, k:k + 1] * _tt_load(buf.at[k], tc)
    o_ref[...] = _rms(_tt_load(x_ref, tc) + acc, g_ref[...])


def _combine(x, yb, dest, p_t, g, tc):
    n = x.shape[0] // TT
    d = TT * LANES
    return pl.pallas_call(
        functools.partial(_combine_kernel, tc=tc),
        grid=(n // tc,),
        in_specs=[pl.BlockSpec((TOP_K, tc), lambda i: (0, i), memory_space=pltpu.SMEM),
                  pl.BlockSpec((tc * TT, LANES), lambda i: (i, 0)),
                  pl.BlockSpec((tc, TOP_K), lambda i: (i, 0)),
                  pl.BlockSpec((1, d), lambda i: (0, 0)),
                  pl.BlockSpec(memory_space=pl.ANY)],
        out_specs=pl.BlockSpec((tc, d), lambda i: (i, 0)),
        out_shape=jax.ShapeDtypeStruct((n, d), F32),
        scratch_shapes=[pltpu.VMEM((TOP_K, tc * TT, LANES), F32), pltpu.SemaphoreType.DMA((1,))],
        compiler_params=_cparams("arbitrary"),
        name="moe_combine",
    )(dest, x, p_t, g.reshape(1, d), yb)


def _moe_final(x, h2, g_final, wr_t, br, wgu, bgu, wdn, bdn, blk, tile):
    n = x.shape[0] // TT
    e, p, rank, cnt = _route(h2, wr_t, br, tile)
    counts = cnt[:, 0].astype(I32)
    padded = (counts + blk - 1) // blk * blk
    pad_end = jnp.cumsum(padded)
    pad_start = pad_end - padded
    n_blocks = -(-n * TOP_K // blk) + N_EXPERTS
    expert_ids = jnp.arange(N_EXPERTS, dtype=I32)[:, None, None]
    dest = rank + jnp.sum(jnp.where(e[None] == expert_ids, pad_start[:, None, None], 0), axis=0)
    block_lo = jnp.arange(n_blocks, dtype=I32)[:, None] * blk
    block_e = jnp.minimum(jnp.sum((pad_end[None, :] <= block_lo).astype(I32), axis=1), N_EXPERTS - 1)
    nb_used = (pad_end[-1:] // blk).astype(I32)
    xg = _dispatch(h2, dest, pad_start + counts, padded - counts, nb_used, n_blocks * blk, blk, tile)
    yb = _experts(xg, block_e, nb_used, wgu, bgu, wdn, bdn, blk)
    return _combine(x, yb, dest, p.T, g_final, min(tile, 256))


def _prep_w_in(w):
    cuts = np.cumsum((0,) + _SPLITS)
    seg = lambda i: w[:, int(cuts[i]):int(cuts[i + 1])]
    small = jnp.concatenate([seg(8), seg(9), seg(2), seg(3), jnp.zeros((w.shape[0], LANES - _SM_END), w.dtype)], axis=1)
    w1 = jnp.concatenate([seg(0), seg(1), seg(4), seg(5), seg(6), seg(7), seg(10), small], axis=1)
    return w1.astype(BF16), w[:, int(cuts[-1]):].astype(BF16)


def kernel(x_prompt, x_sample, cache_k, cache_v, cache_k_idx, cache_mem_k, cache_mem_v, state_gdn, state_conv,
           page_table, mem_prompt, norm_attn, w_in, conv_w, gdn_a_log, gdn_dt_bias, gdn_norm, norm_mem, w_mem_kv,
           w_branch, w_out, norm_ffn, w_router, b_router, w_gate_up, b_gate_up, w_down, b_down, norm_final):
    B, T, D = x_prompt.shape
    Bd, Ts, _ = x_sample.shape
    assert Ts == 1 and w_in.shape[0] == 1, "one layer, one new token per sample"
    H = GDN_HEADS
    G = DSA_HEADS // DSA_KV_HEADS
    n_pages = page_table.shape[1]
    M = mem_prompt.shape[1]

    w1, wg = _prep_w_in(w_in[0])
    wb = w_branch[0].astype(BF16)
    wo = w_out[0].astype(BF16)
    wr_t = w_router[0].T
    br = b_router[0].reshape(N_EXPERTS, 1)
    wgu = w_gate_up[0]
    wdn = w_down[0]
    bgu = b_gate_up[0].reshape(N_EXPERTS, 1, -1)
    bdn = b_down[0].reshape(N_EXPERTS, 1, -1)
    alog = gdn_a_log[0]
    dtb = gdn_dt_bias[0]

    xp = x_prompt.reshape(B * T, D)
    g_qkv, g_z, d_k, d_v, i_q, m_q, small, qh, kt, vh = _proj_prompt(xp, norm_attn[0], w1, B, T, min(512, T))

    quad_rows = lambda a: (a.reshape(B, T // GDN_CHUNK, GDN_CHUNK, 2, 4).transpose(0, 1, 3, 4, 2)
                           .reshape(B, T // GDN_CHUNK, 2, 1, QW))
    per_group = lambda a: jnp.repeat(a, GDN_DK).reshape(2, 1, QW)
    o_g, ssm_q = _gdn_prompt(g_qkv.reshape(B, T, GDN_QKV), g_z.reshape(B, T, H * GDN_DV),
                             quad_rows(small[:, _SM_GA:_SM_GA + H]), quad_rows(small[:, _SM_GB:_SM_GB + H]),
                             conv_w[0], per_group(alog), per_group(dtb), jnp.tile(gdn_norm[0], 4).reshape(1, QW),
                             min(512, T))
    o_g = o_g.reshape(B * T, H * GDN_DV)
    ssm_p = ssm_q.reshape(B, 2, GDN_DK, 4, GDN_DV).transpose(0, 1, 3, 2, 4).reshape(B, H, GDN_DK, GDN_DV)

    o_d = _dsa_prompt(i_q.reshape(B, T, -1), small.reshape(B, T, LANES), qh, kt, vh, min(TOPK_MAX, T // 4),
                      min(256, T))
    o_d = o_d.reshape(B * T, DSA_HEADS * DSA_DH)

    mk, mv = _norm_matmul(mem_prompt.reshape(B * M, D), norm_mem[0], w_mem_kv[0].astype(BF16),
                          (MEM_HEADS * MEM_DH,) * 2, min(512, B * M))
    o_m = _mem_prompt(m_q.reshape(B, T, -1), mk.reshape(B, M, -1), mv.reshape(B, M, -1), min(512, T))
    o_m = o_m.reshape(B * T, MEM_HEADS * MEM_DH)

    xres, h2 = _merge(xp, o_g, o_d, o_m, norm_attn[0], wg, wb, wo, norm_ffn[0], min(256, B * T))
    y_prompt = _moe_final(xres, h2, norm_final, wr_t, br, wgu, bgu, wdn, bdn, 256, min(512, B * T))

    k_prompt = d_k.reshape(1, B, T, DSA_KV_HEADS, DSA_DH)
    v_prompt = d_v.reshape(1, B, T, DSA_KV_HEADS, DSA_DH)
    kidx_prompt = small[:, _SM_IK:_SM_IK + IDX_DIM].reshape(1, B, T, IDX_DIM)
    memk_prompt = mk.reshape(1, B, M, MEM_HEADS, MEM_DH)
    memv_prompt = mv.reshape(1, B, M, MEM_HEADS, MEM_DH)
    conv_prompt = g_qkv.reshape(B, T, GDN_QKV)[:, T - (GDN_CONV - 1):, :][None]

    xs = x_sample.reshape(Bd, D)
    s_qkv, s_z, sd_q, sd_k, sd_v, si_q, sm_q, ssmall = _norm_matmul(xs, norm_attn[0], w1, _PROJ_SPLITS, Bd)

    lanes_b = lambda a: jnp.broadcast_to(a[..., None], a.shape + (Bd,))
    og_t, s_t = _gdn_sample(
        s_qkv.T, state_conv[0].transpose(1, 2, 0), lanes_b(conv_w[0]), s_z.T,
        ssmall[:, _SM_GA:_SM_GA + H].T.reshape(H, 1, Bd), ssmall[:, _SM_GB:_SM_GB + H].T.reshape(H, 1, Bd),
        lanes_b(alog.reshape(H, 1)), lanes_b(dtb.reshape(H, 1)), lanes_b(gdn_norm[0]),
        state_gdn[0].transpose(1, 2, 3, 0))
    so_g = og_t.T
    ssm_sample = s_t.transpose(3, 0, 1, 2)[None]
    conv_sample = jnp.concatenate([state_conv[0][:, 1:], s_qkv[:, None, :]], axis=1)[None]

    qi8 = jnp.pad(si_q.reshape(Bd, IDX_HEADS, IDX_DIM), ((0, 0), (0, 8 - IDX_HEADS), (0, 0)))
    wi8 = jnp.pad(ssmall[:, _SM_IW:_SM_IW + IDX_HEADS], ((0, 0), (0, 8 - IDX_HEADS)))[..., None]
    n_pg = 32 if n_pages % 32 == 0 else n_pages
    cache_kit = cache_k_idx[0].transpose(0, 2, 1)
    cache_kt = cache_k[0].transpose(0, 2, 3, 1)
    cache_vt = cache_v[0].transpose(0, 2, 3, 1)
    scores = _dsa_sample_scores(page_table, qi8, wi8, cache_kit)
    scores = scores.reshape(Bd, n_pages * PAGE_SIZE)
    n_sel = min(TOPK_MAX, (n_pages * PAGE_SIZE + 1) // 4)
    sel, sel_new = _dsa_sample_select(scores, si_q, ssmall, n_sel)
    over_lanes = lambda a: jnp.broadcast_to(a[..., None], a.shape + (PAGE_SIZE,))
    head_kv = jnp.arange(DSA_HEADS, dtype=I32) // G
    q8 = jnp.where((head_kv[None, :] == jnp.arange(DSA_KV_HEADS, dtype=I32)[:, None])[None, :, :, None],
                   sd_q.reshape(Bd, 1, DSA_HEADS, DSA_DH), 0.0)
    k_new = sd_k.reshape(Bd, DSA_KV_HEADS, DSA_DH)
    v_new = sd_v.reshape(Bd, DSA_KV_HEADS, DSA_DH)
    so_d = _dsa_sample_attn(page_table, q8, sel, sel_new, over_lanes(k_new), over_lanes(v_new),
                            cache_kt, cache_vt, n_pg).reshape(Bd, DSA_HEADS * DSA_DH)

    so_m = _mem_sample(sm_q.reshape(Bd, MEM_HEADS, MEM_DH), cache_mem_k[0], cache_mem_v[0], 4)
    so_m = so_m.reshape(Bd, MEM_HEADS * MEM_DH)

    sres, sh2 = _merge(xs, so_g, so_d, so_m, norm_attn[0], wg, wb, wo, norm_ffn[0], Bd)
    y_sample = _moe_final(sres, sh2, norm_final, wr_t, br, wgu, bgu, wdn, bdn, 256, Bd)

    return (y_prompt.reshape(B, T, D), y_sample.reshape(Bd, 1, D), k_prompt, v_prompt, kidx_prompt,
            memk_prompt, memv_prompt, ssm_p[None], conv_prompt,
            k_new.reshape(1, Bd, 1, DSA_KV_HEADS, DSA_DH), v_new.reshape(1, Bd, 1, DSA_KV_HEADS, DSA_DH),
            ssmall[:, _SM_IK:_SM_IK + IDX_DIM].reshape(1, Bd, 1, IDX_DIM), ssm_sample, conv_sample)
```

```python
import functools

import numpy as np
import jax
import jax.numpy as jnp
from jax import lax
from jax.experimental import pallas as pl
from jax.experimental.pallas import tpu as pltpu

F32 = jnp.float32
BF16 = jnp.bfloat16
I32 = jnp.int32

EPS = 1e-6
GDN_HEADS = 8
GDN_DK = 64
GDN_DV = 64
GDN_CONV = 4
GDN_CHUNK = 64
GDN_QKV = 2 * GDN_HEADS * GDN_DK + GDN_HEADS * GDN_DV
DSA_HEADS = 8
DSA_KV_HEADS = 4
DSA_DH = 64
IDX_HEADS = 4
IDX_DIM = 64
TOPK_MAX = 256
Q_BLOCK = 128
MEM_HEADS = 4
MEM_DH = 128
N_BRANCH = 3
BRANCH_W = 512
N_EXPERTS = 32
TOP_K = 4
SWIGLU_LIMIT = 7.0
SWIGLU_ALPHA = 1.702
PAGE_SIZE = 128
LANES = 128

_SPLITS = (GDN_QKV, GDN_HEADS * GDN_DV, GDN_HEADS, GDN_HEADS,
           DSA_HEADS * DSA_DH, DSA_KV_HEADS * DSA_DH, DSA_KV_HEADS * DSA_DH,
           IDX_HEADS * IDX_DIM, IDX_DIM, IDX_HEADS,
           MEM_HEADS * MEM_DH)
_PROJ_SPLITS = (GDN_QKV, 512, 512, 256, 256, 256, 512, LANES)
_SM_IK = 0
_SM_IW = IDX_DIM
_SM_GB = _SM_IW + IDX_HEADS
_SM_GA = _SM_GB + GDN_HEADS
_SM_END = _SM_GA + GDN_HEADS

VMEM_LIMIT_BYTES = 56 * 1024 * 1024


def _cparams(*sem):
    return pltpu.CompilerParams(dimension_semantics=sem, vmem_limit_bytes=VMEM_LIMIT_BYTES)


def _bf(x):
    return x.astype(BF16)


def _dot(a, b):
    return jnp.dot(_bf(a), _bf(b), preferred_element_type=F32)


_NT = (((1,), (1,)), ((), ()))


def _dot_nt(a, b):
    return lax.dot_general(_bf(a), _bf(b), _NT, preferred_element_type=F32)


def _split2(x):
    hi = x.astype(BF16)
    lo = (x - hi.astype(F32)).astype(BF16)
    return hi, lo


def _split3(x):
    hi = x.astype(BF16)
    r = x - hi.astype(F32)
    mid = r.astype(BF16)
    lo = (r - mid.astype(F32)).astype(BF16)
    return hi, mid, lo


def _dot_exact01(a, b01):
    r = a.shape[0]
    parts = jnp.dot(jnp.concatenate(_split3(a), axis=0), b01, preferred_element_type=F32)
    return parts[:r] + (parts[r:2 * r] + parts[2 * r:])


def _rms(x, g):
    return x * lax.rsqrt(jnp.mean(x * x, axis=-1, keepdims=True) + EPS) * g


def _sigmoid(x):
    return 1.0 / (1.0 + jnp.exp(-x))


def _silu(x):
    return x * _sigmoid(x)


def _softplus(x):
    return jnp.maximum(x, 0.0) + jnp.log(1.0 + jnp.exp(-jnp.abs(x)))


def _iota(shape, axis):
    return lax.broadcasted_iota(I32, shape, axis)


TT = 8


def _tt_load(ref, n):
    return jnp.concatenate([ref[pl.ds(c, n, stride=TT), :] for c in range(TT)], axis=1)


def _tt_store(ref, val):
    n = val.shape[0]
    for c in range(TT):
        ref[pl.ds(c, n, stride=TT), :] = val[:, c * LANES:(c + 1) * LANES]


def _norm_matmul_kernel(x_ref, g_ref, w_ref, *o_refs, splits):
    hb = _rms(x_ref[...], g_ref[...]).astype(BF16)
    off = 0
    for o_ref, n in zip(o_refs, splits):
        o_ref[...] = jnp.dot(hb, w_ref[:, off:off + n], preferred_element_type=F32)
        off += n


def _norm_matmul(x, g, w, splits, tm):
    n, d = x.shape
    return pl.pallas_call(
        functools.partial(_norm_matmul_kernel, splits=splits),
        grid=(n // tm,),
        in_specs=[pl.BlockSpec((tm, d), lambda i: (i, 0)),
                  pl.BlockSpec((1, d), lambda i: (0, 0)),
                  pl.BlockSpec((d, sum(splits)), lambda i: (0, 0))],
        out_specs=[pl.BlockSpec((tm, s), lambda i: (i, 0)) for s in splits],
        out_shape=[jax.ShapeDtypeStruct((n, s), F32) for s in splits],
        compiler_params=_cparams("parallel"),
        name="norm_matmul",
    )(x, g.reshape(1, d), w)


def _proj_prompt_kernel(x_ref, g_ref, w_ref, qkv_ref, z_ref, dk_ref, dv_ref, iq_ref, mq_ref, sm_ref,
                        qh_ref, kt_ref, vh_ref):
    hb = _rms(x_ref[...], g_ref[...]).astype(BF16)
    offs = np.cumsum((0,) + _PROJ_SPLITS)
    part = lambda i: jnp.dot(hb, w_ref[:, int(offs[i]):int(offs[i + 1])], preferred_element_type=F32)
    qkv_ref[...] = part(0)
    z_ref[...] = part(1)
    dq = part(2)
    for h in range(DSA_HEADS):
        qh_ref[0, h] = dq[:, h * DSA_DH:(h + 1) * DSA_DH]
    dk = part(3)
    dk_ref[...] = dk
    kt_ref[0] = dk.T.reshape(DSA_KV_HEADS, DSA_DH, dk.shape[0])
    dv = part(4)
    dv_ref[...] = dv
    for kv in range(DSA_KV_HEADS):
        vh_ref[0, kv] = dv[:, kv * DSA_DH:(kv + 1) * DSA_DH]
    iq_ref[...] = part(5)
    mq_ref[...] = part(6)
    sm_ref[...] = part(7)


def _proj_prompt(x, g, w, B, T, tm):
    n, d = x.shape
    tpb = T // tm
    tok = lambda width: pl.BlockSpec((tm, width), lambda b, t: (b * tpb + t, 0))
    flat = lambda width: jax.ShapeDtypeStruct((n, width), F32)
    sp = _PROJ_SPLITS
    return pl.pallas_call(
        _proj_prompt_kernel,
        grid=(B, tpb),
        in_specs=[tok(d), pl.BlockSpec((1, d), lambda b, t: (0, 0)), pl.BlockSpec((d, sum(sp)), lambda b, t: (0, 0))],
        out_specs=[tok(sp[0]), tok(sp[1]), tok(sp[3]), tok(sp[4]), tok(sp[5]), tok(sp[6]), tok(sp[7]),
                   pl.BlockSpec((1, DSA_HEADS, tm, DSA_DH), lambda b, t: (b, 0, t, 0)),
                   pl.BlockSpec((1, DSA_KV_HEADS, DSA_DH, tm), lambda b, t: (b, 0, 0, t)),
                   pl.BlockSpec((1, DSA_KV_HEADS, tm, DSA_DH), lambda b, t: (b, 0, t, 0))],
        out_shape=[flat(sp[0]), flat(sp[1]), flat(sp[3]), flat(sp[4]), flat(sp[5]), flat(sp[6]), flat(sp[7]),
                   jax.ShapeDtypeStruct((B, DSA_HEADS, T, DSA_DH), F32),
                   jax.ShapeDtypeStruct((B, DSA_KV_HEADS, DSA_DH, T), F32),
                   jax.ShapeDtypeStruct((B, DSA_KV_HEADS, T, DSA_DH), F32)],
        compiler_params=_cparams("parallel", "parallel"),
        name="proj_prompt",
    )(x, g.reshape(1, d), w)


QW = 4 * GDN_DK


def _bd_stack(x):
    lane_head = (_iota(x.shape, 1) >> 6) & 3
    return jnp.concatenate([jnp.where(lane_head == h, x, 0.0) for h in range(4)], axis=0)


def _bd_dot(a, x):
    return jnp.dot(_bf(a), _bf(_bd_stack(x)), preferred_element_type=F32)


def _bd_split(x):
    xh = x.astype(BF16).astype(F32)
    return _bf(_bd_stack(xh)), _bf(_bd_stack(x - xh))


def _bd_dot3(lhs, xb):
    bh, bl = xb
    r = lhs[0].shape[0]
    parts = [_split2(a) for a in lhs]
    hi = jnp.dot(jnp.concatenate([p for hl in parts for p in hl], axis=0), bh, preferred_element_type=F32)
    lo = jnp.dot(jnp.concatenate([hl[0] for hl in parts], axis=0), bl, preferred_element_type=F32)
    return [hi[2 * i * r:(2 * i + 1) * r] + (lo[i * r:(i + 1) * r] + hi[(2 * i + 1) * r:(2 * i + 2) * r])
            for i in range(len(lhs))]


def _gdn_prompt_kernel(q_ref, k_ref, v_ref, qh_ref, kh_ref, vh_ref, z_ref, a_ref, b_ref,
                       cwq_ref, cwk_ref, cwv_ref, alog_ref, dtb_ref, nw_ref, o_ref, s_ref,
                       xq, xk, xv, s_scr, *, tt):
    ti = pl.program_id(1)
    C = GDN_CHUNK
    HALO = 8
    NG = GDN_HEADS // 4
    CPI = 2

    @pl.when(ti == 0)
    def _():
        s_scr[...] = jnp.zeros_like(s_scr)

    r = _iota((QW, QW), 0)
    c = _iota((QW, QW), 1)
    same = (r >> 6) == (c >> 6)
    ones_bd = jnp.where(same, 1.0, 0.0).astype(BF16)
    su_bd = jnp.where(same, jnp.where((r & 63) > (c & 63), 1.0, 0.0), 0.0).astype(BF16)
    rhs01 = jnp.concatenate([su_bd, ones_bd], axis=1)
    ident = jnp.where(r == c, 1.0, 0.0).astype(BF16)

    keep = (ti > 0).astype(F32)
    for idx, (src, halo, dst, cw) in enumerate(((q_ref, qh_ref, xq, cwq_ref), (k_ref, kh_ref, xk, cwk_ref),
                                                (v_ref, vh_ref, xv, cwv_ref))):
        dst[0:HALO, :] = halo[0] * keep
        dst[HALO:HALO + tt, :] = src[0]
        w = cw[...]
        y = w[0:1, :] * dst[pl.ds(HALO - 3, tt), :]
        for i in range(1, GDN_CONV):
            y = y + w[i:i + 1, :] * dst[pl.ds(HALO - 3 + i, tt), :]
        y = _silu(y)
        if idx < 2:
            ss = jnp.concatenate([_dot_exact01((y * y)[:, g * QW:(g + 1) * QW], ones_bd) for g in range(NG)], axis=1)
            y = y * lax.rsqrt(ss + EPS)
            if idx == 0:
                y = y * (GDN_DK ** -0.5)
        dst[HALO:HALO + tt, :] = y

    ri = _iota((C, QW), 0)
    li = _iota((C, QW), 1) & 63
    lane_head = _iota((C, QW), 1) >> 6
    incl = li <= ri
    strict = li < ri
    eye = li == ri
    nw = nw_ref[...]

    def solve_steps(cis, out):
        streams = [(ci, g) for ci in cis for g in range(NG)]
        n = range(len(streams))
        rows = [slice(HALO + ci * C, HALO + (ci + 1) * C) for ci, _ in streams]
        cols = [slice(g * QW, (g + 1) * QW) for _, g in streams]
        qc = [xq[rows[i], cols[i]] for i in n]
        kc = [xk[rows[i], cols[i]] for i in n]
        vc = [xv[rows[i], cols[i]] for i in n]
        g_row = [-jnp.exp(alog_ref[g]) * _softplus(a_ref[0, ci, g] + dtb_ref[g]) for ci, g in streams]
        b_row = [_sigmoid(b_ref[0, ci, g]) for ci, g in streams]
        res = [_dot_exact01(jnp.concatenate([jnp.where(incl, g_row[i], 0.0), jnp.where(eye, b_row[i], 0.0)], axis=0),
                            rhs01) for i in n]
        yield
        gcol = [res[i][:C, QW:] for i in n]
        bcol = [res[i][C:, QW:] for i in n]
        gamma = [jnp.where(incl, jnp.exp(res[i][:C, :QW]), 0.0) for i in n]
        eg = [jnp.exp(gcol[i]) for i in n]
        glast = [gcol[i][C - 1:C, :] for i in n]
        kb = [kc[i] * bcol[i] for i in n]
        kt = [lax.dot_general(ident, jnp.concatenate([_bf(kc[i])] * 4, axis=0), _NT, preferred_element_type=F32)
              for i in n]
        yield
        kt = [_bf(jnp.where(same, kt[i], 0.0)) for i in n]
        kk = [jnp.dot(_bf(jnp.concatenate([kb[i], qc[i]], axis=0)), kt[i], preferred_element_type=F32)
              for i in n]
        yield
        r = [jnp.where(strict, -(kk[i][:C] * gamma[i]), 0.0) for i in n]
        t = [jnp.where(eye, 1.0, r[i]) for i in n]
        for k in range(6):
            rb = [_bd_split(r[i]) for i in n]
            prod = [_bd_dot3(([t[i]] if k > 0 else []) + ([r[i]] if k < 5 else []), rb[i]) for i in n]
            if k > 0:
                t = [t[i] + prod[i][0] for i in n]
            if k < 5:
                r = [prod[i][-1] for i in n]
            yield
        rhs = [_bd_split(jnp.concatenate([vc[i] * bcol[i], kb[i] * eg[i]], axis=1)) for i in n]
        y = [_bd_dot3([t[i]], rhs[i])[0] for i in n]
        out.extend(dict(ci=streams[i][0], g=streams[i][1], y=y[i], qg=qc[i] * eg[i], qk=kk[i][C:] * gamma[i],
                        kd=kc[i] * jnp.exp(glast[i] - gcol[i]), dec=jnp.exp(glast[i])) for i in n)

    def scan_steps(st):
        n = range(len(st))
        s = [s_scr[t["g"]] for t in st]
        ws = [_bd_dot(jnp.concatenate([st[i]["y"][:, QW:], st[i]["qg"]], axis=0), s[i]) for i in n]
        yield
        vn = [st[i]["y"][:, :QW] - ws[i][:C] for i in n]
        o = [ws[i][C:] + _bd_dot(st[i]["qk"], vn[i]) for i in n]
        kdt = [lax.dot_general(ident, _bf(st[i]["kd"]), _NT, preferred_element_type=F32) for i in n]
        yield
        full = [jnp.dot(_bf(kdt[i]), _bf(vn[i]), preferred_element_type=F32) for i in n]
        ms = [_dot_exact01(o[i] * o[i], ones_bd) * (1.0 / GDN_DV) for i in n]
        yield
        for i in n:
            g = st[i]["g"]
            sadd = jnp.where(lane_head == 0, full[i][0:C], 0.0)
            for h in range(1, 4):
                sadd = sadd + jnp.where(lane_head == h, full[i][h * C:(h + 1) * C], 0.0)
            s_scr[g] = s[i] * st[i]["dec"] + sadd
            rows = slice(st[i]["ci"] * C, (st[i]["ci"] + 1) * C)
            cols = slice(g * QW, (g + 1) * QW)
            o_ref[0, rows, cols] = o[i] * lax.rsqrt(ms[i] + EPS) * nw * _silu(z_ref[0, rows, cols])

    def scan_chunks(st):
        for j in range(len(st) // NG):
            yield from scan_steps(st[j * NG:(j + 1) * NG])

    def run(*gens):
        live = list(gens)
        while live:
            for gen in list(live):
                try:
                    next(gen)
                except StopIteration:
                    live.remove(gen)

    n_groups = tt // (CPI * C)
    solved = [[] for _ in range(n_groups)]
    run(solve_steps(range(CPI), solved[0]))
    for gi in range(1, n_groups):
        run(solve_steps(range(gi * CPI, (gi + 1) * CPI), solved[gi]), scan_chunks(solved[gi - 1]))
    run(scan_chunks(solved[-1]))
    s_ref[0] = s_scr[...]


def _gdn_prompt(qkv, z, a, b, cw, alog, dtb, nw, tt):
    B, T, _ = qkv.shape
    nct = tt // GDN_CHUNK
    NG = GDN_HEADS // 4
    W = NG * QW
    col = lambda j: pl.BlockSpec((1, tt, W), lambda bi, ti: (bi, ti, j))
    halo = lambda j: pl.BlockSpec((1, 8, W), lambda bi, ti: (bi, jnp.maximum(ti * (tt // 8) - 1, 0), j))
    chunked = pl.BlockSpec((1, nct, NG, 1, QW), lambda bi, ti: (bi, ti, 0, 0, 0))
    cwcol = lambda j: pl.BlockSpec((GDN_CONV, W), lambda bi, ti: (0, j))
    per_group = pl.BlockSpec((NG, 1, QW), lambda bi, ti: (0, 0, 0))
    return pl.pallas_call(
        functools.partial(_gdn_prompt_kernel, tt=tt),
        grid=(B, T // tt),
        in_specs=[col(0), col(1), col(2), halo(0), halo(1), halo(2), col(0), chunked, chunked,
                  cwcol(0), cwcol(1), cwcol(2), per_group, per_group,
                  pl.BlockSpec((1, QW), lambda bi, ti: (0, 0))],
        out_specs=[col(0), pl.BlockSpec((1, NG, GDN_DK, QW), lambda bi, ti: (bi, 0, 0, 0))],
        out_shape=[jax.ShapeDtypeStruct((B, T, W), F32), jax.ShapeDtypeStruct((B, NG, GDN_DK, QW), F32)],
        scratch_shapes=[pltpu.VMEM((8 + tt, W), F32)] * 3 + [pltpu.VMEM((NG, GDN_DK, QW), F32)],
        compiler_params=_cparams("parallel", "arbitrary"),
        name="gdn_prompt",
    )(qkv, qkv, qkv, qkv, qkv, qkv, z, a, b, cw, cw, cw, alog, dtb, nw)


def _gdn_sample_kernel(q_ref, k_ref, v_ref, cq_ref, ck_ref, cv_ref, wq_ref, wk_ref, wv_ref,
                       z_ref, a_ref, b_ref, alog_ref, dtb_ref, nw_ref, s_ref, o_ref, so_ref):
    def conv(x_ref, c_ref, w_ref):
        y = w_ref[GDN_CONV - 1] * x_ref[...]
        for i in range(GDN_CONV - 1):
            y = y + w_ref[i] * c_ref[i]
        return _silu(y)

    q = conv(q_ref, cq_ref, wq_ref)
    q = q * lax.rsqrt(jnp.sum(q * q, axis=0, keepdims=True) + EPS) * (GDN_DK ** -0.5)
    k = conv(k_ref, ck_ref, wk_ref)
    k = k * lax.rsqrt(jnp.sum(k * k, axis=0, keepdims=True) + EPS)
    v = conv(v_ref, cv_ref, wv_ref)
    beta = _sigmoid(b_ref[0])
    g = -jnp.exp(alog_ref[0]) * _softplus(a_ref[0] + dtb_ref[0])
    dec = jnp.exp(g)
    ks = jnp.zeros_like(v)
    for i in range(GDN_DK):
        ks = ks + k[i:i + 1, :] * (s_ref[0, i] * dec)
    vn = beta * (v - ks)
    o = jnp.zeros_like(v)
    for i in range(GDN_DK):
        sn = s_ref[0, i] * dec + k[i:i + 1, :] * vn
        so_ref[0, i] = sn
        o = o + q[i:i + 1, :] * sn
    on = o * lax.rsqrt(jnp.mean(o * o, axis=0, keepdims=True) + EPS) * nw_ref[...]
    o_ref[...] = on * _silu(z_ref[...])


def _gdn_sample(qkv_t, conv_t, cw_t, z_t, a_t, b_t, alog_t, dtb_t, nw_t, s_t):
    nb = qkv_t.shape[1]
    H, DK, DV = GDN_HEADS, GDN_DK, GDN_DV
    row = lambda off: pl.BlockSpec((DK, nb), lambda h: (off + h, 0))
    crow = lambda off: pl.BlockSpec((GDN_CONV - 1, DK, nb), lambda h: (0, off + h, 0))
    wrow = lambda off: pl.BlockSpec((GDN_CONV, DK, nb), lambda h: (0, off + h, 0))
    per_head = pl.BlockSpec((1, 1, nb), lambda h: (h, 0, 0))
    return pl.pallas_call(
        _gdn_sample_kernel,
        grid=(H,),
        in_specs=[row(0), row(H), row(2 * H), crow(0), crow(H), crow(2 * H), wrow(0), wrow(H), wrow(2 * H),
                  row(0), per_head, per_head, per_head, per_head,
                  pl.BlockSpec((DV, nb), lambda h: (0, 0)),
                  pl.BlockSpec((1, DK, DV, nb), lambda h: (h, 0, 0, 0))],
        out_specs=[row(0), pl.BlockSpec((1, DK, DV, nb), lambda h: (h, 0, 0, 0))],
        out_shape=[jax.ShapeDtypeStruct((H * DV, nb), F32), jax.ShapeDtypeStruct((H, DK, DV, nb), F32)],
        compiler_params=_cparams("parallel"),
        name="gdn_sample",
    )(qkv_t, qkv_t, qkv_t, conv_t, conv_t, conv_t, cw_t, cw_t, cw_t, z_t, a_t, b_t, alog_t, dtb_t, nw_t, s_t)


def _count_ge(x, thr):
    return jnp.sum(jnp.where(x >= thr, 1.0, 0.0), axis=1, keepdims=True)


def _topk_mask(score, valid, k):
    rows, width = score.shape
    kf = float(k)
    x = jnp.where(valid, score, -jnp.inf)
    validf = jnp.where(valid, 1.0, 0.0)
    nvalid = jnp.sum(validf, axis=1, keepdims=True)
    few = nvalid <= kf
    mx = jnp.max(x, axis=1, keepdims=True)
    mn = jnp.min(jnp.where(valid, score, jnp.inf), axis=1, keepdims=True)
    lo0 = mn
    hi0 = mx + (jnp.abs(mx) * 1e-6 + 1e-30)
    chi0 = jnp.zeros_like(mx)
    clo0 = nvalid

    def step(_, carry):
        lo, hi, chi, clo = carry
        mid = 0.5 * lo + 0.5 * hi
        c = _count_ge(x, mid)
        ge = c >= kf
        return jnp.where(ge, mid, lo), jnp.where(ge, hi, mid), jnp.where(ge, chi, c), jnp.where(ge, c, clo)

    def finish(hi, chi):
        tau = jnp.max(jnp.where(x < hi, x, -jnp.inf), axis=1, keepdims=True)
        ceq = jnp.sum(jnp.where(x == tau, 1.0, 0.0), axis=1, keepdims=True)
        return tau, ceq

    def not_done(hi, chi, clo):
        tau, ceq = finish(hi, chi)
        bad = jnp.where(few | (clo == kf) | (chi + ceq >= kf), 0.0, 1.0)
        return jnp.max(bad) > 0.0

    def exact_cond(carry):
        it, _, _, _, clo = carry
        open_rows = jnp.max(jnp.where(few | (clo == kf), 0.0, 1.0)) > 0.0
        return jnp.logical_and(it < 6, open_rows)

    def rounds(nsteps):
        def body(carry):
            it, lo, hi, chi, clo = carry
            lo, hi, chi, clo = lax.fori_loop(0, nsteps, step, (lo, hi, chi, clo))
            return it + 1, lo, hi, chi, clo
        return body

    _, lo, hi, chi, clo = lax.while_loop(exact_cond, rounds(4), (jnp.int32(0), lo0, hi0, chi0, clo0))

    def w_cond(carry):
        it, _, hi, chi, clo = carry
        open_rows = jnp.max(jnp.where(few | (clo == kf), 0.0, 1.0)) > 0.0
        return jnp.logical_and(open_rows, jnp.logical_and(it < 48, not_done(hi, chi, clo)))

    _, lo, hi, chi, clo = lax.while_loop(w_cond, rounds(8), (jnp.int32(0), lo, hi, chi, clo))
    exact = clo == kf
    tau, ceq = finish(hi, chi)
    tau = jnp.where(exact, lo, tau)
    need = jnp.where(exact, 0.0, kf - chi)
    gtf = jnp.where(x >= jnp.where(exact, lo, hi), 1.0, 0.0)
    eqf = jnp.where(jnp.logical_and(x == tau, jnp.logical_not(exact)), 1.0, 0.0)
    has_tie = jnp.max(jnp.where(few | exact | (ceq <= need), 0.0, 1.0)) > 0.0

    def tie_path(_):
        nchunk = width // LANES
        su = jnp.where(_iota((LANES, LANES), 0) < _iota((LANES, LANES), 1), 1.0, 0.0).astype(BF16)
        run = jnp.zeros_like(need)
        pieces = []
        for c in range(nchunk):
            e = eqf[:, c * LANES:(c + 1) * LANES]
            before = jnp.dot(e.astype(BF16), su, preferred_element_type=F32) + run
            pieces.append(jnp.where(before < need, e, 0.0))
            run = run + jnp.sum(e, axis=1, keepdims=True)
        return jnp.concatenate(pieces, axis=1)

    sel_eq = lax.cond(has_tie, tie_path, lambda _: eqf, 0)
    return jnp.where(few, validf, gtf + sel_eq)


def _dsa_prompt_kernel(qi_ref, smq_ref, sma_ref, qh_ref, kt_ref, v_ref, o_ref, *, n_sel, j0, qb):
    j = j0 + pl.program_id(1)
    S = kt_ref.shape[-1]
    qi = qi_ref[0]
    wi = smq_ref[0][:, _SM_IW:_SM_IW + IDX_HEADS] * (IDX_HEADS ** -0.5 * IDX_DIM ** -0.5)
    ki = sma_ref[0][:, _SM_IK:_SM_IK + IDX_DIM]
    s_all = _dot_nt(jnp.concatenate([qi[:, h * IDX_DIM:(h + 1) * IDX_DIM] for h in range(IDX_HEADS)], axis=0), ki)
    score = jnp.zeros((qb, S), F32)
    for h in range(IDX_HEADS):
        score = score + jnp.maximum(s_all[h * qb:(h + 1) * qb], 0.0) * wi[:, h:h + 1]
    tpos = j * qb + _iota((qb, 1), 0)
    valid = _iota((1, S), 1) <= tpos
    mask = _topk_mask(score, valid, n_sel)
    bias = jnp.where(mask > 0.0, 0.0, -jnp.inf)
    bias2 = jnp.concatenate([bias, bias], axis=0)
    G = DSA_HEADS // DSA_KV_HEADS
    outs = []
    for kv in range(DSA_KV_HEADS):
        q2 = qh_ref[0, G * kv:G * (kv + 1)].reshape(G * qb, DSA_DH) * (DSA_DH ** -0.5)
        s = _dot(q2, kt_ref[0, kv]) + bias2
        p = jnp.exp(s - jnp.max(s, axis=1, keepdims=True))
        o = _dot(p, v_ref[0, kv]) / jnp.sum(p, axis=1, keepdims=True)
        outs += [o[g * qb:(g + 1) * qb] for g in range(G)]
    o_ref[0] = jnp.concatenate(outs, axis=1)


def _dsa_prompt(qi, small, qh, kt, vh, n_sel, qb):
    B, T, _ = qi.shape
    nq = T // qb
    nseg = 8 if nq % 8 == 0 else (4 if nq % 4 == 0 else 1)
    qps = nq // nseg
    outs = []
    for seg in range(nseg):
        j0 = seg * qps
        S = (seg + 1) * qps * qb
        outs.append(pl.pallas_call(
            functools.partial(_dsa_prompt_kernel, n_sel=n_sel, j0=j0, qb=qb),
            grid=(B, qps),
            in_specs=[pl.BlockSpec((1, qb, IDX_HEADS * IDX_DIM), lambda b, j, j0=j0: (b, j0 + j, 0)),
                      pl.BlockSpec((1, qb, LANES), lambda b, j, j0=j0: (b, j0 + j, 0)),
                      pl.BlockSpec((1, S, LANES), lambda b, j: (b, 0, 0)),
                      pl.BlockSpec((1, DSA_HEADS, qb, DSA_DH), lambda b, j, j0=j0: (b, 0, j0 + j, 0)),
                      pl.BlockSpec((1, DSA_KV_HEADS, DSA_DH, S), lambda b, j: (b, 0, 0, 0)),
                      pl.BlockSpec((1, DSA_KV_HEADS, S, DSA_DH), lambda b, j: (b, 0, 0, 0))],
            out_specs=pl.BlockSpec((1, qb, DSA_HEADS * DSA_DH), lambda b, j: (b, j, 0)),
            out_shape=jax.ShapeDtypeStruct((B, qps * qb, DSA_HEADS * DSA_DH), F32),
            compiler_params=_cparams("parallel", "arbitrary"),
            name="dsa_prompt",
        )(qi, small, small, qh, kt, vh))
    return outs[0] if nseg == 1 else jnp.concatenate(outs, axis=1)


def _dsa_sample_score_kernel(pt_ref, qi_ref, wi_ref, kit_hbm, o_ref, buf, sem, *, n_pages):
    b = pl.program_id(0)

    def fetch(sample, slot):
        def issue(p, carry):
            pltpu.make_async_copy(kit_hbm.at[pt_ref[sample, p]], buf.at[slot, p], sem.at[slot]).start()
            return carry

        lax.fori_loop(0, n_pages, issue, 0, unroll=8)

    @pl.when(b == 0)
    def _():
        fetch(0, 0)

    @pl.when(b + 1 < pl.num_programs(0))
    def _():
        fetch(b + 1, (b + 1) % 2)

    slot = b % 2
    pltpu.make_async_copy(kit_hbm.at[pl.ds(0, n_pages)], buf.at[slot], sem.at[slot]).wait()
    qi = qi_ref[0]
    wi = wi_ref[0] * (IDX_HEADS ** -0.5 * IDX_DIM ** -0.5)
    for p in range(n_pages):
        s = _dot(qi, buf[slot, p])
        o_ref[0, :, p * PAGE_SIZE:(p + 1) * PAGE_SIZE] = jnp.sum(jnp.maximum(s, 0.0) * wi, axis=0, keepdims=True)


def _dsa_sample_scores(page_table, qi8, wi8, cache_kit):
    Bd, n_pages = page_table.shape
    grid_spec = pltpu.PrefetchScalarGridSpec(
        num_scalar_prefetch=1,
        grid=(Bd,),
        in_specs=[pl.BlockSpec((1, 8, IDX_DIM), lambda b, pt: (b, 0, 0)),
                  pl.BlockSpec((1, 8, 1), lambda b, pt: (b, 0, 0)),
                  pl.BlockSpec(memory_space=pl.ANY)],
        out_specs=pl.BlockSpec((1, 1, n_pages * PAGE_SIZE), lambda b, pt: (b, 0, 0)),
        scratch_shapes=[pltpu.VMEM((2, n_pages, IDX_DIM, PAGE_SIZE), F32), pltpu.SemaphoreType.DMA((2,))],
    )
    return pl.pallas_call(
        functools.partial(_dsa_sample_score_kernel, n_pages=n_pages),
        grid_spec=grid_spec,
        out_shape=jax.ShapeDtypeStruct((Bd, 1, n_pages * PAGE_SIZE), F32),
        compiler_params=_cparams("arbitrary"),
        name="dsa_sample_scores",
    )(page_table, qi8, wi8, cache_kit)


def _dsa_sample_select_kernel(sc_ref, qi_ref, sm_ref, sel_ref, seln_ref, *, n_sel):
    Bd, past = sc_ref.shape
    width = past + LANES
    sm = sm_ref[...]
    qi = _bf(qi_ref[...]).astype(F32)
    ki = _bf(sm[:, _SM_IK:_SM_IK + IDX_DIM]).astype(F32)
    wi = sm[:, _SM_IW:_SM_IW + IDX_HEADS] * (IDX_HEADS ** -0.5 * IDX_DIM ** -0.5)
    snew = jnp.zeros((Bd, 1), F32)
    for h in range(IDX_HEADS):
        s = jnp.sum(qi[:, h * IDX_DIM:(h + 1) * IDX_DIM] * ki, axis=1, keepdims=True)
        snew = snew + jnp.maximum(s, 0.0) * wi[:, h:h + 1]
    tail = jnp.where(_iota((Bd, LANES), 1) == 0, snew, -jnp.inf)
    x = jnp.concatenate([sc_ref[...], tail], axis=1)
    valid = jnp.broadcast_to(_iota((1, width), 1) <= past, (Bd, width))
    mask = _topk_mask(x, valid, n_sel)
    sel_ref[...] = mask[:, :past]
    seln_ref[...] = mask[:, past:]


def _dsa_sample_select(scores, qi, small, n_sel):
    Bd, past = scores.shape
    return pl.pallas_call(
        functools.partial(_dsa_sample_select_kernel, n_sel=n_sel),
        out_shape=[jax.ShapeDtypeStruct((Bd, past), F32), jax.ShapeDtypeStruct((Bd, LANES), F32)],
        compiler_params=pltpu.CompilerParams(vmem_limit_bytes=VMEM_LIMIT_BYTES),
        name="dsa_sample_select",
    )(scores, qi, small)


_MASKED = -1e30


def _dsa_sample_attn_kernel(pt_ref, q_ref, sel_ref, seln_ref, kn_ref, vn_ref, kt_hbm, vt_hbm, o_ref,
                            kbuf, vbuf, sem, m_scr, l_scr, acc_scr, *, n_pg):
    jj = pl.program_id(1)
    njj = pl.num_programs(1)
    step = pl.program_id(0) * njj + jj
    G = DSA_HEADS // DSA_KV_HEADS
    row_kv = _iota((DSA_HEADS, DSA_DH), 0) // G

    def fetch(st, slot):
        sample = st // njj
        first = (st % njj) * n_pg

        def issue(p, carry):
            page = pt_ref[sample, first + p]
            pltpu.make_async_copy(kt_hbm.at[page], kbuf.at[slot, p], sem.at[0, slot]).start()
            pltpu.make_async_copy(vt_hbm.at[page], vbuf.at[slot, p], sem.at[1, slot]).start()
            return carry

        lax.fori_loop(0, n_pg, issue, 0, unroll=8)

    @pl.when(step == 0)
    def _():
        fetch(0, 0)

    @pl.when(step + 1 < pl.num_programs(0) * njj)
    def _():
        fetch(step + 1, (step + 1) % 2)

    slot = step % 2
    pltpu.make_async_copy(kt_hbm.at[pl.ds(0, n_pg)], kbuf.at[slot], sem.at[0, slot]).wait()
    pltpu.make_async_copy(vt_hbm.at[pl.ds(0, n_pg)], vbuf.at[slot], sem.at[1, slot]).wait()

    @pl.when(jj == 0)
    def _():
        m_scr[...] = jnp.full_like(m_scr, _MASKED)
        l_scr[...] = jnp.zeros_like(l_scr)
        acc_scr[...] = jnp.zeros_like(acc_scr)

    def block(kts, vts, msk):
        s = jnp.dot(_bf(q_ref[0, 0]), kts[0], preferred_element_type=F32)
        for kv in range(1, DSA_KV_HEADS):
            s = s + jnp.dot(_bf(q_ref[0, kv]), kts[kv], preferred_element_type=F32)
        s = s * (DSA_DH ** -0.5)
        on = msk > 0.0
        m_old = m_scr[...]
        m_new = jnp.maximum(m_old, jnp.max(jnp.where(on, s, _MASKED), axis=1, keepdims=True))
        alpha = jnp.exp(m_old - m_new)
        p = jnp.where(on, jnp.exp(s - m_new), 0.0)
        l_scr[...] = l_scr[...] * alpha + jnp.sum(p, axis=1, keepdims=True)
        pb = _bf(p)
        acc = acc_scr[...] * alpha
        for kv in range(DSA_KV_HEADS):
            o_kv = lax.dot_general(pb, vts[kv], _NT, preferred_element_type=F32)
            acc = acc + jnp.where(row_kv == kv, o_kv, 0.0)
        acc_scr[...] = acc
        m_scr[...] = m_new

    gather = lambda pages, kv: jnp.concatenate([_bf(pages[slot, p, kv]) for p in range(n_pg)], axis=1)
    block([gather(kbuf, kv) for kv in range(DSA_KV_HEADS)], [gather(vbuf, kv) for kv in range(DSA_KV_HEADS)],
          sel_ref[0])

    @pl.when(jj == njj - 1)
    def _():
        lanes = lambda col: _bf(jnp.broadcast_to(col, (DSA_DH, PAGE_SIZE)))
        block([lanes(kn_ref[0, kv]) for kv in range(DSA_KV_HEADS)], [lanes(vn_ref[0, kv]) for kv in range(DSA_KV_HEADS)],
              seln_ref[0])
        o_ref[0] = acc_scr[...] / l_scr[...]


def _dsa_sample_attn(page_table, q8, sel, sel_new, k_new_col, v_new_col, cache_kt, cache_vt, n_pg):
    Bd, n_pages = page_table.shape
    new_tok = pl.BlockSpec((1, DSA_KV_HEADS, DSA_DH, 1), lambda b, jj, pt: (b, 0, 0, 0))
    pages = pltpu.VMEM((2, n_pg, DSA_KV_HEADS, DSA_DH, PAGE_SIZE), F32)
    grid_spec = pltpu.PrefetchScalarGridSpec(
        num_scalar_prefetch=1,
        grid=(Bd, n_pages // n_pg),
        in_specs=[pl.BlockSpec((1, DSA_KV_HEADS, DSA_HEADS, DSA_DH), lambda b, jj, pt: (b, 0, 0, 0)),
                  pl.BlockSpec((1, 1, n_pg * PAGE_SIZE), lambda b, jj, pt: (b, 0, jj)),
                  pl.BlockSpec((1, 1, PAGE_SIZE), lambda b, jj, pt: (b, 0, 0)), new_tok, new_tok,
                  pl.BlockSpec(memory_space=pl.ANY), pl.BlockSpec(memory_space=pl.ANY)],
        out_specs=pl.BlockSpec((1, DSA_HEADS, DSA_DH), lambda b, jj, pt: (b, 0, 0)),
        scratch_shapes=[pages, pages, pltpu.SemaphoreType.DMA((2, 2)),
                        pltpu.VMEM((DSA_HEADS, 1), F32), pltpu.VMEM((DSA_HEADS, 1), F32),
                        pltpu.VMEM((DSA_HEADS, DSA_DH), F32)],
    )
    return pl.pallas_call(
        functools.partial(_dsa_sample_attn_kernel, n_pg=n_pg),
        grid_spec=grid_spec,
        out_shape=jax.ShapeDtypeStruct((Bd, DSA_HEADS, DSA_DH), F32),
        compiler_params=_cparams("arbitrary", "arbitrary"),
        name="dsa_sample_attn",
    )(page_table, q8, sel.reshape(Bd, 1, -1), sel_new.reshape(Bd, 1, PAGE_SIZE), k_new_col, v_new_col, cache_kt,
      cache_vt)


def _attend_rows(q, kk, vv, scale):
    s = jnp.sum(kk * q[None], axis=-1, keepdims=True) * scale
    m = jnp.max(s, axis=0, keepdims=True)
    p = jnp.exp(s - m)
    l = jnp.sum(p, axis=0)
    return jnp.sum(p * vv, axis=0) / l


def _mem_prompt_kernel(q_ref, mk_ref, mv_ref, o_ref):
    q = q_ref[0]
    mk = mk_ref[0]
    mv = mv_ref[0]
    outs = []
    for h in range(MEM_HEADS):
        sl = slice(h * MEM_DH, (h + 1) * MEM_DH)
        s = _dot_nt(q[:, sl], mk[:, sl]) * (MEM_DH ** -0.5)
        m = jnp.max(s, axis=1, keepdims=True)
        p = jnp.exp(s - m)
        p = p / jnp.sum(p, axis=1, keepdims=True)
        outs.append(_dot(p, mv[:, sl]))
    o_ref[0] = jnp.concatenate(outs, axis=1)


def _mem_prompt(q, mk, mv, tq):
    B, T, W = q.shape
    M = mk.shape[1]
    kv = pl.BlockSpec((1, M, W), lambda b, i: (b, 0, 0))
    return pl.pallas_call(
        _mem_prompt_kernel,
        grid=(B, T // tq),
        in_specs=[pl.BlockSpec((1, tq, W), lambda b, i: (b, i, 0)), kv, kv],
        out_specs=pl.BlockSpec((1, tq, W), lambda b, i: (b, i, 0)),
        out_shape=jax.ShapeDtypeStruct((B, T, W), F32),
        compiler_params=_cparams("parallel", "parallel"),
        name="mem_prompt",
    )(q, mk, mv)


def _mem_sample_kernel(q_ref, k_ref, v_ref, o_ref, *, ns):
    for i in range(ns):
        o_ref[i] = _attend_rows(q_ref[i], k_ref[i], v_ref[i], MEM_DH ** -0.5)


def _mem_sample(q, ck, cv, ns):
    Bd, M = ck.shape[:2]
    kv = pl.BlockSpec((ns, M, MEM_HEADS, MEM_DH), lambda b: (b, 0, 0, 0))
    qs = pl.BlockSpec((ns, MEM_HEADS, MEM_DH), lambda b: (b, 0, 0))
    return pl.pallas_call(
        functools.partial(_mem_sample_kernel, ns=ns),
        grid=(Bd // ns,),
        in_specs=[qs, kv, kv],
        out_specs=qs,
        out_shape=jax.ShapeDtypeStruct((Bd, MEM_HEADS, MEM_DH), F32),
        compiler_params=_cparams("parallel"),
        name="mem_sample",
    )(q, ck, cv)


def _merge_kernel(x_ref, og_ref, od_ref, om_ref, gn_ref, wg_ref, wb_ref, wo_ref, fn_ref, xo_ref, h2_ref):
    x = x_ref[...]
    d = x.shape[1]
    hb = _rms(x, gn_ref[...]).astype(BF16)
    acc = jnp.zeros_like(x)
    for n, o_ref in enumerate((og_ref, od_ref, om_ref)):
        gate = _sigmoid(jnp.dot(hb, wg_ref[:, n * d:(n + 1) * d], preferred_element_type=F32))
        acc = acc + gate * jnp.dot(_bf(o_ref[...]), wb_ref[n], preferred_element_type=F32)
    xo = x + jnp.dot(_bf(acc), wo_ref[...], preferred_element_type=F32)
    _tt_store(xo_ref, xo)
    _tt_store(h2_ref, _rms(xo, fn_ref[...]))


def _merge(x, o_g, o_d, o_m, gn, wg, wb, wo, fn, tm):
    n, d = x.shape
    tok = lambda w: pl.BlockSpec((tm, w), lambda i: (i, 0))
    full = lambda shape: pl.BlockSpec(shape, lambda i: (0,) * len(shape))
    return pl.pallas_call(
        _merge_kernel,
        grid=(n // tm,),
        in_specs=[tok(d), tok(BRANCH_W), tok(BRANCH_W), tok(BRANCH_W), full((1, d)), full((d, N_BRANCH * d)),
                  full((N_BRANCH, BRANCH_W, d)), full((d, d)), full((1, d))],
        out_specs=[pl.BlockSpec((tm * TT, LANES), lambda i: (i, 0))] * 2,
        out_shape=[jax.ShapeDtypeStruct((n * TT, LANES), F32)] * 2,
        compiler_params=_cparams("parallel"),
        name="merge",
    )(x, o_g, o_d, o_m, gn.reshape(1, d), wg, wb, wo, fn.reshape(1, d))


def _route_kernel(h_ref, wr_ref, br_ref, e_ref, p_ref, r_ref, cnt_ref, run_scr, *, tr):
    i = pl.program_id(0)

    @pl.when(i == 0)
    def _():
        run_scr[...] = jnp.zeros_like(run_scr)

    logits = _dot_nt(wr_ref[...], _tt_load(h_ref, tr)) + br_ref[...]
    eidx = _iota((N_EXPERTS, LANES), 0)
    su = jnp.where(_iota((LANES, LANES), 0) < _iota((LANES, LANES), 1), 1.0, 0.0).astype(BF16)
    run = run_scr[...]
    for c in range(tr // LANES):
        sl = slice(c * LANES, (c + 1) * LANES)
        l = logits[:, sl]
        vals, idxs = [], []
        for _ in range(TOP_K):
            m = jnp.max(l, axis=0, keepdims=True)
            idx = jnp.min(jnp.where(l == m, eidx, N_EXPERTS), axis=0, keepdims=True)
            vals.append(m)
            idxs.append(idx)
            l = jnp.where(eidx == idx, -jnp.inf, l)
        ex = [jnp.exp(v - vals[0]) for v in vals]
        den = ex[0] + ex[1] + ex[2] + ex[3]
        oh = jnp.zeros((N_EXPERTS, LANES), F32)
        for k in range(TOP_K):
            p_ref[k:k + 1, sl] = ex[k] / den
            e_ref[k:k + 1, sl] = idxs[k]
            oh = oh + jnp.where(eidx == idxs[k], 1.0, 0.0)
        before = jnp.dot(oh.astype(BF16), su, preferred_element_type=F32) + run
        for k in range(TOP_K):
            rk = jnp.sum(jnp.where(eidx == idxs[k], before, 0.0), axis=0, keepdims=True)
            r_ref[k:k + 1, sl] = rk.astype(I32)
        run = run + jnp.sum(oh, axis=1, keepdims=True)
    run_scr[...] = run
    cnt_ref[...] = jnp.broadcast_to(run, cnt_ref.shape)


def _route(h2, wr_t, br, tr):
    n = h2.shape[0] // TT
    d = TT * LANES
    tokrow = pl.BlockSpec((TOP_K, tr), lambda i: (0, i))
    return pl.pallas_call(
        functools.partial(_route_kernel, tr=tr),
        grid=(n // tr,),
        in_specs=[pl.BlockSpec((tr * TT, LANES), lambda i: (i, 0)),
                  pl.BlockSpec((N_EXPERTS, d), lambda i: (0, 0)),
                  pl.BlockSpec((N_EXPERTS, 1), lambda i: (0, 0))],
        out_specs=[tokrow, tokrow, tokrow, pl.BlockSpec((N_EXPERTS, LANES), lambda i: (0, 0))],
        out_shape=[jax.ShapeDtypeStruct((TOP_K, n), I32), jax.ShapeDtypeStruct((TOP_K, n), F32),
                   jax.ShapeDtypeStruct((TOP_K, n), I32), jax.ShapeDtypeStruct((N_EXPERTS, LANES), F32)],
        scratch_shapes=[pltpu.VMEM((N_EXPERTS, 1), F32)],
        compiler_params=_cparams("arbitrary"),
        name="moe_route",
    )(h2, wr_t, br)


def _dispatch_kernel(plo_ref, pn_ref, nb_ref, dest_ref, h_ref, xg_hbm, zblk, sem, *, td):
    blk = zblk.shape[0] // TT
    n_blocks = xg_hbm.shape[0] // (blk * TT)
    tile = lambda i: pl.ds(pl.multiple_of(i * TT, TT), TT)

    def issue(t, carry):
        for k in range(TOP_K):
            pltpu.make_async_copy(h_ref.at[tile(t)], xg_hbm.at[tile(dest_ref[k, t])], sem.at[0]).start()
        return carry

    lax.fori_loop(0, td, issue, 0, unroll=8)

    @pl.when(pl.program_id(0) == 0)
    def _():
        zblk[...] = jnp.zeros_like(zblk)

        def zero_block(i):
            rows = pl.ds(pl.multiple_of(i * (blk * TT), blk * TT), blk * TT)
            return pltpu.make_async_copy(zblk, xg_hbm.at[rows], sem.at[2])

        def tail_start(i, c):
            zero_block(i).start()
            return c

        def tail_wait(i, c):
            zero_block(0).wait()
            return c

        lax.fori_loop(nb_ref[0], n_blocks, tail_start, 0)
        lax.fori_loop(nb_ref[0], n_blocks, tail_wait, 0)

        pieces = [1 << i for i in reversed(range(blk.bit_length() - 1))]

        def per_expert(e, carry):
            n = pn_ref[e]
            for wait in (False, True):
                off = plo_ref[e]
                for size in pieces:
                    rows = pl.ds(pl.multiple_of(off * TT, TT), size * TT)
                    copy = pltpu.make_async_copy(zblk.at[pl.ds(0, size * TT)], xg_hbm.at[rows], sem.at[1])

                    @pl.when((n & size) != 0)
                    def _():
                        copy.wait() if wait else copy.start()

                    off = off + (n & size)
            return carry

        lax.fori_loop(0, N_EXPERTS, per_expert, 0)

    for k in range(TOP_K):
        pltpu.make_async_copy(h_ref, xg_hbm.at[pl.ds(0, td * TT)], sem.at[0]).wait()


def _dispatch(h2, dest, pad_lo, pad_n, nb_used, n_slots, blk, td):
    n = h2.shape[0] // TT
    grid_spec = pltpu.PrefetchScalarGridSpec(
        num_scalar_prefetch=3,
        grid=(n // td,),
        in_specs=[pl.BlockSpec((TOP_K, td), lambda i, plo, pn, nb: (0, i), memory_space=pltpu.SMEM),
                  pl.BlockSpec((td * TT, LANES), lambda i, plo, pn, nb: (i, 0))],
        out_specs=pl.BlockSpec(memory_space=pl.ANY),
        scratch_shapes=[pltpu.VMEM((blk * TT, LANES), F32), pltpu.SemaphoreType.DMA((3,))],
    )
    return pl.pallas_call(
        functools.partial(_dispatch_kernel, td=td),
        grid_spec=grid_spec,
        out_shape=jax.ShapeDtypeStruct((n_slots * TT, LANES), F32),
        compiler_params=_cparams("arbitrary"),
        name="moe_dispatch",
    )(pad_lo, pad_n, nb_used, dest, h2)


def _expert_kernel(be_ref, nb_ref, x_ref, wgu_ref, bgu_ref, wdn_ref, bdn_ref, o_ref):
    del be_ref
    f = wdn_ref.shape[1]
    blk = x_ref.shape[0] // TT

    used = pl.program_id(0) < nb_ref[0]

    @pl.when(jnp.logical_not(used))
    def _():
        o_ref[...] = jnp.zeros_like(o_ref)

    @pl.when(used)
    def _():
        gu = jnp.dot(_bf(_tt_load(x_ref, blk)), _bf(wgu_ref[0]), preferred_element_type=F32) + bgu_ref[0]
        gate = jnp.minimum(gu[:, :f], SWIGLU_LIMIT)
        up = jnp.clip(gu[:, f:], -SWIGLU_LIMIT, SWIGLU_LIMIT)
        glu = gate * _sigmoid(SWIGLU_ALPHA * gate)
        _tt_store(o_ref, jnp.dot(_bf((up + 1.0) * glu), _bf(wdn_ref[0]), preferred_element_type=F32) + bdn_ref[0])


def _experts(xg, block_e, nb_used, wgu, bgu, wdn, bdn, blk):
    n_slots = xg.shape[0] // TT
    d = TT * LANES
    f = wdn.shape[1]
    blk_of = lambda i, nb: jnp.minimum(i, nb[0] - 1)
    tok = pl.BlockSpec((blk * TT, LANES), lambda i, be, nb: (blk_of(i, nb), 0))
    per_e = lambda shape: pl.BlockSpec((1,) + shape, lambda i, be, nb: (be[blk_of(i, nb)], 0, 0))
    grid_spec = pltpu.PrefetchScalarGridSpec(
        num_scalar_prefetch=2,
        grid=(n_slots // blk,),
        in_specs=[tok, per_e((d, 2 * f)), per_e((1, 2 * f)), per_e((f, d)), per_e((1, d))],
        out_specs=pl.BlockSpec((blk * TT, LANES), lambda i, be, nb: (i, 0)),
    )
    return pl.pallas_call(
        _expert_kernel,
        grid_spec=grid_spec,
        out_shape=jax.ShapeDtypeStruct((n_slots * TT, LANES), F32),
        compiler_params=_cparams("arbitrary"),
        name="moe_experts",
    )(block_e, nb_used, xg, wgu, bgu, wdn, bdn)


def _combine_kernel(dest_ref, x_ref, p_ref, g_ref, yb_hbm, o_ref, buf, sem, *, tc):
    tile = lambda i: pl.ds(pl.multiple_of(i * TT, TT), TT)

    def issue(t, carry):
        for k in range(TOP_K):
            pltpu.make_async_copy(yb_hbm.at[tile(dest_ref[k, t])], buf.at[k, tile(t)], sem.at[0]).start()
        return carry

    lax.fori_loop(0, tc, issue, 0, unroll=8)
    for k in range(TOP_K):
        pltpu.make_async_copy(yb_hbm.at[pl.ds(0, tc * TT)], buf.at[k], sem.at[0]).wait()
    p = p_ref[...]
    acc = p[:, 0:1] * _tt_load(buf.at[0], tc)
    for k in range(1, TOP_K):
        acc = acc + p[:, k:k + 1] * _tt_load(buf.at[k], tc)
    o_ref[...] = _rms(_tt_load(x_ref, tc) + acc, g_ref[...])


def _combine(x, yb, dest, p_t, g, tc):
    n = x.shape[0] // TT
    d = TT * LANES
    return pl.pallas_call(
        functools.partial(_combine_kernel, tc=tc),
        grid=(n // tc,),
        in_specs=[pl.BlockSpec((TOP_K, tc), lambda i: (0, i), memory_space=pltpu.SMEM),
                  pl.BlockSpec((tc * TT, LANES), lambda i: (i, 0)),
                  pl.BlockSpec((tc, TOP_K), lambda i: (i, 0)),
                  pl.BlockSpec((1, d), lambda i: (0, 0)),
                  pl.BlockSpec(memory_space=pl.ANY)],
        out_specs=pl.BlockSpec((tc, d), lambda i: (i, 0)),
        out_shape=jax.ShapeDtypeStruct((n, d), F32),
        scratch_shapes=[pltpu.VMEM((TOP_K, tc * TT, LANES), F32), pltpu.SemaphoreType.DMA((1,))],
        compiler_params=_cparams("arbitrary"),
        name="moe_combine",
    )(dest, x, p_t, g.reshape(1, d), yb)


def _moe_final(x, h2, g_final, wr_t, br, wgu, bgu, wdn, bdn, blk, tile):
    n = x.shape[0] // TT
    e, p, rank, cnt = _route(h2, wr_t, br, tile)
    counts = cnt[:, 0].astype(I32)
    padded = (counts + blk - 1) // blk * blk
    pad_end = jnp.cumsum(padded)
    pad_start = pad_end - padded
    n_blocks = -(-n * TOP_K // blk) + N_EXPERTS
    expert_ids = jnp.arange(N_EXPERTS, dtype=I32)[:, None, None]
    dest = rank + jnp.sum(jnp.where(e[None] == expert_ids, pad_start[:, None, None], 0), axis=0)
    block_lo = jnp.arange(n_blocks, dtype=I32)[:, None] * blk
    block_e = jnp.minimum(jnp.sum((pad_end[None, :] <= block_lo).astype(I32), axis=1), N_EXPERTS - 1)
    nb_used = (pad_end[-1:] // blk).astype(I32)
    xg = _dispatch(h2, dest, pad_start + counts, padded - counts, nb_used, n_blocks * blk, blk, tile)
    yb = _experts(xg, block_e, nb_used, wgu, bgu, wdn, bdn, blk)
    return _combine(x, yb, dest, p.T, g_final, min(tile, 256))


def _prep_w_in(w):
    cuts = np.cumsum((0,) + _SPLITS)
    seg = lambda i: w[:, int(cuts[i]):int(cuts[i + 1])]
    small = jnp.concatenate([seg(8), seg(9), seg(2), seg(3), jnp.zeros((w.shape[0], LANES - _SM_END), w.dtype)], axis=1)
    w1 = jnp.concatenate([seg(0), seg(1), seg(4), seg(5), seg(6), seg(7), seg(10), small], axis=1)
    return w1.astype(BF16), w[:, int(cuts[-1]):].astype(BF16)


def kernel(x_prompt, x_sample, cache_k, cache_v, cache_k_idx, cache_mem_k, cache_mem_v, state_gdn, state_conv,
           page_table, mem_prompt, norm_attn, w_in, conv_w, gdn_a_log, gdn_dt_bias, gdn_norm, norm_mem, w_mem_kv,
           w_branch, w_out, norm_ffn, w_router, b_router, w_gate_up, b_gate_up, w_down, b_down, norm_final):
    B, T, D = x_prompt.shape
    Bd, Ts, _ = x_sample.shape
    assert Ts == 1 and w_in.shape[0] == 1, "one layer, one new token per sample"
    H = GDN_HEADS
    G = DSA_HEADS // DSA_KV_HEADS
    n_pages = page_table.shape[1]
    M = mem_prompt.shape[1]

    w1, wg = _prep_w_in(w_in[0])
    wb = w_branch[0].astype(BF16)
    wo = w_out[0].astype(BF16)
    wr_t = w_router[0].T
    br = b_router[0].reshape(N_EXPERTS, 1)
    wgu = w_gate_up[0]
    wdn = w_down[0]
    bgu = b_gate_up[0].reshape(N_EXPERTS, 1, -1)
    bdn = b_down[0].reshape(N_EXPERTS, 1, -1)
    alog = gdn_a_log[0]
    dtb = gdn_dt_bias[0]

    xp = x_prompt.reshape(B * T, D)
    g_qkv, g_z, d_k, d_v, i_q, m_q, small, qh, kt, vh = _proj_prompt(xp, norm_attn[0], w1, B, T, min(512, T))

    quad_rows = lambda a: (a.reshape(B, T // GDN_CHUNK, GDN_CHUNK, 2, 4).transpose(0, 1, 3, 4, 2)
                           .reshape(B, T // GDN_CHUNK, 2, 1, QW))
    per_group = lambda a: jnp.repeat(a, GDN_DK).reshape(2, 1, QW)
    o_g, ssm_q = _gdn_prompt(g_qkv.reshape(B, T, GDN_QKV), g_z.reshape(B, T, H * GDN_DV),
                             quad_rows(small[:, _SM_GA:_SM_GA + H]), quad_rows(small[:, _SM_GB:_SM_GB + H]),
                             conv_w[0], per_group(alog), per_group(dtb), jnp.tile(gdn_norm[0], 4).reshape(1, QW),
                             min(512, T))
    o_g = o_g.reshape(B * T, H * GDN_DV)
    ssm_p = ssm_q.reshape(B, 2, GDN_DK, 4, GDN_DV).transpose(0, 1, 3, 2, 4).reshape(B, H, GDN_DK, GDN_DV)

    o_d = _dsa_prompt(i_q.reshape(B, T, -1), small.reshape(B, T, LANES), qh, kt, vh, min(TOPK_MAX, T // 4),
                      min(256, T))
    o_d = o_d.reshape(B * T, DSA_HEADS * DSA_DH)

    mk, mv = _norm_matmul(mem_prompt.reshape(B * M, D), norm_mem[0], w_mem_kv[0].astype(BF16),
                          (MEM_HEADS * MEM_DH,) * 2, min(512, B * M))
    o_m = _mem_prompt(m_q.reshape(B, T, -1), mk.reshape(B, M, -1), mv.reshape(B, M, -1), min(512, T))
    o_m = o_m.reshape(B * T, MEM_HEADS * MEM_DH)

    xres, h2 = _merge(xp, o_g, o_d, o_m, norm_attn[0], wg, wb, wo, norm_ffn[0], min(256, B * T))
    y_prompt = _moe_final(xres, h2, norm_final, wr_t, br, wgu, bgu, wdn, bdn, 256, min(512, B * T))

    k_prompt = d_k.reshape(1, B, T, DSA_KV_HEADS, DSA_DH)
    v_prompt = d_v.reshape(1, B, T, DSA_KV_HEADS, DSA_DH)
    kidx_prompt = small[:, _SM_IK:_SM_IK + IDX_DIM].reshape(1, B, T, IDX_DIM)
    memk_prompt = mk.reshape(1, B, M, MEM_HEADS, MEM_DH)
    memv_prompt = mv.reshape(1, B, M, MEM_HEADS, MEM_DH)
    conv_prompt = g_qkv.reshape(B, T, GDN_QKV)[:, T - (GDN_CONV - 1):, :][None]

    xs = x_sample.reshape(Bd, D)
    s_qkv, s_z, sd_q, sd_k, sd_v, si_q, sm_q, ssmall = _norm_matmul(xs, norm_attn[0], w1, _PROJ_SPLITS, Bd)

    lanes_b = lambda a: jnp.broadcast_to(a[..., None], a.shape + (Bd,))
    og_t, s_t = _gdn_sample(
        s_qkv.T, state_conv[0].transpose(1, 2, 0), lanes_b(conv_w[0]), s_z.T,
        ssmall[:, _SM_GA:_SM_GA + H].T.reshape(H, 1, Bd), ssmall[:, _SM_GB:_SM_GB + H].T.reshape(H, 1, Bd),
        lanes_b(alog.reshape(H, 1)), lanes_b(dtb.reshape(H, 1)), lanes_b(gdn_norm[0]),
        state_gdn[0].transpose(1, 2, 3, 0))
    so_g = og_t.T
    ssm_sample = s_t.transpose(3, 0, 1, 2)[None]
    conv_sample = jnp.concatenate([state_conv[0][:, 1:], s_qkv[:, None, :]], axis=1)[None]

    qi8 = jnp.pad(si_q.reshape(Bd, IDX_HEADS, IDX_DIM), ((0, 0), (0, 8 - IDX_HEADS), (0, 0)))
    wi8 = jnp.pad(ssmall[:, _SM_IW:_SM_IW + IDX_HEADS], ((0, 0), (0, 8 - IDX_HEADS)))[..., None]
    n_pg = 32 if n_pages % 32 == 0 else n_pages
    cache_kit = cache_k_idx[0].transpose(0, 2, 1)
    cache_kt = cache_k[0].transpose(0, 2, 3, 1)
    cache_vt = cache_v[0].transpose(0, 2, 3, 1)
    scores = _dsa_sample_scores(page_table, qi8, wi8, cache_kit)
    scores = scores.reshape(Bd, n_pages * PAGE_SIZE)
    n_sel = min(TOPK_MAX, (n_pages * PAGE_SIZE + 1) // 4)
    sel, sel_new = _dsa_sample_select(scores, si_q, ssmall, n_sel)
    head_kv = jnp.arange(DSA_HEADS, dtype=I32) // G
    q8 = jnp.where((head_kv[None, :] == jnp.arange(DSA_KV_HEADS, dtype=I32)[:, None])[None, :, :, None],
                   sd_q.reshape(Bd, 1, DSA_HEADS, DSA_DH), 0.0)
    k_new = sd_k.reshape(Bd, DSA_KV_HEADS, DSA_DH)
    v_new = sd_v.reshape(Bd, DSA_KV_HEADS, DSA_DH)
    so_d = _dsa_sample_attn(page_table, q8, sel, sel_new, k_new[..., None], v_new[..., None],
                            cache_kt, cache_vt, n_pg).reshape(Bd, DSA_HEADS * DSA_DH)

    so_m = _mem_sample(sm_q.reshape(Bd, MEM_HEADS, MEM_DH), cache_mem_k[0], cache_mem_v[0], 4)
    so_m = so_m.reshape(Bd, MEM_HEADS * MEM_DH)

    sres, sh2 = _merge(xs, so_g, so_d, so_m, norm_attn[0], wg, wb, wo, norm_ffn[0], Bd)
    y_sample = _moe_final(sres, sh2, norm_final, wr_t, br, wgu, bgu, wdn, bdn, 256, Bd)

    return (y_prompt.reshape(B, T, D), y_sample.reshape(Bd, 1, D), k_prompt, v_prompt, kidx_prompt,
            memk_prompt, memv_prompt, ssm_p[None], conv_prompt,
            k_new.reshape(1, Bd, 1, DSA_KV_HEADS, DSA_DH), v_new.reshape(1, Bd, 1, DSA_KV_HEADS, DSA_DH),
            ssmall[:, _SM_IK:_SM_IK + IDX_DIM].reshape(1, Bd, 1, IDX_DIM), ssm_sample, conv_sample)
```

```python
import functools

import numpy as np
import jax
import jax.numpy as jnp
from jax import lax
from jax.experimental import pallas as pl
from jax.experimental.pallas import tpu as pltpu

F32 = jnp.float32
BF16 = jnp.bfloat16
I32 = jnp.int32

EPS = 1e-6
GDN_HEADS = 8
GDN_DK = 64
GDN_DV = 64
GDN_CONV = 4
GDN_CHUNK = 64
GDN_QKV = 2 * GDN_HEADS * GDN_DK + GDN_HEADS * GDN_DV
DSA_HEADS = 8
DSA_KV_HEADS = 4
DSA_DH = 64
IDX_HEADS = 4
IDX_DIM = 64
TOPK_MAX = 256
Q_BLOCK = 128
MEM_HEADS = 4
MEM_DH = 128
N_BRANCH = 3
BRANCH_W = 512
N_EXPERTS = 32
TOP_K = 4
SWIGLU_LIMIT = 7.0
SWIGLU_ALPHA = 1.702
PAGE_SIZE = 128
LANES = 128

_SPLITS = (GDN_QKV, GDN_HEADS * GDN_DV, GDN_HEADS, GDN_HEADS,
           DSA_HEADS * DSA_DH, DSA_KV_HEADS * DSA_DH, DSA_KV_HEADS * DSA_DH,
           IDX_HEADS * IDX_DIM, IDX_DIM, IDX_HEADS,
           MEM_HEADS * MEM_DH)
_PROJ_SPLITS = (GDN_QKV, 512, 512, 256, 256, 256, 512, LANES)
_SM_IK = 0
_SM_IW = IDX_DIM
_SM_GB = _SM_IW + IDX_HEADS
_SM_GA = _SM_GB + GDN_HEADS
_SM_END = _SM_GA + GDN_HEADS

VMEM_LIMIT_BYTES = 56 * 1024 * 1024


def _cparams(*sem):
    return pltpu.CompilerParams(dimension_semantics=sem, vmem_limit_bytes=VMEM_LIMIT_BYTES)


def _bf(x):
    return x.astype(BF16)


def _dot(a, b):
    return jnp.dot(_bf(a), _bf(b), preferred_element_type=F32)


_NT = (((1,), (1,)), ((), ()))


def _dot_nt(a, b):
    return lax.dot_general(_bf(a), _bf(b), _NT, preferred_element_type=F32)


def _split2(x):
    hi = x.astype(BF16)
    lo = (x - hi.astype(F32)).astype(BF16)
    return hi, lo


def _split3(x):
    hi = x.astype(BF16)
    r = x - hi.astype(F32)
    mid = r.astype(BF16)
    lo = (r - mid.astype(F32)).astype(BF16)
    return hi, mid, lo


def _dot_exact01(a, b01):
    r = a.shape[0]
    parts = jnp.dot(jnp.concatenate(_split3(a), axis=0), b01, preferred_element_type=F32)
    return parts[:r] + (parts[r:2 * r] + parts[2 * r:])


def _rms(x, g):
    return x * lax.rsqrt(jnp.mean(x * x, axis=-1, keepdims=True) + EPS) * g


def _sigmoid(x):
    return 1.0 / (1.0 + jnp.exp(-x))


def _silu(x):
    return x * _sigmoid(x)


def _softplus(x):
    return jnp.maximum(x, 0.0) + jnp.log(1.0 + jnp.exp(-jnp.abs(x)))


def _iota(shape, axis):
    return lax.broadcasted_iota(I32, shape, axis)


TT = 8


def _tt_load(ref, n):
    return jnp.concatenate([ref[pl.ds(c, n, stride=TT), :] for c in range(TT)], axis=1)


def _tt_store(ref, val):
    n = val.shape[0]
    for c in range(TT):
        ref[pl.ds(c, n, stride=TT), :] = val[:, c * LANES:(c + 1) * LANES]


def _norm_matmul_kernel(x_ref, g_ref, w_ref, *o_refs, splits):
    hb = _rms(x_ref[...], g_ref[...]).astype(BF16)
    off = 0
    for o_ref, n in zip(o_refs, splits):
        o_ref[...] = jnp.dot(hb, w_ref[:, off:off + n], preferred_element_type=F32)
        off += n


def _norm_matmul(x, g, w, splits, tm):
    n, d = x.shape
    return pl.pallas_call(
        functools.partial(_norm_matmul_kernel, splits=splits),
        grid=(n // tm,),
        in_specs=[pl.BlockSpec((tm, d), lambda i: (i, 0)),
                  pl.BlockSpec((1, d), lambda i: (0, 0)),
                  pl.BlockSpec((d, sum(splits)), lambda i: (0, 0))],
        out_specs=[pl.BlockSpec((tm, s), lambda i: (i, 0)) for s in splits],
        out_shape=[jax.ShapeDtypeStruct((n, s), F32) for s in splits],
        compiler_params=_cparams("parallel"),
        name="norm_matmul",
    )(x, g.reshape(1, d), w)


def _proj_prompt_kernel(x_ref, g_ref, w_ref, qkv_ref, z_ref, dk_ref, dv_ref, iq_ref, mq_ref, sm_ref,
                        qh_ref, kt_ref, vh_ref):
    hb = _rms(x_ref[...], g_ref[...]).astype(BF16)
    offs = np.cumsum((0,) + _PROJ_SPLITS)
    part = lambda i: jnp.dot(hb, w_ref[:, int(offs[i]):int(offs[i + 1])], preferred_element_type=F32)
    qkv_ref[...] = part(0)
    z_ref[...] = part(1)
    dq = part(2)
    for h in range(DSA_HEADS):
        qh_ref[0, h] = dq[:, h * DSA_DH:(h + 1) * DSA_DH]
    dk = part(3)
    dk_ref[...] = dk
    kt_ref[0] = dk.T.reshape(DSA_KV_HEADS, DSA_DH, dk.shape[0])
    dv = part(4)
    dv_ref[...] = dv
    for kv in range(DSA_KV_HEADS):
        vh_ref[0, kv] = dv[:, kv * DSA_DH:(kv + 1) * DSA_DH]
    iq_ref[...] = part(5)
    mq_ref[...] = part(6)
    sm_ref[...] = part(7)


def _proj_prompt(x, g, w, B, T, tm):
    n, d = x.shape
    tpb = T // tm
    tok = lambda width: pl.BlockSpec((tm, width), lambda b, t: (b * tpb + t, 0))
    flat = lambda width: jax.ShapeDtypeStruct((n, width), F32)
    sp = _PROJ_SPLITS
    return pl.pallas_call(
        _proj_prompt_kernel,
        grid=(B, tpb),
        in_specs=[tok(d), pl.BlockSpec((1, d), lambda b, t: (0, 0)), pl.BlockSpec((d, sum(sp)), lambda b, t: (0, 0))],
        out_specs=[tok(sp[0]), tok(sp[1]), tok(sp[3]), tok(sp[4]), tok(sp[5]), tok(sp[6]), tok(sp[7]),
                   pl.BlockSpec((1, DSA_HEADS, tm, DSA_DH), lambda b, t: (b, 0, t, 0)),
                   pl.BlockSpec((1, DSA_KV_HEADS, DSA_DH, tm), lambda b, t: (b, 0, 0, t)),
                   pl.BlockSpec((1, DSA_KV_HEADS, tm, DSA_DH), lambda b, t: (b, 0, t, 0))],
        out_shape=[flat(sp[0]), flat(sp[1]), flat(sp[3]), flat(sp[4]), flat(sp[5]), flat(sp[6]), flat(sp[7]),
                   jax.ShapeDtypeStruct((B, DSA_HEADS, T, DSA_DH), F32),
                   jax.ShapeDtypeStruct((B, DSA_KV_HEADS, DSA_DH, T), F32),
                   jax.ShapeDtypeStruct((B, DSA_KV_HEADS, T, DSA_DH), F32)],
        compiler_params=_cparams("parallel", "parallel"),
        name="proj_prompt",
    )(x, g.reshape(1, d), w)


QW = 4 * GDN_DK


def _bd_stack(x):
    lane_head = (_iota(x.shape, 1) >> 6) & 3
    return jnp.concatenate([jnp.where(lane_head == h, x, 0.0) for h in range(4)], axis=0)


def _bd_dot(a, x):
    return jnp.dot(_bf(a), _bf(_bd_stack(x)), preferred_element_type=F32)


def _bd_split(x):
    xh = x.astype(BF16).astype(F32)
    return _bf(_bd_stack(xh)), _bf(_bd_stack(x - xh))


def _bd_dot3(lhs, xb):
    bh, bl = xb
    r = lhs[0].shape[0]
    parts = [_split2(a) for a in lhs]
    hi = jnp.dot(jnp.concatenate([p for hl in parts for p in hl], axis=0), bh, preferred_element_type=F32)
    lo = jnp.dot(jnp.concatenate([hl[0] for hl in parts], axis=0), bl, preferred_element_type=F32)
    return [hi[2 * i * r:(2 * i + 1) * r] + (lo[i * r:(i + 1) * r] + hi[(2 * i + 1) * r:(2 * i + 2) * r])
            for i in range(len(lhs))]


def _gdn_prompt_kernel(q_ref, k_ref, v_ref, qh_ref, kh_ref, vh_ref, z_ref, a_ref, b_ref,
                       cwq_ref, cwk_ref, cwv_ref, alog_ref, dtb_ref, nw_ref, o_ref, s_ref,
                       xq, xk, xv, s_scr, *, tt):
    ti = pl.program_id(1)
    C = GDN_CHUNK
    HALO = 8
    NG = GDN_HEADS // 4
    CPI = 2

    @pl.when(ti == 0)
    def _():
        s_scr[...] = jnp.zeros_like(s_scr)

    r = _iota((QW, QW), 0)
    c = _iota((QW, QW), 1)
    same = (r >> 6) == (c >> 6)
    ones_bd = jnp.where(same, 1.0, 0.0).astype(BF16)
    su_bd = jnp.where(same, jnp.where((r & 63) > (c & 63), 1.0, 0.0), 0.0).astype(BF16)
    rhs01 = jnp.concatenate([su_bd, ones_bd], axis=1)
    ident = jnp.where(r == c, 1.0, 0.0).astype(BF16)

    keep = (ti > 0).astype(F32)
    for idx, (src, halo, dst, cw) in enumerate(((q_ref, qh_ref, xq, cwq_ref), (k_ref, kh_ref, xk, cwk_ref),
                                                (v_ref, vh_ref, xv, cwv_ref))):
        dst[0:HALO, :] = halo[0] * keep
        dst[HALO:HALO + tt, :] = src[0]
        w = cw[...]
        y = w[0:1, :] * dst[pl.ds(HALO - 3, tt), :]
        for i in range(1, GDN_CONV):
            y = y + w[i:i + 1, :] * dst[pl.ds(HALO - 3 + i, tt), :]
        y = _silu(y)
        if idx < 2:
            ss = jnp.concatenate([_dot_exact01((y * y)[:, g * QW:(g + 1) * QW], ones_bd) for g in range(NG)], axis=1)
            y = y * lax.rsqrt(ss + EPS)
            if idx == 0:
                y = y * (GDN_DK ** -0.5)
        dst[HALO:HALO + tt, :] = y

    ri = _iota((C, QW), 0)
    li = _iota((C, QW), 1) & 63
    lane_head = _iota((C, QW), 1) >> 6
    incl = li <= ri
    strict = li < ri
    eye = li == ri
    nw = nw_ref[...]

    def solve_steps(cis, out):
        streams = [(ci, g) for ci in cis for g in range(NG)]
        n = range(len(streams))
        rows = [slice(HALO + ci * C, HALO + (ci + 1) * C) for ci, _ in streams]
        cols = [slice(g * QW, (g + 1) * QW) for _, g in streams]
        qc = [xq[rows[i], cols[i]] for i in n]
        kc = [xk[rows[i], cols[i]] for i in n]
        vc = [xv[rows[i], cols[i]] for i in n]
        g_row = [-jnp.exp(alog_ref[g]) * _softplus(a_ref[0, ci, g] + dtb_ref[g]) for ci, g in streams]
        b_row = [_sigmoid(b_ref[0, ci, g]) for ci, g in streams]
        res = [_dot_exact01(jnp.concatenate([jnp.where(incl, g_row[i], 0.0), jnp.where(eye, b_row[i], 0.0)], axis=0),
                            rhs01) for i in n]
        yield
        gcol = [res[i][:C, QW:] for i in n]
        bcol = [res[i][C:, QW:] for i in n]
        gamma = [jnp.where(incl, jnp.exp(res[i][:C, :QW]), 0.0) for i in n]
        eg = [jnp.exp(gcol[i]) for i in n]
        glast = [gcol[i][C - 1:C, :] for i in n]
        kb = [kc[i] * bcol[i] for i in n]
        kt = [lax.dot_general(ident, jnp.concatenate([_bf(kc[i])] * 4, axis=0), _NT, preferred_element_type=F32)
              for i in n]
        yield
        kt = [_bf(jnp.where(same, kt[i], 0.0)) for i in n]
        kk = [jnp.dot(_bf(jnp.concatenate([kb[i], qc[i]], axis=0)), kt[i], preferred_element_type=F32)
              for i in n]
        yield
        r = [jnp.where(strict, -(kk[i][:C] * gamma[i]), 0.0) for i in n]
        t = [jnp.where(eye, 1.0, r[i]) for i in n]
        for k in range(6):
            rb = [_bd_split(r[i]) for i in n]
            prod = [_bd_dot3(([t[i]] if k > 0 else []) + ([r[i]] if k < 5 else []), rb[i]) for i in n]
            if k > 0:
                t = [t[i] + prod[i][0] for i in n]
            if k < 5:
                r = [prod[i][-1] for i in n]
            yield
        rhs = [_bd_split(jnp.concatenate([vc[i] * bcol[i], kb[i] * eg[i]], axis=1)) for i in n]
        y = [_bd_dot3([t[i]], rhs[i])[0] for i in n]
        out.extend(dict(ci=streams[i][0], g=streams[i][1], y=y[i], qg=qc[i] * eg[i], qk=kk[i][C:] * gamma[i],
                        kd=kc[i] * jnp.exp(glast[i] - gcol[i]), dec=jnp.exp(glast[i])) for i in n)

    def scan_steps(st):
        n = range(len(st))
        s = [s_scr[t["g"]] for t in st]
        ws = [_bd_dot(jnp.concatenate([st[i]["y"][:, QW:], st[i]["qg"]], axis=0), s[i]) for i in n]
        yield
        vn = [st[i]["y"][:, :QW] - ws[i][:C] for i in n]
        o = [ws[i][C:] + _bd_dot(st[i]["qk"], vn[i]) for i in n]
        kdt = [lax.dot_general(ident, _bf(st[i]["kd"]), _NT, preferred_element_type=F32) for i in n]
        yield
        full = [jnp.dot(_bf(kdt[i]), _bf(vn[i]), preferred_element_type=F32) for i in n]
        ms = [_dot_exact01(o[i] * o[i], ones_bd) * (1.0 / GDN_DV) for i in n]
        yield
        for i in n:
            g = st[i]["g"]
            sadd = jnp.where(lane_head == 0, full[i][0:C], 0.0)
            for h in range(1, 4):
                sadd = sadd + jnp.where(lane_head == h, full[i][h * C:(h + 1) * C], 0.0)
            s_scr[g] = s[i] * st[i]["dec"] + sadd
            rows = slice(st[i]["ci"] * C, (st[i]["ci"] + 1) * C)
            cols = slice(g * QW, (g + 1) * QW)
            o_ref[0, rows, cols] = o[i] * lax.rsqrt(ms[i] + EPS) * nw * _silu(z_ref[0, rows, cols])

    def scan_chunks(st):
        for j in range(len(st) // NG):
            yield from scan_steps(st[j * NG:(j + 1) * NG])

    def run(*gens):
        live = list(gens)
        while live:
            for gen in list(live):
                try:
                    next(gen)
                except StopIteration:
                    live.remove(gen)

    n_groups = tt // (CPI * C)
    solved = [[] for _ in range(n_groups)]
    run(solve_steps(range(CPI), solved[0]))
    for gi in range(1, n_groups):
        run(solve_steps(range(gi * CPI, (gi + 1) * CPI), solved[gi]), scan_chunks(solved[gi - 1]))
    run(scan_chunks(solved[-1]))
    s_ref[0] = s_scr[...]


def _gdn_prompt(qkv, z, a, b, cw, alog, dtb, nw, tt):
    B, T, _ = qkv.shape
    nct = tt // GDN_CHUNK
    NG = GDN_HEADS // 4
    W = NG * QW
    col = lambda j: pl.BlockSpec((1, tt, W), lambda bi, ti: (bi, ti, j))
    halo = lambda j: pl.BlockSpec((1, 8, W), lambda bi, ti: (bi, jnp.maximum(ti * (tt // 8) - 1, 0), j))
    chunked = pl.BlockSpec((1, nct, NG, 1, QW), lambda bi, ti: (bi, ti, 0, 0, 0))
    cwcol = lambda j: pl.BlockSpec((GDN_CONV, W), lambda bi, ti: (0, j))
    per_group = pl.BlockSpec((NG, 1, QW), lambda bi, ti: (0, 0, 0))
    return pl.pallas_call(
        functools.partial(_gdn_prompt_kernel, tt=tt),
        grid=(B, T // tt),
        in_specs=[col(0), col(1), col(2), halo(0), halo(1), halo(2), col(0), chunked, chunked,
                  cwcol(0), cwcol(1), cwcol(2), per_group, per_group,
                  pl.BlockSpec((1, QW), lambda bi, ti: (0, 0))],
        out_specs=[col(0), pl.BlockSpec((1, NG, GDN_DK, QW), lambda bi, ti: (bi, 0, 0, 0))],
        out_shape=[jax.ShapeDtypeStruct((B, T, W), F32), jax.ShapeDtypeStruct((B, NG, GDN_DK, QW), F32)],
        scratch_shapes=[pltpu.VMEM((8 + tt, W), F32)] * 3 + [pltpu.VMEM((NG, GDN_DK, QW), F32)],
        compiler_params=_cparams("parallel", "arbitrary"),
        name="gdn_prompt",
    )(qkv, qkv, qkv, qkv, qkv, qkv, z, a, b, cw, cw, cw, alog, dtb, nw)


def _gdn_sample_kernel(q_ref, k_ref, v_ref, cq_ref, ck_ref, cv_ref, wq_ref, wk_ref, wv_ref,
                       z_ref, a_ref, b_ref, alog_ref, dtb_ref, nw_ref, s_ref, o_ref, so_ref):
    def conv(x_ref, c_ref, w_ref):
        y = w_ref[GDN_CONV - 1] * x_ref[...]
        for i in range(GDN_CONV - 1):
            y = y + w_ref[i] * c_ref[i]
        return _silu(y)

    q = conv(q_ref, cq_ref, wq_ref)
    q = q * lax.rsqrt(jnp.sum(q * q, axis=0, keepdims=True) + EPS) * (GDN_DK ** -0.5)
    k = conv(k_ref, ck_ref, wk_ref)
    k = k * lax.rsqrt(jnp.sum(k * k, axis=0, keepdims=True) + EPS)
    v = conv(v_ref, cv_ref, wv_ref)
    beta = _sigmoid(b_ref[0])
    g = -jnp.exp(alog_ref[0]) * _softplus(a_ref[0] + dtb_ref[0])
    dec = jnp.exp(g)
    ks = jnp.zeros_like(v)
    for i in range(GDN_DK):
        ks = ks + k[i:i + 1, :] * (s_ref[0, i] * dec)
    vn = beta * (v - ks)
    o = jnp.zeros_like(v)
    for i in range(GDN_DK):
        sn = s_ref[0, i] * dec + k[i:i + 1, :] * vn
        so_ref[0, i] = sn
        o = o + q[i:i + 1, :] * sn
    on = o * lax.rsqrt(jnp.mean(o * o, axis=0, keepdims=True) + EPS) * nw_ref[...]
    o_ref[...] = on * _silu(z_ref[...])


def _gdn_sample(qkv_t, conv_t, cw_t, z_t, a_t, b_t, alog_t, dtb_t, nw_t, s_t):
    nb = qkv_t.shape[1]
    H, DK, DV = GDN_HEADS, GDN_DK, GDN_DV
    row = lambda off: pl.BlockSpec((DK, nb), lambda h: (off + h, 0))
    crow = lambda off: pl.BlockSpec((GDN_CONV - 1, DK, nb), lambda h: (0, off + h, 0))
    wrow = lambda off: pl.BlockSpec((GDN_CONV, DK, nb), lambda h: (0, off + h, 0))
    per_head = pl.BlockSpec((1, 1, nb), lambda h: (h, 0, 0))
    return pl.pallas_call(
        _gdn_sample_kernel,
        grid=(H,),
        in_specs=[row(0), row(H), row(2 * H), crow(0), crow(H), crow(2 * H), wrow(0), wrow(H), wrow(2 * H),
                  row(0), per_head, per_head, per_head, per_head,
                  pl.BlockSpec((DV, nb), lambda h: (0, 0)),
                  pl.BlockSpec((1, DK, DV, nb), lambda h: (h, 0, 0, 0))],
        out_specs=[row(0), pl.BlockSpec((1, DK, DV, nb), lambda h: (h, 0, 0, 0))],
        out_shape=[jax.ShapeDtypeStruct((H * DV, nb), F32), jax.ShapeDtypeStruct((H, DK, DV, nb), F32)],
        compiler_params=_cparams("parallel"),
        name="gdn_sample",
    )(qkv_t, qkv_t, qkv_t, conv_t, conv_t, conv_t, cw_t, cw_t, cw_t, z_t, a_t, b_t, alog_t, dtb_t, nw_t, s_t)


def _count_ge(x, thr):
    return jnp.sum(jnp.where(x >= thr, 1.0, 0.0), axis=1, keepdims=True)


def _topk_mask(score, valid, k):
    rows, width = score.shape
    kf = float(k)
    x = jnp.where(valid, score, -jnp.inf)
    validf = jnp.where(valid, 1.0, 0.0)
    nvalid = jnp.sum(validf, axis=1, keepdims=True)
    few = nvalid <= kf
    mx = jnp.max(x, axis=1, keepdims=True)
    mn = jnp.min(jnp.where(valid, score, jnp.inf), axis=1, keepdims=True)
    lo0 = mn
    hi0 = mx + (jnp.abs(mx) * 1e-6 + 1e-30)
    chi0 = jnp.zeros_like(mx)
    clo0 = nvalid

    def step(_, carry):
        lo, hi, chi, clo = carry
        mid = 0.5 * lo + 0.5 * hi
        c = _count_ge(x, mid)
        ge = c >= kf
        return jnp.where(ge, mid, lo), jnp.where(ge, hi, mid), jnp.where(ge, chi, c), jnp.where(ge, c, clo)

    def finish(hi, chi):
        tau = jnp.max(jnp.where(x < hi, x, -jnp.inf), axis=1, keepdims=True)
        ceq = jnp.sum(jnp.where(x == tau, 1.0, 0.0), axis=1, keepdims=True)
        return tau, ceq

    def not_done(hi, chi, clo):
        tau, ceq = finish(hi, chi)
        bad = jnp.where(few | (clo == kf) | (chi + ceq >= kf), 0.0, 1.0)
        return jnp.max(bad) > 0.0

    def exact_cond(carry):
        it, _, _, _, clo = carry
        open_rows = jnp.max(jnp.where(few | (clo == kf), 0.0, 1.0)) > 0.0
        return jnp.logical_and(it < 6, open_rows)

    def rounds(nsteps):
        def body(carry):
            it, lo, hi, chi, clo = carry
            lo, hi, chi, clo = lax.fori_loop(0, nsteps, step, (lo, hi, chi, clo))
            return it + 1, lo, hi, chi, clo
        return body

    _, lo, hi, chi, clo = lax.while_loop(exact_cond, rounds(4), (jnp.int32(0), lo0, hi0, chi0, clo0))

    def w_cond(carry):
        it, _, hi, chi, clo = carry
        open_rows = jnp.max(jnp.where(few | (clo == kf), 0.0, 1.0)) > 0.0
        return jnp.logical_and(open_rows, jnp.logical_and(it < 48, not_done(hi, chi, clo)))

    _, lo, hi, chi, clo = lax.while_loop(w_cond, rounds(8), (jnp.int32(0), lo, hi, chi, clo))
    exact = clo == kf
    tau, ceq = finish(hi, chi)
    tau = jnp.where(exact, lo, tau)
    need = jnp.where(exact, 0.0, kf - chi)
    gtf = jnp.where(x >= jnp.where(exact, lo, hi), 1.0, 0.0)
    eqf = jnp.where(jnp.logical_and(x == tau, jnp.logical_not(exact)), 1.0, 0.0)
    has_tie = jnp.max(jnp.where(few | exact | (ceq <= need), 0.0, 1.0)) > 0.0

    def tie_path(_):
        nchunk = width // LANES
        su = jnp.where(_iota((LANES, LANES), 0) < _iota((LANES, LANES), 1), 1.0, 0.0).astype(BF16)
        run = jnp.zeros_like(need)
        pieces = []
        for c in range(nchunk):
            e = eqf[:, c * LANES:(c + 1) * LANES]
            before = jnp.dot(e.astype(BF16), su, preferred_element_type=F32) + run
            pieces.append(jnp.where(before < need, e, 0.0))
            run = run + jnp.sum(e, axis=1, keepdims=True)
        return jnp.concatenate(pieces, axis=1)

    sel_eq = lax.cond(has_tie, tie_path, lambda _: eqf, 0)
    return jnp.where(few, validf, gtf + sel_eq)


def _dsa_prompt_kernel(qi_ref, smq_ref, sma_ref, qh_ref, kt_ref, v_ref, o_ref, *, n_sel, j0, qb):
    j = j0 + pl.program_id(1)
    S = kt_ref.shape[-1]
    qi = qi_ref[0]
    wi = smq_ref[0][:, _SM_IW:_SM_IW + IDX_HEADS] * (IDX_HEADS ** -0.5 * IDX_DIM ** -0.5)
    ki = sma_ref[0][:, _SM_IK:_SM_IK + IDX_DIM]
    score = jnp.zeros((qb, S), F32)
    for h in range(IDX_HEADS):
        s = _dot_nt(qi[:, h * IDX_DIM:(h + 1) * IDX_DIM], ki)
        score = score + jnp.maximum(s, 0.0) * wi[:, h:h + 1]
    tpos = j * qb + _iota((qb, 1), 0)
    valid = _iota((1, S), 1) <= tpos
    mask = _topk_mask(score, valid, n_sel)
    bias = jnp.where(mask > 0.0, 0.0, -jnp.inf)
    bias2 = jnp.concatenate([bias, bias], axis=0)
    G = DSA_HEADS // DSA_KV_HEADS
    outs = []
    for kv in range(DSA_KV_HEADS):
        q2 = qh_ref[0, G * kv:G * (kv + 1)].reshape(G * qb, DSA_DH) * (DSA_DH ** -0.5)
        s = _dot(q2, kt_ref[0, kv]) + bias2
        p = jnp.exp(s - jnp.max(s, axis=1, keepdims=True))
        o = _dot(p, v_ref[0, kv]) / jnp.sum(p, axis=1, keepdims=True)
        outs += [o[g * qb:(g + 1) * qb] for g in range(G)]
    o_ref[0] = jnp.concatenate(outs, axis=1)


def _dsa_prompt(qi, small, qh, kt, vh, n_sel, qb):
    B, T, _ = qi.shape
    nq = T // qb
    nseg = 8 if nq % 8 == 0 else (4 if nq % 4 == 0 else 1)
    qps = nq // nseg
    outs = []
    for seg in range(nseg):
        j0 = seg * qps
        S = (seg + 1) * qps * qb
        outs.append(pl.pallas_call(
            functools.partial(_dsa_prompt_kernel, n_sel=n_sel, j0=j0, qb=qb),
            grid=(B, qps),
            in_specs=[pl.BlockSpec((1, qb, IDX_HEADS * IDX_DIM), lambda b, j, j0=j0: (b, j0 + j, 0)),
                      pl.BlockSpec((1, qb, LANES), lambda b, j, j0=j0: (b, j0 + j, 0)),
                      pl.BlockSpec((1, S, LANES), lambda b, j: (b, 0, 0)),
                      pl.BlockSpec((1, DSA_HEADS, qb, DSA_DH), lambda b, j, j0=j0: (b, 0, j0 + j, 0)),
                      pl.BlockSpec((1, DSA_KV_HEADS, DSA_DH, S), lambda b, j: (b, 0, 0, 0)),
                      pl.BlockSpec((1, DSA_KV_HEADS, S, DSA_DH), lambda b, j: (b, 0, 0, 0))],
            out_specs=pl.BlockSpec((1, qb, DSA_HEADS * DSA_DH), lambda b, j: (b, j, 0)),
            out_shape=jax.ShapeDtypeStruct((B, qps * qb, DSA_HEADS * DSA_DH), F32),
            compiler_params=_cparams("parallel", "arbitrary"),
            name="dsa_prompt",
        )(qi, small, small, qh, kt, vh))
    return outs[0] if nseg == 1 else jnp.concatenate(outs, axis=1)


def _dsa_sample_score_kernel(pt_ref, qi_ref, wi_ref, kit_hbm, o_ref, buf, sem, *, n_pages):
    b = pl.program_id(0)

    def fetch(sample, slot):
        def issue(p, carry):
            pltpu.make_async_copy(kit_hbm.at[pt_ref[sample, p]], buf.at[slot, p], sem.at[slot]).start()
            return carry

        lax.fori_loop(0, n_pages, issue, 0, unroll=8)

    @pl.when(b == 0)
    def _():
        fetch(0, 0)

    @pl.when(b + 1 < pl.num_programs(0))
    def _():
        fetch(b + 1, (b + 1) % 2)

    slot = b % 2
    pltpu.make_async_copy(kit_hbm.at[pl.ds(0, n_pages)], buf.at[slot], sem.at[slot]).wait()
    qi = qi_ref[0]
    wi = wi_ref[0] * (IDX_HEADS ** -0.5 * IDX_DIM ** -0.5)
    for p in range(n_pages):
        s = _dot(qi, buf[slot, p])
        o_ref[0, :, p * PAGE_SIZE:(p + 1) * PAGE_SIZE] = jnp.sum(jnp.maximum(s, 0.0) * wi, axis=0, keepdims=True)


def _dsa_sample_scores(page_table, qi8, wi8, cache_kit):
    Bd, n_pages = page_table.shape
    grid_spec = pltpu.PrefetchScalarGridSpec(
        num_scalar_prefetch=1,
        grid=(Bd,),
        in_specs=[pl.BlockSpec((1, 8, IDX_DIM), lambda b, pt: (b, 0, 0)),
                  pl.BlockSpec((1, 8, 1), lambda b, pt: (b, 0, 0)),
                  pl.BlockSpec(memory_space=pl.ANY)],
        out_specs=pl.BlockSpec((1, 1, n_pages * PAGE_SIZE), lambda b, pt: (b, 0, 0)),
        scratch_shapes=[pltpu.VMEM((2, n_pages, IDX_DIM, PAGE_SIZE), F32), pltpu.SemaphoreType.DMA((2,))],
    )
    return pl.pallas_call(
        functools.partial(_dsa_sample_score_kernel, n_pages=n_pages),
        grid_spec=grid_spec,
        out_shape=jax.ShapeDtypeStruct((Bd, 1, n_pages * PAGE_SIZE), F32),
        compiler_params=_cparams("arbitrary"),
        name="dsa_sample_scores",
    )(page_table, qi8, wi8, cache_kit)


def _dsa_sample_select_kernel(sc_ref, qi_ref, sm_ref, sel_ref, seln_ref, *, n_sel):
    Bd, past = sc_ref.shape
    width = past + LANES
    sm = sm_ref[...]
    qi = _bf(qi_ref[...]).astype(F32)
    ki = _bf(sm[:, _SM_IK:_SM_IK + IDX_DIM]).astype(F32)
    wi = sm[:, _SM_IW:_SM_IW + IDX_HEADS] * (IDX_HEADS ** -0.5 * IDX_DIM ** -0.5)
    snew = jnp.zeros((Bd, 1), F32)
    for h in range(IDX_HEADS):
        s = jnp.sum(qi[:, h * IDX_DIM:(h + 1) * IDX_DIM] * ki, axis=1, keepdims=True)
        snew = snew + jnp.maximum(s, 0.0) * wi[:, h:h + 1]
    tail = jnp.where(_iota((Bd, LANES), 1) == 0, snew, -jnp.inf)
    x = jnp.concatenate([sc_ref[...], tail], axis=1)
    valid = jnp.broadcast_to(_iota((1, width), 1) <= past, (Bd, width))
    mask = _topk_mask(x, valid, n_sel)
    sel_ref[...] = mask[:, :past]
    seln_ref[...] = mask[:, past:]


def _dsa_sample_select(scores, qi, small, n_sel):
    Bd, past = scores.shape
    return pl.pallas_call(
        functools.partial(_dsa_sample_select_kernel, n_sel=n_sel),
        out_shape=[jax.ShapeDtypeStruct((Bd, past), F32), jax.ShapeDtypeStruct((Bd, LANES), F32)],
        compiler_params=pltpu.CompilerParams(vmem_limit_bytes=VMEM_LIMIT_BYTES),
        name="dsa_sample_select",
    )(scores, qi, small)


_MASKED = -1e30


def _dsa_sample_attn_kernel(pt_ref, q_ref, sel_ref, seln_ref, kn_ref, vn_ref, kt_hbm, vt_hbm, o_ref,
                            kbuf, vbuf, sem, m_scr, l_scr, acc_scr, *, n_pg):
    jj = pl.program_id(1)
    njj = pl.num_programs(1)
    step = pl.program_id(0) * njj + jj
    G = DSA_HEADS // DSA_KV_HEADS
    row_kv = _iota((DSA_HEADS, DSA_DH), 0) // G

    def fetch(st, slot):
        sample = st // njj
        first = (st % njj) * n_pg

        def issue(p, carry):
            page = pt_ref[sample, first + p]
            pltpu.make_async_copy(kt_hbm.at[page], kbuf.at[slot, p], sem.at[0, slot]).start()
            pltpu.make_async_copy(vt_hbm.at[page], vbuf.at[slot, p], sem.at[1, slot]).start()
            return carry

        lax.fori_loop(0, n_pg, issue, 0, unroll=8)

    @pl.when(step == 0)
    def _():
        fetch(0, 0)

    @pl.when(step + 1 < pl.num_programs(0) * njj)
    def _():
        fetch(step + 1, (step + 1) % 2)

    slot = step % 2
    pltpu.make_async_copy(kt_hbm.at[pl.ds(0, n_pg)], kbuf.at[slot], sem.at[0, slot]).wait()
    pltpu.make_async_copy(vt_hbm.at[pl.ds(0, n_pg)], vbuf.at[slot], sem.at[1, slot]).wait()

    @pl.when(jj == 0)
    def _():
        m_scr[...] = jnp.full_like(m_scr, _MASKED)
        l_scr[...] = jnp.zeros_like(l_scr)
        acc_scr[...] = jnp.zeros_like(acc_scr)

    def block(kts, vts, msk):
        s = jnp.dot(_bf(q_ref[0, 0]), kts[0], preferred_element_type=F32)
        for kv in range(1, DSA_KV_HEADS):
            s = s + jnp.dot(_bf(q_ref[0, kv]), kts[kv], preferred_element_type=F32)
        s = s * (DSA_DH ** -0.5)
        on = msk > 0.0
        m_old = m_scr[...]
        m_new = jnp.maximum(m_old, jnp.max(jnp.where(on, s, _MASKED), axis=1, keepdims=True))
        alpha = jnp.exp(m_old - m_new)
        p = jnp.where(on, jnp.exp(s - m_new), 0.0)
        l_scr[...] = l_scr[...] * alpha + jnp.sum(p, axis=1, keepdims=True)
        pb = _bf(p)
        acc = acc_scr[...] * alpha
        for kv in range(DSA_KV_HEADS):
            o_kv = lax.dot_general(pb, vts[kv], _NT, preferred_element_type=F32)
            acc = acc + jnp.where(row_kv == kv, o_kv, 0.0)
        acc_scr[...] = acc
        m_scr[...] = m_new

    gather = lambda pages, kv: jnp.concatenate([_bf(pages[slot, p, kv]) for p in range(n_pg)], axis=1)
    block([gather(kbuf, kv) for kv in range(DSA_KV_HEADS)], [gather(vbuf, kv) for kv in range(DSA_KV_HEADS)],
          sel_ref[0])

    @pl.when(jj == njj - 1)
    def _():
        block([_bf(kn_ref[0, kv]) for kv in range(DSA_KV_HEADS)], [_bf(vn_ref[0, kv]) for kv in range(DSA_KV_HEADS)],
              seln_ref[0])
        o_ref[0] = acc_scr[...] / l_scr[...]


def _dsa_sample_attn(page_table, q8, sel, sel_new, k_new_b, v_new_b, cache_kt, cache_vt, n_pg):
    Bd, n_pages = page_table.shape
    new_tok = pl.BlockSpec((1, DSA_KV_HEADS, DSA_DH, PAGE_SIZE), lambda b, jj, pt: (b, 0, 0, 0))
    pages = pltpu.VMEM((2, n_pg, DSA_KV_HEADS, DSA_DH, PAGE_SIZE), F32)
    grid_spec = pltpu.PrefetchScalarGridSpec(
        num_scalar_prefetch=1,
        grid=(Bd, n_pages // n_pg),
        in_specs=[pl.BlockSpec((1, DSA_KV_HEADS, DSA_HEADS, DSA_DH), lambda b, jj, pt: (b, 0, 0, 0)),
                  pl.BlockSpec((1, 1, n_pg * PAGE_SIZE), lambda b, jj, pt: (b, 0, jj)),
                  pl.BlockSpec((1, 1, PAGE_SIZE), lambda b, jj, pt: (b, 0, 0)), new_tok, new_tok,
                  pl.BlockSpec(memory_space=pl.ANY), pl.BlockSpec(memory_space=pl.ANY)],
        out_specs=pl.BlockSpec((1, DSA_HEADS, DSA_DH), lambda b, jj, pt: (b, 0, 0)),
        scratch_shapes=[pages, pages, pltpu.SemaphoreType.DMA((2, 2)),
                        pltpu.VMEM((DSA_HEADS, 1), F32), pltpu.VMEM((DSA_HEADS, 1), F32),
                        pltpu.VMEM((DSA_HEADS, DSA_DH), F32)],
    )
    return pl.pallas_call(
        functools.partial(_dsa_sample_attn_kernel, n_pg=n_pg),
        grid_spec=grid_spec,
        out_shape=jax.ShapeDtypeStruct((Bd, DSA_HEADS, DSA_DH), F32),
        compiler_params=_cparams("arbitrary", "arbitrary"),
        name="dsa_sample_attn",
    )(page_table, q8, sel.reshape(Bd, 1, -1), sel_new.reshape(Bd, 1, PAGE_SIZE), k_new_b, v_new_b, cache_kt, cache_vt)


def _attend_rows(q, kk, vv, scale):
    s = jnp.sum(kk * q[None], axis=-1, keepdims=True) * scale
    m = jnp.max(s, axis=0, keepdims=True)
    p = jnp.exp(s - m)
    l = jnp.sum(p, axis=0)
    return jnp.sum(p * vv, axis=0) / l


def _mem_prompt_kernel(q_ref, mk_ref, mv_ref, o_ref):
    q = q_ref[0]
    mk = mk_ref[0]
    mv = mv_ref[0]
    outs = []
    for h in range(MEM_HEADS):
        sl = slice(h * MEM_DH, (h + 1) * MEM_DH)
        s = _dot_nt(q[:, sl], mk[:, sl]) * (MEM_DH ** -0.5)
        m = jnp.max(s, axis=1, keepdims=True)
        p = jnp.exp(s - m)
        p = p / jnp.sum(p, axis=1, keepdims=True)
        outs.append(_dot(p, mv[:, sl]))
    o_ref[0] = jnp.concatenate(outs, axis=1)


def _mem_prompt(q, mk, mv, tq):
    B, T, W = q.shape
    M = mk.shape[1]
    kv = pl.BlockSpec((1, M, W), lambda b, i: (b, 0, 0))
    return pl.pallas_call(
        _mem_prompt_kernel,
        grid=(B, T // tq),
        in_specs=[pl.BlockSpec((1, tq, W), lambda b, i: (b, i, 0)), kv, kv],
        out_specs=pl.BlockSpec((1, tq, W), lambda b, i: (b, i, 0)),
        out_shape=jax.ShapeDtypeStruct((B, T, W), F32),
        compiler_params=_cparams("parallel", "parallel"),
        name="mem_prompt",
    )(q, mk, mv)


def _mem_sample_kernel(q_ref, k_ref, v_ref, o_ref, *, ns):
    for i in range(ns):
        o_ref[i] = _attend_rows(q_ref[i], k_ref[i], v_ref[i], MEM_DH ** -0.5)


def _mem_sample(q, ck, cv, ns):
    Bd, M = ck.shape[:2]
    kv = pl.BlockSpec((ns, M, MEM_HEADS, MEM_DH), lambda b: (b, 0, 0, 0))
    qs = pl.BlockSpec((ns, MEM_HEADS, MEM_DH), lambda b: (b, 0, 0))
    return pl.pallas_call(
        functools.partial(_mem_sample_kernel, ns=ns),
        grid=(Bd // ns,),
        in_specs=[qs, kv, kv],
        out_specs=qs,
        out_shape=jax.ShapeDtypeStruct((Bd, MEM_HEADS, MEM_DH), F32),
        compiler_params=_cparams("parallel"),
        name="mem_sample",
    )(q, ck, cv)


def _merge_kernel(x_ref, og_ref, od_ref, om_ref, gn_ref, wg_ref, wb_ref, wo_ref, fn_ref, xo_ref, h2_ref):
    x = x_ref[...]
    d = x.shape[1]
    hb = _rms(x, gn_ref[...]).astype(BF16)
    acc = jnp.zeros_like(x)
    for n, o_ref in enumerate((og_ref, od_ref, om_ref)):
        gate = _sigmoid(jnp.dot(hb, wg_ref[:, n * d:(n + 1) * d], preferred_element_type=F32))
        acc = acc + gate * jnp.dot(_bf(o_ref[...]), wb_ref[n], preferred_element_type=F32)
    xo = x + jnp.dot(_bf(acc), wo_ref[...], preferred_element_type=F32)
    _tt_store(xo_ref, xo)
    _tt_store(h2_ref, _rms(xo, fn_ref[...]))


def _merge(x, o_g, o_d, o_m, gn, wg, wb, wo, fn, tm):
    n, d = x.shape
    tok = lambda w: pl.BlockSpec((tm, w), lambda i: (i, 0))
    full = lambda shape: pl.BlockSpec(shape, lambda i: (0,) * len(shape))
    return pl.pallas_call(
        _merge_kernel,
        grid=(n // tm,),
        in_specs=[tok(d), tok(BRANCH_W), tok(BRANCH_W), tok(BRANCH_W), full((1, d)), full((d, N_BRANCH * d)),
                  full((N_BRANCH, BRANCH_W, d)), full((d, d)), full((1, d))],
        out_specs=[pl.BlockSpec((tm * TT, LANES), lambda i: (i, 0))] * 2,
        out_shape=[jax.ShapeDtypeStruct((n * TT, LANES), F32)] * 2,
        compiler_params=_cparams("parallel"),
        name="merge",
    )(x, o_g, o_d, o_m, gn.reshape(1, d), wg, wb, wo, fn.reshape(1, d))


def _route_kernel(h_ref, wr_ref, br_ref, e_ref, p_ref, r_ref, cnt_ref, run_scr, *, tr):
    i = pl.program_id(0)

    @pl.when(i == 0)
    def _():
        run_scr[...] = jnp.zeros_like(run_scr)

    logits = _dot_nt(wr_ref[...], _tt_load(h_ref, tr)) + br_ref[...]
    eidx = _iota((N_EXPERTS, LANES), 0)
    su = jnp.where(_iota((LANES, LANES), 0) < _iota((LANES, LANES), 1), 1.0, 0.0).astype(BF16)
    run = run_scr[...]
    for c in range(tr // LANES):
        sl = slice(c * LANES, (c + 1) * LANES)
        l = logits[:, sl]
        vals, idxs = [], []
        for _ in range(TOP_K):
            m = jnp.max(l, axis=0, keepdims=True)
            idx = jnp.min(jnp.where(l == m, eidx, N_EXPERTS), axis=0, keepdims=True)
            vals.append(m)
            idxs.append(idx)
            l = jnp.where(eidx == idx, -jnp.inf, l)
        ex = [jnp.exp(v - vals[0]) for v in vals]
        den = ex[0] + ex[1] + ex[2] + ex[3]
        oh = jnp.zeros((N_EXPERTS, LANES), F32)
        for k in range(TOP_K):
            p_ref[k:k + 1, sl] = ex[k] / den
            e_ref[k:k + 1, sl] = idxs[k]
            oh = oh + jnp.where(eidx == idxs[k], 1.0, 0.0)
        before = jnp.dot(oh.astype(BF16), su, preferred_element_type=F32) + run
        for k in range(TOP_K):
            rk = jnp.sum(jnp.where(eidx == idxs[k], before, 0.0), axis=0, keepdims=True)
            r_ref[k:k + 1, sl] = rk.astype(I32)
        run = run + jnp.sum(oh, axis=1, keepdims=True)
    run_scr[...] = run
    cnt_ref[...] = jnp.broadcast_to(run, cnt_ref.shape)


def _route(h2, wr_t, br, tr):
    n = h2.shape[0] // TT
    d = TT * LANES
    tokrow = pl.BlockSpec((TOP_K, tr), lambda i: (0, i))
    return pl.pallas_call(
        functools.partial(_route_kernel, tr=tr),
        grid=(n // tr,),
        in_specs=[pl.BlockSpec((tr * TT, LANES), lambda i: (i, 0)),
                  pl.BlockSpec((N_EXPERTS, d), lambda i: (0, 0)),
                  pl.BlockSpec((N_EXPERTS, 1), lambda i: (0, 0))],
        out_specs=[tokrow, tokrow, tokrow, pl.BlockSpec((N_EXPERTS, LANES), lambda i: (0, 0))],
        out_shape=[jax.ShapeDtypeStruct((TOP_K, n), I32), jax.ShapeDtypeStruct((TOP_K, n), F32),
                   jax.ShapeDtypeStruct((TOP_K, n), I32), jax.ShapeDtypeStruct((N_EXPERTS, LANES), F32)],
        scratch_shapes=[pltpu.VMEM((N_EXPERTS, 1), F32)],
        compiler_params=_cparams("arbitrary"),
        name="moe_route",
    )(h2, wr_t, br)


def _dispatch_kernel(plo_ref, pn_ref, nb_ref, dest_ref, h_ref, xg_hbm, zblk, sem, *, td):
    blk = zblk.shape[0] // TT
    n_blocks = xg_hbm.shape[0] // (blk * TT)
    tile = lambda i: pl.ds(pl.multiple_of(i * TT, TT), TT)

    def issue(t, carry):
        for k in range(TOP_K):
            pltpu.make_async_copy(h_ref.at[tile(t)], xg_hbm.at[tile(dest_ref[k, t])], sem.at[0]).start(priority=k % 2)
        return carry

    lax.fori_loop(0, td, issue, 0, unroll=8)

    @pl.when(pl.program_id(0) == 0)
    def _():
        zblk[...] = jnp.zeros_like(zblk)

        def zero_block(i):
            rows = pl.ds(pl.multiple_of(i * (blk * TT), blk * TT), blk * TT)
            return pltpu.make_async_copy(zblk, xg_hbm.at[rows], sem.at[2])

        def tail_start(i, c):
            zero_block(i).start()
            return c

        def tail_wait(i, c):
            zero_block(0).wait()
            return c

        lax.fori_loop(nb_ref[0], n_blocks, tail_start, 0)
        lax.fori_loop(nb_ref[0], n_blocks, tail_wait, 0)

        pieces = [1 << i for i in reversed(range(blk.bit_length() - 1))]

        def per_expert(e, carry):
            n = pn_ref[e]
            for wait in (False, True):
                off = plo_ref[e]
                for size in pieces:
                    rows = pl.ds(pl.multiple_of(off * TT, TT), size * TT)
                    copy = pltpu.make_async_copy(zblk.at[pl.ds(0, size * TT)], xg_hbm.at[rows], sem.at[1])

                    @pl.when((n & size) != 0)
                    def _():
                        copy.wait() if wait else copy.start()

                    off = off + (n & size)
            return carry

        lax.fori_loop(0, N_EXPERTS, per_expert, 0)

    for k in range(TOP_K):
        pltpu.make_async_copy(h_ref, xg_hbm.at[pl.ds(0, td * TT)], sem.at[0]).wait()


def _dispatch(h2, dest, pad_lo, pad_n, nb_used, n_slots, blk, td):
    n = h2.shape[0] // TT
    grid_spec = pltpu.PrefetchScalarGridSpec(
        num_scalar_prefetch=3,
        grid=(n // td,),
        in_specs=[pl.BlockSpec((TOP_K, td), lambda i, plo, pn, nb: (0, i), memory_space=pltpu.SMEM),
                  pl.BlockSpec((td * TT, LANES), lambda i, plo, pn, nb: (i, 0))],
        out_specs=pl.BlockSpec(memory_space=pl.ANY),
        scratch_shapes=[pltpu.VMEM((blk * TT, LANES), F32), pltpu.SemaphoreType.DMA((3,))],
    )
    return pl.pallas_call(
        functools.partial(_dispatch_kernel, td=td),
        grid_spec=grid_spec,
        out_shape=jax.ShapeDtypeStruct((n_slots * TT, LANES), F32),
        compiler_params=_cparams("arbitrary"),
        name="moe_dispatch",
    )(pad_lo, pad_n, nb_used, dest, h2)


def _expert_kernel(be_ref, nb_ref, x_ref, wgu_ref, bgu_ref, wdn_ref, bdn_ref, o_ref):
    del be_ref
    f = wdn_ref.shape[1]
    blk = x_ref.shape[0] // TT

    used = pl.program_id(0) < nb_ref[0]

    @pl.when(jnp.logical_not(used))
    def _():
        o_ref[...] = jnp.zeros_like(o_ref)

    @pl.when(used)
    def _():
        gu = jnp.dot(_bf(_tt_load(x_ref, blk)), _bf(wgu_ref[0]), preferred_element_type=F32) + bgu_ref[0]
        gate = jnp.minimum(gu[:, :f], SWIGLU_LIMIT)
        up = jnp.clip(gu[:, f:], -SWIGLU_LIMIT, SWIGLU_LIMIT)
        glu = gate * _sigmoid(SWIGLU_ALPHA * gate)
        _tt_store(o_ref, jnp.dot(_bf((up + 1.0) * glu), _bf(wdn_ref[0]), preferred_element_type=F32) + bdn_ref[0])


def _experts(xg, block_e, nb_used, wgu, bgu, wdn, bdn, blk):
    n_slots = xg.shape[0] // TT
    d = TT * LANES
    f = wdn.shape[1]
    blk_of = lambda i, nb: jnp.minimum(i, nb[0] - 1)
    tok = pl.BlockSpec((blk * TT, LANES), lambda i, be, nb: (blk_of(i, nb), 0))
    per_e = lambda shape: pl.BlockSpec((1,) + shape, lambda i, be, nb: (be[blk_of(i, nb)], 0, 0))
    grid_spec = pltpu.PrefetchScalarGridSpec(
        num_scalar_prefetch=2,
        grid=(n_slots // blk,),
        in_specs=[tok, per_e((d, 2 * f)), per_e((1, 2 * f)), per_e((f, d)), per_e((1, d))],
        out_specs=pl.BlockSpec((blk * TT, LANES), lambda i, be, nb: (i, 0)),
    )
    return pl.pallas_call(
        _expert_kernel,
        grid_spec=grid_spec,
        out_shape=jax.ShapeDtypeStruct((n_slots * TT, LANES), F32),
        compiler_params=_cparams("arbitrary"),
        name="moe_experts",
    )(block_e, nb_used, xg, wgu, bgu, wdn, bdn)


def _combine_kernel(dest_ref, x_ref, p_ref, g_ref, yb_hbm, o_ref, buf, sem, *, tc):
    tile = lambda i: pl.ds(pl.multiple_of(i * TT, TT), TT)

    def issue(t, carry):
        for k in range(TOP_K):
            pltpu.make_async_copy(yb_hbm.at[tile(dest_ref[k, t])], buf.at[k, tile(t)], sem.at[0]).start(priority=k % 2)
        return carry

    lax.fori_loop(0, tc, issue, 0, unroll=8)
    for k in range(TOP_K):
        pltpu.make_async_copy(yb_hbm.at[pl.ds(0, tc * TT)], buf.at[k], sem.at[0]).wait()
    p = p_ref[...]
    acc = p[:, 0:1] * _tt_load(buf.at[0], tc)
    for k in range(1, TOP_K):
        acc = acc + p[:, k:k + 1] * _tt_load(buf.at[k], tc)
    o_ref[...] = _rms(_tt_load(x_ref, tc) + acc, g_ref[...])


def _combine(x, yb, dest, p_t, g, tc):
    n = x.shape[0] // TT
    d = TT * LANES
    return pl.pallas_call(
        functools.partial(_combine_kernel, tc=tc),
        grid=(n // tc,),
        in_specs=[pl.BlockSpec((TOP_K, tc), lambda i: (0, i), memory_space=pltpu.SMEM),
                  pl.BlockSpec((tc * TT, LANES), lambda i: (i, 0)),
                  pl.BlockSpec((tc, TOP_K), lambda i: (i, 0)),
                  pl.BlockSpec((1, d), lambda i: (0, 0)),
                  pl.BlockSpec(memory_space=pl.ANY)],
        out_specs=pl.BlockSpec((tc, d), lambda i: (i, 0)),
        out_shape=jax.ShapeDtypeStruct((n, d), F32),
        scratch_shapes=[pltpu.VMEM((TOP_K, tc * TT, LANES), F32), pltpu.SemaphoreType.DMA((1,))],
        compiler_params=_cparams("arbitrary"),
        name="moe_combine",
    )(dest, x, p_t, g.reshape(1, d), yb)


def _moe_final(x, h2, g_final, wr_t, br, wgu, bgu, wdn, bdn, blk, tile):
    n = x.shape[0] // TT
    e, p, rank, cnt = _route(h2, wr_t, br, tile)
    counts = cnt[:, 0].astype(I32)
    padded = (counts + blk - 1) // blk * blk
    pad_end = jnp.cumsum(padded)
    pad_start = pad_end - padded
    n_blocks = -(-n * TOP_K // blk) + N_EXPERTS
    expert_ids = jnp.arange(N_EXPERTS, dtype=I32)[:, None, None]
    dest = rank + jnp.sum(jnp.where(e[None] == expert_ids, pad_start[:, None, None], 0), axis=0)
    block_lo = jnp.arange(n_blocks, dtype=I32)[:, None] * blk
    block_e = jnp.minimum(jnp.sum((pad_end[None, :] <= block_lo).astype(I32), axis=1), N_EXPERTS - 1)
    nb_used = (pad_end[-1:] // blk).astype(I32)
    xg = _dispatch(h2, dest, pad_start + counts, padded - counts, nb_used, n_blocks * blk, blk, tile)
    yb = _experts(xg, block_e, nb_used, wgu, bgu, wdn, bdn, blk)
    return _combine(x, yb, dest, p.T, g_final, min(tile, 256))


def _prep_w_in(w):
    cuts = np.cumsum((0,) + _SPLITS)
    seg = lambda i: w[:, int(cuts[i]):int(cuts[i + 1])]
    small = jnp.concatenate([seg(8), seg(9), seg(2), seg(3), jnp.zeros((w.shape[0], LANES - _SM_END), w.dtype)], axis=1)
    w1 = jnp.concatenate([seg(0), seg(1), seg(4), seg(5), seg(6), seg(7), seg(10), small], axis=1)
    return w1.astype(BF16), w[:, int(cuts[-1]):].astype(BF16)


def kernel(x_prompt, x_sample, cache_k, cache_v, cache_k_idx, cache_mem_k, cache_mem_v, state_gdn, state_conv,
           page_table, mem_prompt, norm_attn, w_in, conv_w, gdn_a_log, gdn_dt_bias, gdn_norm, norm_mem, w_mem_kv,
           w_branch, w_out, norm_ffn, w_router, b_router, w_gate_up, b_gate_up, w_down, b_down, norm_final):
    B, T, D = x_prompt.shape
    Bd, Ts, _ = x_sample.shape
    assert Ts == 1 and w_in.shape[0] == 1, "one layer, one new token per sample"
    H = GDN_HEADS
    G = DSA_HEADS // DSA_KV_HEADS
    n_pages = page_table.shape[1]
    M = mem_prompt.shape[1]

    w1, wg = _prep_w_in(w_in[0])
    wb = w_branch[0].astype(BF16)
    wo = w_out[0].astype(BF16)
    wr_t = w_router[0].T
    br = b_router[0].reshape(N_EXPERTS, 1)
    wgu = w_gate_up[0]
    wdn = w_down[0]
    bgu = b_gate_up[0].reshape(N_EXPERTS, 1, -1)
    bdn = b_down[0].reshape(N_EXPERTS, 1, -1)
    alog = gdn_a_log[0]
    dtb = gdn_dt_bias[0]

    xp = x_prompt.reshape(B * T, D)
    g_qkv, g_z, d_k, d_v, i_q, m_q, small, qh, kt, vh = _proj_prompt(xp, norm_attn[0], w1, B, T, min(512, T))

    quad_rows = lambda a: (a.reshape(B, T // GDN_CHUNK, GDN_CHUNK, 2, 4).transpose(0, 1, 3, 4, 2)
                           .reshape(B, T // GDN_CHUNK, 2, 1, QW))
    per_group = lambda a: jnp.repeat(a, GDN_DK).reshape(2, 1, QW)
    o_g, ssm_q = _gdn_prompt(g_qkv.reshape(B, T, GDN_QKV), g_z.reshape(B, T, H * GDN_DV),
                             quad_rows(small[:, _SM_GA:_SM_GA + H]), quad_rows(small[:, _SM_GB:_SM_GB + H]),
                             conv_w[0], per_group(alog), per_group(dtb), jnp.tile(gdn_norm[0], 4).reshape(1, QW),
                             min(512, T))
    o_g = o_g.reshape(B * T, H * GDN_DV)
    ssm_p = ssm_q.reshape(B, 2, GDN_DK, 4, GDN_DV).transpose(0, 1, 3, 2, 4).reshape(B, H, GDN_DK, GDN_DV)

    o_d = _dsa_prompt(i_q.reshape(B, T, -1), small.reshape(B, T, LANES), qh, kt, vh, min(TOPK_MAX, T // 4),
                      min(256, T))
    o_d = o_d.reshape(B * T, DSA_HEADS * DSA_DH)

    mk, mv = _norm_matmul(mem_prompt.reshape(B * M, D), norm_mem[0], w_mem_kv[0].astype(BF16),
                          (MEM_HEADS * MEM_DH,) * 2, min(512, B * M))
    o_m = _mem_prompt(m_q.reshape(B, T, -1), mk.reshape(B, M, -1), mv.reshape(B, M, -1), min(512, T))
    o_m = o_m.reshape(B * T, MEM_HEADS * MEM_DH)

    xres, h2 = _merge(xp, o_g, o_d, o_m, norm_attn[0], wg, wb, wo, norm_ffn[0], min(256, B * T))
    y_prompt = _moe_final(xres, h2, norm_final, wr_t, br, wgu, bgu, wdn, bdn, 256, min(512, B * T))

    k_prompt = d_k.reshape(1, B, T, DSA_KV_HEADS, DSA_DH)
    v_prompt = d_v.reshape(1, B, T, DSA_KV_HEADS, DSA_DH)
    kidx_prompt = small[:, _SM_IK:_SM_IK + IDX_DIM].reshape(1, B, T, IDX_DIM)
    memk_prompt = mk.reshape(1, B, M, MEM_HEADS, MEM_DH)
    memv_prompt = mv.reshape(1, B, M, MEM_HEADS, MEM_DH)
    conv_prompt = g_qkv.reshape(B, T, GDN_QKV)[:, T - (GDN_CONV - 1):, :][None]

    xs = x_sample.reshape(Bd, D)
    s_qkv, s_z, sd_q, sd_k, sd_v, si_q, sm_q, ssmall = _norm_matmul(xs, norm_attn[0], w1, _PROJ_SPLITS, Bd)

    lanes_b = lambda a: jnp.broadcast_to(a[..., None], a.shape + (Bd,))
    og_t, s_t = _gdn_sample(
        s_qkv.T, state_conv[0].transpose(1, 2, 0), lanes_b(conv_w[0]), s_z.T,
        ssmall[:, _SM_GA:_SM_GA + H].T.reshape(H, 1, Bd), ssmall[:, _SM_GB:_SM_GB + H].T.reshape(H, 1, Bd),
        lanes_b(alog.reshape(H, 1)), lanes_b(dtb.reshape(H, 1)), lanes_b(gdn_norm[0]),
        state_gdn[0].transpose(1, 2, 3, 0))
    so_g = og_t.T
    ssm_sample = s_t.transpose(3, 0, 1, 2)[None]
    conv_sample = jnp.concatenate([state_conv[0][:, 1:], s_qkv[:, None, :]], axis=1)[None]

    qi8 = jnp.pad(si_q.reshape(Bd, IDX_HEADS, IDX_DIM), ((0, 0), (0, 8 - IDX_HEADS), (0, 0)))
    wi8 = jnp.pad(ssmall[:, _SM_IW:_SM_IW + IDX_HEADS], ((0, 0), (0, 8 - IDX_HEADS)))[..., None]
    n_pg = 32 if n_pages % 32 == 0 else n_pages
    cache_kit = cache_k_idx[0].transpose(0, 2, 1)
    cache_kt = cache_k[0].transpose(0, 2, 3, 1)
    cache_vt = cache_v[0].transpose(0, 2, 3, 1)
    scores = _dsa_sample_scores(page_table, qi8, wi8, cache_kit)
    scores = scores.reshape(Bd, n_pages * PAGE_SIZE)
    n_sel = min(TOPK_MAX, (n_pages * PAGE_SIZE + 1) // 4)
    sel, sel_new = _dsa_sample_select(scores, si_q, ssmall, n_sel)
    over_lanes = lambda a: jnp.broadcast_to(a[..., None], a.shape + (PAGE_SIZE,))
    head_kv = jnp.arange(DSA_HEADS, dtype=I32) // G
    q8 = jnp.where((head_kv[None, :] == jnp.arange(DSA_KV_HEADS, dtype=I32)[:, None])[None, :, :, None],
                   sd_q.reshape(Bd, 1, DSA_HEADS, DSA_DH), 0.0)
    k_new = sd_k.reshape(Bd, DSA_KV_HEADS, DSA_DH)
    v_new = sd_v.reshape(Bd, DSA_KV_HEADS, DSA_DH)
    so_d = _dsa_sample_attn(page_table, q8, sel, sel_new, over_lanes(k_new), over_lanes(v_new),
                            cache_kt, cache_vt, n_pg).reshape(Bd, DSA_HEADS * DSA_DH)

    so_m = _mem_sample(sm_q.reshape(Bd, MEM_HEADS, MEM_DH), cache_mem_k[0], cache_mem_v[0], 4)
    so_m = so_m.reshape(Bd, MEM_HEADS * MEM_DH)

    sres, sh2 = _merge(xs, so_g, so_d, so_m, norm_attn[0], wg, wb, wo, norm_ffn[0], Bd)
    y_sample = _moe_final(sres, sh2, norm_final, wr_t, br, wgu, bgu, wdn, bdn, 256, Bd)

    return (y_prompt.reshape(B, T, D), y_sample.reshape(Bd, 1, D), k_prompt, v_prompt, kidx_prompt,
            memk_prompt, memv_prompt, ssm_p[None], conv_prompt,
            k_new.reshape(1, Bd, 1, DSA_KV_HEADS, DSA_DH), v_new.reshape(1, Bd, 1, DSA_KV_HEADS, DSA_DH),
            ssmall[:, _SM_IK:_SM_IK + IDX_DIM].reshape(1, Bd, 1, IDX_DIM), ssm_sample, conv_sample)
```
